```python
import math
import jax, jax.numpy as jnp
from jax import lax
import numpy as np

D_MODEL = 1024
BATCH = 2
SEQ = 8192
DEPTH = 2
DEC_BATCH = 128
DEC_SEQ = 1
PAST_LEN = 2048
PAGE_SIZE = 128

HEAD_DIM = 64
H_FOX = 6
H_DSA = 6
H_SB = 4
H_MEM = 4
N_MEM = 256
IDX_HEADS = 4
IDX_DIM = 64
DSA_TOPK_MAX = 256
ROPE_THETA = 500000.0
D_FF = (((8 * D_MODEL + 2) // 3 + 255) // 256) * 256
QBLOCK = 128
N_BRANCH = 4
EPS = 1e-6
FORGET_BIAS_OFFSET = 4.0

SPLIT_SIZES = (
    H_FOX * HEAD_DIM, H_FOX * HEAD_DIM, H_FOX * HEAD_DIM, H_FOX,
    H_DSA * HEAD_DIM, H_DSA * HEAD_DIM, H_DSA * HEAD_DIM,
    IDX_HEADS * IDX_DIM, IDX_DIM, IDX_HEADS,
    H_SB * HEAD_DIM, H_SB * HEAD_DIM, H_SB * HEAD_DIM,
    H_MEM * HEAD_DIM,
    N_BRANCH * D_MODEL,
)
N_IN = sum(SPLIT_SIZES)

kernel_name = "hybrid_fox_dsa_stickbreak_decoder_step"


def rms_norm(x, g):
    xf = x.astype(jnp.float32)
    y = xf * lax.rsqrt(jnp.mean(xf * xf, axis=-1, keepdims=True) + EPS)
    return (y * g.astype(jnp.float32)).astype(x.dtype)


def partial_rope(x, pos):
    rd = x.shape[-1] // 4
    half = rd // 2
    inv_freq = ROPE_THETA ** (-jnp.arange(half, dtype=jnp.float32) * 2.0 / rd)
    ang = pos.astype(jnp.float32)[:, None] * inv_freq[None, :]
    cos = jnp.cos(ang)[None, :, None, :]
    sin = jnp.sin(ang)[None, :, None, :]
    xr = x[..., :rd].astype(jnp.float32)
    x1, x2 = xr[..., :half], xr[..., half:]
    rot = jnp.concatenate([x1 * cos - x2 * sin, x2 * cos + x1 * sin], axis=-1)
    return jnp.concatenate([rot.astype(x.dtype), x[..., rd:]], axis=-1)


def sweep_query_blocks(block_fn, q_arrays, q_pos):
    T = q_pos.shape[0]
    qb = QBLOCK if T % QBLOCK == 0 else T
    nb = T // qb
    def split_blocks(a):
        return jnp.swapaxes(a.reshape(a.shape[0], nb, qb, *a.shape[2:]), 0, 1)
    xs = tuple(split_blocks(a) for a in q_arrays) + (q_pos.reshape(nb, qb),)
    out = lax.map(lambda blk: block_fn(*blk), xs)
    out = jnp.swapaxes(out, 0, 1)
    return out.reshape(out.shape[0], T, *out.shape[3:])


def fox_attention(q, k, v, logf):
    T, L = q.shape[1], k.shape[1]
    c = jnp.cumsum(logf.astype(jnp.float32), axis=1)
    cq = c[:, L - T:]
    cT = jnp.transpose(c, (0, 2, 1))
    kpos = jnp.arange(L)
    scale = HEAD_DIM ** -0.5
    def block(qb_, cqb, qpos):
        s = jnp.einsum('bqhd,bkhd->bhqk', qb_, k).astype(jnp.float32) * scale
        s = s + (jnp.transpose(cqb, (0, 2, 1))[..., None] - cT[:, :, None, :])
        s = jnp.where(kpos[None, None, None, :] <= qpos[None, None, :, None], s, -jnp.inf)
        p = jax.nn.softmax(s, axis=-1).astype(v.dtype)
        return jnp.einsum('bhqk,bkhd->bqhd', p, v)
    return sweep_query_blocks(block, (q, cq), (L - T) + jnp.arange(T))


def dsa_attention(q, k, v, iq, ik, iw, topk):
    T, L = q.shape[1], k.shape[1]
    kpos = jnp.arange(L)
    scale = HEAD_DIM ** -0.5
    gather_rows = jax.vmap(lambda kb, ib: kb[ib])
    def block(qb_, iqb, iwb, qpos):
        isc = jnp.einsum('bqhd,bkd->bqhk', iqb, ik).astype(jnp.float32)
        isc = jnp.einsum('bqh,bqhk->bqk', iwb.astype(jnp.float32), jax.nn.relu(isc))
        isc = jnp.where(kpos[None, None, :] <= qpos[None, :, None], isc, -jnp.inf)
        _, idx = lax.top_k(isc, topk)
        valid = idx <= qpos[None, :, None]
        k_sel = gather_rows(k, idx)
        v_sel = gather_rows(v, idx)
        s = jnp.einsum('bqhd,bqkhd->bhqk', qb_, k_sel).astype(jnp.float32) * scale
        s = jnp.where(valid[:, None], s, -jnp.inf)
        p = jax.nn.softmax(s, axis=-1).astype(v.dtype)
        return jnp.einsum('bhqk,bqkhd->bqhd', p, v_sel)
    return sweep_query_blocks(block, (q, iq, iw), (L - T) + jnp.arange(T))


def stick_breaking_attention(q, k, v):
    T, L = q.shape[1], k.shape[1]
    kpos = jnp.arange(L)
    scale = HEAD_DIM ** -0.5
    def block(qb_, qpos):
        z = jnp.einsum('bqhd,bkhd->bhqk', qb_, k).astype(jnp.float32) * scale
        strict = kpos[None, None, None, :] < qpos[None, None, :, None]
        log_beta = jax.nn.log_sigmoid(z)
        log_1mb = jnp.where(strict, log_beta - z, 0.0)
        tail = lax.cumsum(log_1mb, axis=3, reverse=True) - log_1mb
        a = jnp.where(strict, jnp.exp(log_beta + tail), 0.0)
        return jnp.einsum('bhqk,bkhd->bqhd', a.astype(v.dtype), v)
    return sweep_query_blocks(block, (q,), (L - T) + jnp.arange(T))


def memory_attention(q, mk, mv):
    s = jnp.einsum('bqhd,bmhd->bhqm', q, mk).astype(jnp.float32) * (HEAD_DIM ** -0.5)
    p = jax.nn.softmax(s, axis=-1).astype(mv.dtype)
    return jnp.einsum('bhqm,bmhd->bqhd', p, mv)


def token_projections(h, w_in, b_forget, q_offset):
    B, T, _ = h.shape
    pos = q_offset + jnp.arange(T, dtype=jnp.int32)
    z = jnp.einsum('btd,dn->btn', h, w_in)
    pts = np.cumsum(SPLIT_SIZES)[:-1].tolist()
    (fq, fk, fv, ff, dq, dk, dv, iq, ik, iw, sq, sk, sv, mq, gates) = jnp.split(z, pts, axis=-1)
    heads = lambda t, n: t.reshape(B, T, n, HEAD_DIM)
    return dict(
        fox_q=heads(fq, H_FOX),
        fox_kv=jnp.stack([heads(fk, H_FOX), heads(fv, H_FOX)], axis=2),
        fox_logf=jax.nn.log_sigmoid((ff + b_forget).astype(jnp.float32)).astype(h.dtype),
        dsa_q=partial_rope(heads(dq, H_DSA), pos),
        dsa_kv=jnp.stack([partial_rope(heads(dk, H_DSA), pos), heads(dv, H_DSA)], axis=2),
        idx_q=partial_rope(iq.reshape(B, T, IDX_HEADS, IDX_DIM), pos),
        dsa_idxk=partial_rope(ik[:, :, None, :], pos)[:, :, 0],
        idx_w=iw,
        sb_q=heads(sq, H_SB),
        sb_kv=jnp.stack([heads(sk, H_SB), heads(sv, H_SB)], axis=2),
        mem_q=heads(mq, H_MEM),
        gates=gates,
    )


def decoder_layer(x, mem_kv, past, q_offset, w_in, b_forget, w_br_fox, w_br_dsa, w_br_sb,
                  w_br_mem, w_out, w_ffn_in, w_ffn_out, g_mix_pre, g_mix_post, g_ffn_pre, g_ffn_post):
    B, T, _ = x.shape
    h = rms_norm(x, g_mix_pre)
    new = token_projections(h, w_in, b_forget, q_offset)
    rows = (new['fox_kv'], new['fox_logf'], new['dsa_kv'], new['dsa_idxk'], new['sb_kv'])
    if past is None:
        full = rows
    else:
        full = tuple(jnp.concatenate([a, b.astype(a.dtype)], axis=1) for a, b in zip(past, rows))
    fox_kv, fox_logf, dsa_kv, dsa_idxk, sb_kv = full
    L = fox_kv.shape[1]
    topk = min(DSA_TOPK_MAX, L // 4)
    o_fox = fox_attention(new['fox_q'], fox_kv[:, :, 0], fox_kv[:, :, 1], fox_logf)
    o_dsa = dsa_attention(new['dsa_q'], dsa_kv[:, :, 0], dsa_kv[:, :, 1],
                          new['idx_q'], dsa_idxk, new['idx_w'], topk)
    o_sb = stick_breaking_attention(new['sb_q'], sb_kv[:, :, 0], sb_kv[:, :, 1])
    o_mem = memory_attention(new['mem_q'], mem_kv[:, :, 0], mem_kv[:, :, 1])
    g = jax.nn.sigmoid(new['gates'].astype(jnp.float32)).astype(x.dtype).reshape(B, T, N_BRANCH, D_MODEL)
    merged = (g[:, :, 0] * (o_fox.reshape(B, T, -1) @ w_br_fox)
              + g[:, :, 1] * (o_dsa.reshape(B, T, -1) @ w_br_dsa)
              + g[:, :, 2] * (o_sb.reshape(B, T, -1) @ w_br_sb)
              + g[:, :, 3] * (o_mem.reshape(B, T, -1) @ w_br_mem))
    x = x + rms_norm(merged @ w_out, g_mix_post)
    gate, up = jnp.split(rms_norm(x, g_ffn_pre) @ w_ffn_in, 2, axis=-1)
    x = x + rms_norm((jax.nn.silu(gate) * up) @ w_ffn_out, g_ffn_post)
    return x, rows


def gather_pages(pool, page_table):
    g = pool[page_table]
    return g.reshape(g.shape[0], g.shape[1] * g.shape[2], *g.shape[3:])


def setup_inputs(seed: int = 0) -> dict:
    key = jax.random.key(seed)
    ks = jax.random.split(key, 24)
    n_pages = PAST_LEN // PAGE_SIZE
    used = DEC_BATCH * n_pages
    n_pool = (5 * used + 3) // 4
    nrm = lambda k, shape, s=1.0: jax.random.normal(k, shape, jnp.float32) * s
    page_table = jax.random.permutation(ks[10], n_pool)[:used].reshape(DEC_BATCH, n_pages).astype(jnp.int32)
    return {
        "x_prompt": nrm(ks[0], (BATCH, SEQ, D_MODEL)),
        "x_sample": nrm(ks[1], (DEC_BATCH, DEC_SEQ, D_MODEL)),
        "cache_fox_kv": nrm(ks[2], (DEPTH, n_pool, PAGE_SIZE, 2, H_FOX, HEAD_DIM)),
        "cache_fox_logf": jax.nn.log_sigmoid(FORGET_BIAS_OFFSET + nrm(ks[3], (DEPTH, n_pool, PAGE_SIZE, H_FOX))),
        "cache_dsa_kv": nrm(ks[4], (DEPTH, n_pool, PAGE_SIZE, 2, H_DSA, HEAD_DIM)),
        "cache_dsa_idxk": nrm(ks[5], (DEPTH, n_pool, PAGE_SIZE, IDX_DIM)),
        "cache_sb_kv": nrm(ks[6], (DEPTH, n_pool, PAGE_SIZE, 2, H_SB, HEAD_DIM)),
        "cache_mem_kv": nrm(ks[7], (DEPTH, DEC_BATCH, N_MEM, 2, H_MEM, HEAD_DIM)),
        "page_table": page_table,
        "mem_prompt": nrm(ks[8], (BATCH, N_MEM, D_MODEL)),
        "w_in": nrm(ks[9], (DEPTH, D_MODEL, N_IN), D_MODEL ** -0.5),
        "b_forget": FORGET_BIAS_OFFSET + nrm(ks[11], (DEPTH, H_FOX), 0.5),
        "w_mem_kv": nrm(ks[12], (DEPTH, D_MODEL, 2 * H_MEM * HEAD_DIM), D_MODEL ** -0.5),
        "w_br_fox": nrm(ks[13], (DEPTH, H_FOX * HEAD_DIM, D_MODEL), (H_FOX * HEAD_DIM) ** -0.5),
        "w_br_dsa": nrm(ks[14], (DEPTH, H_DSA * HEAD_DIM, D_MODEL), (H_DSA * HEAD_DIM) ** -0.5),
        "w_br_sb": nrm(ks[15], (DEPTH, H_SB * HEAD_DIM, D_MODEL), (H_SB * HEAD_DIM) ** -0.5),
        "w_br_mem": nrm(ks[16], (DEPTH, H_MEM * HEAD_DIM, D_MODEL), (H_MEM * HEAD_DIM) ** -0.5),
        "w_out": nrm(ks[17], (DEPTH, D_MODEL, D_MODEL), D_MODEL ** -0.5),
        "w_ffn_in": nrm(ks[18], (DEPTH, D_MODEL, 2 * D_FF), D_MODEL ** -0.5),
        "w_ffn_out": nrm(ks[19], (DEPTH, D_FF, D_MODEL), D_FF ** -0.5),
        "g_mix_pre": 1.0 + nrm(ks[20], (DEPTH, D_MODEL), 0.05),
        "g_mix_post": 1.0 + nrm(ks[21], (DEPTH, D_MODEL), 0.05),
        "g_ffn_pre": 1.0 + nrm(ks[22], (DEPTH, D_MODEL), 0.05),
        "g_ffn_post": 1.0 + nrm(ks[23], (DEPTH, D_MODEL), 0.05),
    }


def reference(x_prompt, x_sample, cache_fox_kv, cache_fox_logf, cache_dsa_kv, cache_dsa_idxk,
              cache_sb_kv, cache_mem_kv, page_table, mem_prompt, w_in, b_forget, w_mem_kv,
              w_br_fox, w_br_dsa, w_br_sb, w_br_mem, w_out, w_ffn_in, w_ffn_out,
              g_mix_pre, g_mix_post, g_ffn_pre, g_ffn_post):
    paged = (cache_fox_kv, cache_fox_logf, cache_dsa_kv, cache_dsa_idxk, cache_sb_kv)
    xp, xs = x_prompt, x_sample
    rows_p, rows_s, mem_p = [], [], []
    for l in range(DEPTH):
        lw = (w_in[l], b_forget[l], w_br_fox[l], w_br_dsa[l], w_br_sb[l], w_br_mem[l], w_out[l],
              w_ffn_in[l], w_ffn_out[l], g_mix_pre[l], g_mix_post[l], g_ffn_pre[l], g_ffn_post[l])
        mem_kv = jnp.einsum('bmd,dn->bmn', mem_prompt, w_mem_kv[l]).reshape(
            mem_prompt.shape[0], N_MEM, 2, H_MEM, HEAD_DIM)
        xp, r = decoder_layer(xp, mem_kv, None, 0, *lw)
        rows_p.append(r)
        mem_p.append(mem_kv)
        past = tuple(gather_pages(c[l], page_table) for c in paged)
        xs, r = decoder_layer(xs, cache_mem_kv[l], past, PAST_LEN, *lw)
        rows_s.append(r)
    stk = lambda rows, i: jnp.stack([r[i] for r in rows], axis=0)
    return (xp, xs,
            stk(rows_p, 0), stk(rows_p, 1), stk(rows_p, 2), stk(rows_p, 3), stk(rows_p, 4),
            jnp.stack(mem_p, axis=0),
            stk(rows_s, 0), stk(rows_s, 1), stk(rows_s, 2), stk(rows_s, 3), stk(rows_s, 4))
```

```python
import functools

import jax
import jax.numpy as jnp
import numpy as np
from jax import lax
from jax.experimental import pallas as pl
from jax.experimental.pallas import tpu as pltpu

HEAD_DIM = 64
H_FOX = 6
H_DSA = 6
H_SB = 4
H_MEM = 4
IDX_HEADS = 4
IDX_DIM = 64
DSA_TOPK_MAX = 256
ROPE_THETA = 500000.0
N_BRANCH = 4
EPS = 1e-6
PAGE_SIZE = 128

LANES = 128
VMEM_LIMIT = 56 * 1024 * 1024

F32 = jnp.float32
BF16 = jnp.bfloat16
NEG_INF = float("-inf")
INT_MIN = -2 ** 31

W_FOX = H_FOX * HEAD_DIM
W_DSA = H_DSA * HEAD_DIM
W_SB = H_SB * HEAD_DIM
W_MEM = H_MEM * HEAD_DIM
W_IDX = IDX_HEADS * IDX_DIM

C_FQ, C_FKV = 0, W_FOX
C_DQ = C_FKV + 2 * W_FOX
C_DK = C_DQ + W_DSA
C_DV = C_DK + W_DSA
C_IQ = C_DV + W_DSA
C_SQ = C_IQ + W_IDX
C_SKV = C_SQ + W_SB
C_MQ = C_SKV + 2 * W_SB
C_IK = C_MQ + W_MEM
C_MISC = C_IK + LANES
N_PROJ = C_MISC + LANES
MISC_IW = H_FOX


def _cparams(sem):
    return pltpu.CompilerParams(dimension_semantics=sem, vmem_limit_bytes=VMEM_LIMIT)


def _nt_dot(a, b):
    return lax.dot_general(a, b, (((1,), (1,)), ((), ())), preferred_element_type=F32)


def _softplus_tail(z):
    return jnp.log1p(jnp.exp(-jnp.abs(z)))


def _split3(x):
    hi = x.astype(BF16)
    r1 = x - hi.astype(F32)
    mid = r1.astype(BF16)
    lo = (r1 - mid.astype(F32)).astype(BF16)
    return hi, mid, lo


def _proj_kernel(x_ref, g_ref, w_ref, cos_ref, sa_ref, sb_ref, bf_ref,
                 fq_ref, fkv_ref, fk_ref, fv_ref, dq_ref, dkv_ref, dk_ref, dv_ref,
                 iq_ref, sq_ref, skv_ref, sk_ref, sv_ref, mq_ref, ik32_ref, ikb_ref, misc_ref):
    x = x_ref[...]
    h = x * lax.rsqrt(jnp.mean(x * x, axis=-1, keepdims=True) + EPS)
    hb = (h * g_ref[...]).astype(BF16)
    cosf, sa, sb = cos_ref[...], sa_ref[...], sb_ref[...]
    scale = HEAD_DIM ** -0.5

    def mm(c0, n):
        return jnp.dot(hb, w_ref[:, c0:c0 + n], preferred_element_type=F32)

    def rope(z):
        outs = []
        for j in range(z.shape[1] // LANES):
            zj = z[:, j * LANES:(j + 1) * LANES]
            outs.append(zj * cosf + pltpu.roll(zj, LANES - 8, 1) * sa + pltpu.roll(zj, 8, 1) * sb)
        return outs[0] if len(outs) == 1 else jnp.concatenate(outs, axis=1)

    fq_ref[...] = (mm(C_FQ, W_FOX) * scale).astype(BF16)
    fkv = mm(C_FKV, 2 * W_FOX)
    fkv_ref[...] = fkv
    fk_ref[...] = fkv[:, :W_FOX].astype(BF16)
    fv_ref[...] = fkv[:, W_FOX:].astype(BF16)

    dq_ref[...] = (rope(mm(C_DQ, W_DSA)) * scale).astype(BF16)
    dk = rope(mm(C_DK, W_DSA))
    dv = mm(C_DV, W_DSA)
    dkv_ref[:, :W_DSA] = dk
    dkv_ref[:, W_DSA:] = dv
    dk_ref[...] = dk.astype(BF16)
    dv_ref[...] = dv.astype(BF16)

    iq_ref[...] = rope(mm(C_IQ, W_IDX)).astype(BF16)

    sq_ref[...] = (mm(C_SQ, W_SB) * scale).astype(BF16)
    skv = mm(C_SKV, 2 * W_SB)
    skv_ref[...] = skv
    sk_ref[...] = skv[:, :W_SB].astype(BF16)
    sv_ref[...] = skv[:, W_SB:].astype(BF16)

    mq_ref[...] = (mm(C_MQ, W_MEM) * scale).astype(BF16)

    ik = rope(mm(C_IK, LANES))
    ik32_ref[...] = ik
    ikb_ref[...] = ik.astype(BF16)

    zm = mm(C_MISC, LANES)
    ff = zm + bf_ref[...]
    logf = -(jnp.maximum(-ff, 0.0) + _softplus_tail(ff))
    lane = lax.broadcasted_iota(jnp.int32, zm.shape, 1)
    misc_ref[...] = jnp.where(lane < H_FOX, logf, zm)


def _proj(x, g, w, tabs, bfp, tm):
    n, d = x.shape
    cosf, sa, sb = tabs
    nt = cosf.shape[0] // tm
    row = lambda i: (i, 0)
    tab = lambda i: (i % nt, 0)
    const = lambda i: (0, 0)
    widths = [(W_FOX, BF16), (2 * W_FOX, F32), (W_FOX, BF16), (W_FOX, BF16),
              (W_DSA, BF16), (2 * W_DSA, F32), (W_DSA, BF16), (W_DSA, BF16),
              (W_IDX, BF16), (W_SB, BF16), (2 * W_SB, F32), (W_SB, BF16), (W_SB, BF16),
              (W_MEM, BF16), (LANES, F32), (LANES, BF16), (LANES, F32)]
    return pl.pallas_call(
        _proj_kernel,
        grid=(n // tm,),
        in_specs=[pl.BlockSpec((tm, d), row), pl.BlockSpec((1, d), const),
                  pl.BlockSpec((d, N_PROJ), const),
                  pl.BlockSpec((tm, LANES), tab), pl.BlockSpec((tm, LANES), tab),
                  pl.BlockSpec((tm, LANES), tab), pl.BlockSpec((1, LANES), const)],
        out_specs=[pl.BlockSpec((tm, wd), row) for wd, _ in widths],
        out_shape=[jax.ShapeDtypeStruct((n, wd), dt) for wd, dt in widths],
        compiler_params=_cparams(("parallel",)),
    )(x, g, w, cosf, sa, sb, bfp)


def _matmul_kernel(a_ref, b_ref, o_ref):
    o_ref[...] = jnp.dot(a_ref[...].astype(BF16), b_ref[...], preferred_element_type=F32)


def _matmul(a, b):
    return pl.pallas_call(
        _matmul_kernel,
        out_shape=jax.ShapeDtypeStruct((a.shape[0], b.shape[1]), F32),
        compiler_params=pltpu.CompilerParams(vmem_limit_bytes=VMEM_LIMIT),
    )(a, b)


def _pair_masks(rows):
    lane = lax.broadcasted_iota(jnp.int32, (rows, LANES), 1)
    lo = (lane < HEAD_DIM).astype(F32)
    return lo, 1.0 - lo


def _fox_kernel(q_ref, k_ref, v_ref, ccol_ref, crow_ref, o_ref, *, t):
    qi = pl.program_id(2)
    q2 = q_ref[...].astype(F32)
    lo, hi = _pair_masks(t)
    ccol = ccol_ref[...]
    row = lax.broadcasted_iota(jnp.int32, (t, t), 0)
    col = lax.broadcasted_iota(jnp.int32, (t, t), 1)
    causal = col <= row

    def head(qh, cq, r):
        def chunk(j, carry, diag):
            m, l, acc = carry
            k0 = pl.multiple_of(j * t, t)
            s = _nt_dot(qh, k_ref[pl.ds(k0, t), :])
            s = s + (cq - crow_ref[0, 0, r:r + 1, pl.ds(k0, t)])
            if diag:
                s = jnp.where(causal, s, NEG_INF)
            m_new = jnp.maximum(m, jnp.max(s, axis=-1, keepdims=True))
            alpha = jnp.exp(m - m_new)
            p = jnp.exp(s - m_new)
            l = alpha * l + jnp.sum(p, axis=-1, keepdims=True)
            acc = alpha * acc + jnp.dot(p.astype(BF16), v_ref[pl.ds(k0, t), :],
                                        preferred_element_type=F32)
            return m_new, l, acc

        init = (jnp.full((t, 1), NEG_INF, F32), jnp.zeros((t, 1), F32), jnp.zeros((t, LANES), F32))
        carry = lax.fori_loop(0, qi, lambda j, c: chunk(j, c, False), init)
        _, l, acc = chunk(qi, carry, True)
        return acc / l

    oa = head((q2 * lo).astype(BF16), ccol[:, 0:1], 0)
    ob = head((q2 * hi).astype(BF16), ccol[:, HEAD_DIM:HEAD_DIM + 1], 1)
    o_ref[...] = (oa * lo + ob * hi).astype(BF16)


def _fox_attention(q, k, v, ccol, crow, b, t_len, t):
    npair = H_FOX // 2
    nq = t_len // t
    return pl.pallas_call(
        functools.partial(_fox_kernel, t=t),
        grid=(b, npair, nq),
        in_specs=[pl.BlockSpec((t, LANES), lambda bi, p, i: (bi * nq + i, p)),
                  pl.BlockSpec((t_len, LANES), lambda bi, p, i: (bi, p)),
                  pl.BlockSpec((t_len, LANES), lambda bi, p, i: (bi, p)),
                  pl.BlockSpec((t, LANES), lambda bi, p, i: (bi * nq + i, p)),
                  pl.BlockSpec((1, 1, 8, t_len), lambda bi, p, i: (bi, p, 0, 0))],
        out_specs=pl.BlockSpec((t, LANES), lambda bi, p, i: (bi * nq + i, p)),
        out_shape=jax.ShapeDtypeStruct(q.shape, BF16),
        compiler_params=_cparams(("parallel", "parallel", "arbitrary")),
    )(q, k, v, ccol, crow)


def _sb_kernel(q_ref, k_ref, v_ref, o_ref, *, t):
    qi = pl.program_id(2)
    q2 = q_ref[...].astype(F32)
    lo, hi = _pair_masks(t)
    row = lax.broadcasted_iota(jnp.int32, (t, t), 0)
    col = lax.broadcasted_iota(jnp.int32, (t, t), 1)
    strict = col < row
    after = (row > col).astype(BF16)

    def head(qh):
        def chunk(j, carry, diag):
            run, acc = carry
            k0 = pl.multiple_of(j * t, t)
            z = _nt_dot(qh, k_ref[pl.ds(k0, t), :])
            tl = _softplus_tail(z)
            log_beta = jnp.minimum(z, 0.0) - tl
            log_1mb = -jnp.maximum(z, 0.0) - tl
            if diag:
                log_1mb = jnp.where(strict, log_1mb, 0.0)
            h1, h2, h3 = _split3(log_1mb)
            tail = (jnp.dot(h1, after, preferred_element_type=F32)
                    + jnp.dot(h2, after, preferred_element_type=F32)
                    + jnp.dot(h3, after, preferred_element_type=F32))
            a = jnp.exp(log_beta + (tail + run))
            if diag:
                a = jnp.where(strict, a, 0.0)
            acc = acc + jnp.dot(a.astype(BF16), v_ref[pl.ds(k0, t), :], preferred_element_type=F32)
            run = run + (tail[:, 0:1] + log_1mb[:, 0:1])
            return run, acc

        carry = chunk(qi, (jnp.zeros((t, 1), F32), jnp.zeros((t, LANES), F32)), True)
        _, acc = lax.fori_loop(0, qi, lambda i, c: chunk(qi - 1 - i, c, False), carry)
        return acc

    oa = head((q2 * lo).astype(BF16))
    ob = head((q2 * hi).astype(BF16))
    o_ref[...] = (oa * lo + ob * hi).astype(BF16)


def _sb_attention(q, k, v, b, t_len, t):
    npair = H_SB // 2
    nq = t_len // t
    return pl.pallas_call(
        functools.partial(_sb_kernel, t=t),
        grid=(b, npair, nq),
        in_specs=[pl.BlockSpec((t, LANES), lambda bi, p, i: (bi * nq + i, p)),
                  pl.BlockSpec((t_len, LANES), lambda bi, p, i: (bi, p)),
                  pl.BlockSpec((t_len, LANES), lambda bi, p, i: (bi, p))],
        out_specs=pl.BlockSpec((t, LANES), lambda bi, p, i: (bi * nq + i, p)),
        out_shape=jax.ShapeDtypeStruct(q.shape, BF16),
        compiler_params=_cparams(("parallel", "parallel", "arbitrary")),
    )(q, k, v)


def _sort_key(x):
    x = jnp.where(x == 0.0, 0.0, x)
    bits = pltpu.bitcast(x, jnp.int32)
    return bits ^ ((bits >> 31) & 0x7FFFFFFF)


KEY_NEG_INF = int(np.int32(np.array(-np.inf, np.float32).view(np.int32)) ^ np.int32(0x7FFFFFFF))


def _select_topk_bias(key_ref, nch, tk, rows, topk):
    def fold(mask_f):
        part = mask_f[:, 0:LANES]
        for u in range(1, tk // LANES):
            part = part + mask_f[:, u * LANES:(u + 1) * LANES]
        return part

    def count(pred):
        def body(c, acc):
            k0 = pl.multiple_of(c * tk, tk)
            blk = key_ref[:, pl.ds(k0, tk)]
            return acc + fold(jnp.where(pred(blk, k0), 1.0, 0.0))
        acc = lax.fori_loop(0, nch, body, jnp.zeros((rows, LANES), F32))
        return jnp.sum(acc, axis=-1, keepdims=True)

    kf = float(topk)

    def value_bit(i, ans):
        cand = ans | jnp.left_shift(jnp.int32(1), 31 - i)
        cand_s = cand ^ INT_MIN
        cnt = count(lambda blk, k0: blk >= cand_s)
        return jnp.where(cnt >= kf, cand, ans)

    ans = lax.fori_loop(0, 32, value_bit, jnp.zeros((rows, 1), jnp.int32))
    thr = ans ^ INT_MIN
    cnt_gt = count(lambda blk, k0: blk > thr)
    cnt_ge = count(lambda blk, k0: blk >= thr)
    need = kf - cnt_gt
    tie = jnp.logical_and(cnt_ge - cnt_gt > need, thr != KEY_NEG_INF)
    any_tie = jnp.max(jnp.where(tie, 1.0, 0.0)) > 0.0

    nbits = max(1, int(np.ceil(np.log2(key_ref.shape[1] + 1))))

    def col_index(k0):
        return k0 + lax.broadcasted_iota(jnp.int32, (rows, tk), 1)

    def tie_cut():
        def index_bit(i, x):
            cand = x | jnp.left_shift(jnp.int32(1), nbits - 1 - i)
            cnt = count(lambda blk, k0: jnp.logical_and(blk == thr, col_index(k0) < cand))
            return jnp.where(cnt < need, cand, x)
        return lax.fori_loop(0, nbits, index_bit, jnp.zeros((rows, 1), jnp.int32))

    cut = lax.cond(any_tie, tie_cut, lambda: jnp.full((rows, 1), 2 ** 30, jnp.int32))

    def write(c, _):
        k0 = pl.multiple_of(c * tk, tk)
        blk = key_ref[:, pl.ds(k0, tk)]
        sel = jnp.logical_or(blk > thr, jnp.logical_and(blk == thr, col_index(k0) <= cut))
        key_ref[:, pl.ds(k0, tk)] = pltpu.bitcast(jnp.where(sel, 0.0, NEG_INF), jnp.int32)
        return 0

    lax.fori_loop(0, nch, write, 0)


def _dsa_kernel(q_ref, iq_ref, misc_ref, ik_ref, k_ref, v_ref, o_ref, key_ref, *, tq, tk, topk):
    qi = pl.program_id(1)
    q0 = qi * tq
    jd = q0 // tk
    nch = jd + 1
    lo, hi = _pair_masks(tq)
    row_pos = q0 + lax.broadcasted_iota(jnp.int32, (tq, tk), 0)
    col_in = lax.broadcasted_iota(jnp.int32, (tq, tk), 1)

    misc = misc_ref[...]
    iq = iq_ref[...].astype(F32)
    iq_heads = []
    for hh in range(IDX_HEADS):
        pair = iq[:, (hh // 2) * LANES:(hh // 2 + 1) * LANES]
        iq_heads.append((pair * (lo if hh % 2 == 0 else hi)).astype(BF16))
    iws = [misc[:, MISC_IW + hh:MISC_IW + hh + 1] for hh in range(IDX_HEADS)]

    def score_chunk(c, diag):
        k0 = pl.multiple_of(c * tk, tk)
        ikc = ik_ref[pl.ds(k0, tk), :]
        isc = jnp.zeros((tq, tk), F32)
        for hh in range(IDX_HEADS):
            isc = isc + iws[hh] * jnp.maximum(_nt_dot(iq_heads[hh], ikc), 0.0)
        if diag:
            isc = jnp.where(k0 + col_in <= row_pos, isc, NEG_INF)
        key_ref[:, pl.ds(k0, tk)] = _sort_key(isc)

    def score_body(c, _):
        score_chunk(c, False)
        return 0

    lax.fori_loop(0, jd, score_body, 0)
    score_chunk(jd, True)

    _select_topk_bias(key_ref, nch, tk, tq, topk)

    outs = []
    for p in range(H_DSA // 2):
        q2 = q_ref[:, p * LANES:(p + 1) * LANES].astype(F32)

        def head(qh):
            def chunk(c, carry, diag):
                m, l, acc = carry
                k0 = pl.multiple_of(c * tk, tk)
                s = _nt_dot(qh, k_ref[pl.ds(k0, tk), p * LANES:(p + 1) * LANES])
                s = s + pltpu.bitcast(key_ref[:, pl.ds(k0, tk)], F32)
                if diag:
                    s = jnp.where(k0 + col_in <= row_pos, s, NEG_INF)
                m_new = jnp.maximum(m, jnp.max(s, axis=-1, keepdims=True))
                m_safe = jnp.where(m_new == NEG_INF, 0.0, m_new)
                alpha = jnp.exp(m - m_safe)
                pr = jnp.exp(s - m_safe)
                l = alpha * l + jnp.sum(pr, axis=-1, keepdims=True)
                acc = alpha * acc + jnp.dot(pr.astype(BF16),
                                            v_ref[pl.ds(k0, tk), p * LANES:(p + 1) * LANES],
                                            preferred_element_type=F32)
                return m_new, l, acc

            init = (jnp.full((tq, 1), NEG_INF, F32), jnp.zeros((tq, 1), F32),
                    jnp.zeros((tq, LANES), F32))
            carry = lax.fori_loop(0, jd, lambda c, cr: chunk(c, cr, False), init)
            _, l, acc = chunk(jd, carry, True)
            return acc / l

        oa = head((q2 * lo).astype(BF16))
        ob = head((q2 * hi).astype(BF16))
        outs.append(oa * lo + ob * hi)
    o_ref[...] = jnp.concatenate(outs, axis=1).astype(BF16)


def _dsa_attention(q, iq, misc, ikb, k, v, b, t_len, tq, tk, topk):
    nq = t_len // tq
    rowblk = lambda bi, i: (bi * nq + i, 0)
    per_b = lambda bi, i: (bi, 0)
    return pl.pallas_call(
        functools.partial(_dsa_kernel, tq=tq, tk=tk, topk=topk),
        grid=(b, nq),
        in_specs=[pl.BlockSpec((tq, W_DSA), rowblk), pl.BlockSpec((tq, W_IDX), rowblk),
                  pl.BlockSpec((tq, LANES), rowblk), pl.BlockSpec((t_len, LANES), per_b),
                  pl.BlockSpec((t_len, W_DSA), per_b), pl.BlockSpec((t_len, W_DSA), per_b)],
        out_specs=pl.BlockSpec((tq, W_DSA), rowblk),
        out_shape=jax.ShapeDtypeStruct(q.shape, BF16),
        scratch_shapes=[pltpu.VMEM((tq, t_len), jnp.int32)],
        compiler_params=_cparams(("parallel", "arbitrary")),
    )(q, iq, misc, ikb, k, v)


def _mem_kernel(q_ref, mkv_ref, o_ref, *, tq):
    lo, hi = _pair_masks(tq)
    outs = []
    for p in range(H_MEM // 2):
        q2 = q_ref[:, p * LANES:(p + 1) * LANES].astype(F32)
        mk = mkv_ref[0, :, p * LANES:(p + 1) * LANES].astype(BF16)
        mv = mkv_ref[0, :, W_MEM + p * LANES:W_MEM + (p + 1) * LANES].astype(BF16)

        def head(qh):
            s = _nt_dot(qh, mk)
            pr = jnp.exp(s - jnp.max(s, axis=-1, keepdims=True))
            l = jnp.sum(pr, axis=-1, keepdims=True)
            return jnp.dot(pr.astype(BF16), mv, preferred_element_type=F32) / l

        outs.append(head((q2 * lo).astype(BF16)) * lo + head((q2 * hi).astype(BF16)) * hi)
    o_ref[...] = jnp.concatenate(outs, axis=1).astype(BF16)


def _mem_attention(q, mkv, b, t_len, tq):
    nq = t_len // tq
    n_mem = mkv.shape[1]
    return pl.pallas_call(
        functools.partial(_mem_kernel, tq=tq),
        grid=(b, nq),
        in_specs=[pl.BlockSpec((tq, W_MEM), lambda bi, i: (bi * nq + i, 0)),
                  pl.BlockSpec((1, n_mem, 2 * W_MEM), lambda bi, i: (bi, 0, 0))],
        out_specs=pl.BlockSpec((tq, W_MEM), lambda bi, i: (bi * nq + i, 0)),
        out_shape=jax.ShapeDtypeStruct(q.shape, BF16),
        compiler_params=_cparams(("parallel", "arbitrary")),
    )(q, mkv)


def _rms(x, g):
    return x * lax.rsqrt(jnp.mean(x * x, axis=-1, keepdims=True) + EPS) * g


def _merge_kernel(x_ref, of_ref, od_ref, os_ref, om_ref, gpre_ref, gpost_ref, wg_ref,
                  wf_ref, wd_ref, ws_ref, wm_ref, wo_ref, y_ref):
    x = x_ref[...]
    d = x.shape[1]
    hb = _rms(x, gpre_ref[...]).astype(BF16)
    merged = None
    for i, (o_ref, w_ref) in enumerate(((of_ref, wf_ref), (od_ref, wd_ref),
                                        (os_ref, ws_ref), (om_ref, wm_ref))):
        gate = jax.nn.sigmoid(jnp.dot(hb, wg_ref[:, i * d:(i + 1) * d], preferred_element_type=F32))
        br = jnp.dot(o_ref[...].astype(BF16), w_ref[...], preferred_element_type=F32)
        merged = gate * br if merged is None else merged + gate * br
    y = jnp.dot(merged.astype(BF16), wo_ref[...], preferred_element_type=F32)
    y_ref[...] = x + _rms(y, gpost_ref[...])


def _merge(x, o_fox, o_dsa, o_sb, o_mem, gpre, gpost, wg, wf, wd, ws, wm, wo, tm):
    n, d = x.shape
    row = lambda i: (i, 0)
    const = lambda i: (0, 0)
    full = lambda a: pl.BlockSpec(a.shape, const)
    return pl.pallas_call(
        _merge_kernel,
        grid=(n // tm,),
        in_specs=[pl.BlockSpec((tm, d), row)]
        + [pl.BlockSpec((tm, o.shape[1]), row) for o in (o_fox, o_dsa, o_sb, o_mem)]
        + [full(a) for a in (gpre, gpost, wg, wf, wd, ws, wm, wo)],
        out_specs=pl.BlockSpec((tm, d), row),
        out_shape=jax.ShapeDtypeStruct((n, d), F32),
        compiler_params=_cparams(("parallel",)),
    )(x, o_fox, o_dsa, o_sb, o_mem, gpre, gpost, wg, wf, wd, ws, wm, wo)


def _ffn_kernel(x_ref, gpre_ref, gpost_ref, wi_ref, wo_ref, y_ref, *, d_ff, tc):
    x = x_ref[...]
    hb = _rms(x, gpre_ref[...]).astype(BF16)
    y = jnp.zeros(x.shape, F32)
    for c0 in range(0, d_ff, tc):
        gate = jnp.dot(hb, wi_ref[:, c0:c0 + tc], preferred_element_type=F32)
        up = jnp.dot(hb, wi_ref[:, d_ff + c0:d_ff + c0 + tc], preferred_element_type=F32)
        act = (gate * jax.nn.sigmoid(gate)) * up
        y = y + jnp.dot(act.astype(BF16), wo_ref[c0:c0 + tc, :], preferred_element_type=F32)
    y_ref[...] = x + _rms(y, gpost_ref[...])


def _ffn(x, gpre, gpost, wi, wo, tm):
    n, d = x.shape
    d_ff = wo.shape[0]
    tc = 256 if d_ff % 256 == 0 else d_ff
    row = lambda i: (i, 0)
    const = lambda i: (0, 0)
    return pl.pallas_call(
        functools.partial(_ffn_kernel, d_ff=d_ff, tc=tc),
        grid=(n // tm,),
        in_specs=[pl.BlockSpec((tm, d), row), pl.BlockSpec(gpre.shape, const),
                  pl.BlockSpec(gpost.shape, const), pl.BlockSpec(wi.shape, const),
                  pl.BlockSpec(wo.shape, const)],
        out_specs=pl.BlockSpec((tm, d), row),
        out_shape=jax.ShapeDtypeStruct((n, d), F32),
        compiler_params=_cparams(("parallel",)),
    )(x, gpre, gpost, wi, wo)


def _dec_score_kernel(pt_ref, iqh_ref, iwb_ref, iknew_ref, *rest, npages):
    pages = rest[:npages]
    o_ref = rest[npages]
    iqh = iqh_ref[0]
    iwb = iwb_ref[0]
    for j in range(npages):
        s = _nt_dot(iqh, pages[j][0].astype(BF16))
        o_ref[0, :, j * LANES:(j + 1) * LANES] = jnp.sum(iwb * jnp.maximum(s, 0.0), axis=0,
                                                         keepdims=True)
    ik_new = iknew_ref[0][:, :IDX_DIM].astype(BF16).astype(F32)
    s_new = jnp.sum(iqh.astype(F32) * ik_new, axis=-1, keepdims=True)
    isc_new = jnp.sum(iwb[:, 0:1] * jnp.maximum(s_new, 0.0), axis=0, keepdims=True)
    lane = lax.broadcasted_iota(jnp.int32, (1, LANES), 1)
    o_ref[0, :, npages * LANES:] = jnp.where(lane == 0, isc_new, NEG_INF)


def _dec_scores(pt_flat, iqh, iwb, ik_new, idxk_pool, nb, npages):
    width = (npages + 1) * LANES
    seq = lambda bi, pt: (bi, 0, 0)

    def page_map(j):
        return lambda bi, pt: (pt[bi * npages + j], 0, 0)

    grid_spec = pltpu.PrefetchScalarGridSpec(
        num_scalar_prefetch=1, grid=(nb,),
        in_specs=[pl.BlockSpec((1, 8, IDX_DIM), seq), pl.BlockSpec((1, 8, LANES), seq),
                  pl.BlockSpec((1, 1, LANES), seq)]
        + [pl.BlockSpec((1, PAGE_SIZE, IDX_DIM), page_map(j)) for j in range(npages)],
        out_specs=pl.BlockSpec((1, 1, width), seq))
    return pl.pallas_call(
        functools.partial(_dec_score_kernel, npages=npages),
        grid_spec=grid_spec,
        out_shape=jax.ShapeDtypeStruct((nb, 1, width), F32),
        compiler_params=_cparams(("arbitrary",)),
    )(pt_flat, iqh, iwb, ik_new, *([idxk_pool] * npages))


def _dec_select_kernel(isc_ref, o_ref, key_ref, *, topk):
    rows, width = isc_ref.shape
    key_ref[...] = _sort_key(isc_ref[...])
    _select_topk_bias(key_ref, width // LANES, LANES, rows, topk)
    o_ref[...] = pltpu.bitcast(key_ref[...], F32)


def _dec_select(isc, topk):
    return pl.pallas_call(
        functools.partial(_dec_select_kernel, topk=topk),
        out_shape=jax.ShapeDtypeStruct(isc.shape, F32),
        scratch_shapes=[pltpu.VMEM(isc.shape, jnp.int32)],
        compiler_params=pltpu.CompilerParams(vmem_limit_bytes=VMEM_LIMIT),
    )(isc)


def _head_rows(width):
    sub = lax.broadcasted_iota(jnp.int32, (8, width), 0)
    lane = lax.broadcasted_iota(jnp.int32, (8, width), 1)
    return ((lane >> 6) == sub).astype(F32)


def _dec_attn_kernel(pt_ref, fq_ref, dq_ref, sq_ref, mq_ref, fnew_ref, dnew_ref, lnew_ref,
                     bias_ref, mem_ref, *rest, npages):
    fox_pages = rest[0:npages]
    logf_pages = rest[npages:2 * npages]
    dsa_pages = rest[2 * npages:3 * npages]
    sb_pages = rest[3 * npages:4 * npages]
    of_ref, od_ref, os_ref, om_ref = rest[4 * npages:]

    row = lax.broadcasted_iota(jnp.int32, (PAGE_SIZE, PAGE_SIZE), 0)
    col = lax.broadcasted_iota(jnp.int32, (PAGE_SIZE, PAGE_SIZE), 1)
    after = (row > col).astype(BF16)

    def suffix_after(x):
        h1, h2, h3 = _split3(x)
        return (jnp.dot(h1, after, preferred_element_type=F32)
                + jnp.dot(h2, after, preferred_element_type=F32)
                + jnp.dot(h3, after, preferred_element_type=F32))

    def head_diag(o, mask):
        return jnp.sum(o * mask, axis=0, keepdims=True)

    mask_f = _head_rows(W_FOX)
    qf = (fq_ref[0].astype(F32) * mask_f).astype(BF16)
    k_new = fnew_ref[0][:, :W_FOX].astype(BF16).astype(F32)
    v_new = fnew_ref[0][:, W_FOX:].astype(BF16).astype(F32)
    s_new = jnp.sum(qf.astype(F32) * k_new, axis=-1, keepdims=True)
    run = lnew_ref[0][:, 0:1]
    scores = [None] * npages
    for j in range(npages - 1, -1, -1):
        kp = fox_pages[j][0][:, :W_FOX].astype(BF16)
        lf = logf_pages[j][0]
        scores[j] = _nt_dot(qf, kp) + (suffix_after(lf) + run)
        run = run + jnp.sum(lf, axis=-1, keepdims=True)
    m = s_new
    for j in range(npages):
        m = jnp.maximum(m, jnp.max(scores[j], axis=-1, keepdims=True))
    p_new = jnp.exp(s_new - m)
    l = p_new
    acc = p_new.astype(BF16).astype(F32) * v_new
    for j in range(npages):
        pr = jnp.exp(scores[j] - m)
        l = l + jnp.sum(pr, axis=-1, keepdims=True)
        acc = acc + jnp.dot(pr.astype(BF16), fox_pages[j][0][:, W_FOX:].astype(BF16),
                            preferred_element_type=F32)
    of_ref[0] = head_diag(acc / l, mask_f)

    mask_d = _head_rows(W_DSA)
    qd = (dq_ref[0].astype(F32) * mask_d).astype(BF16)
    k_new = dnew_ref[0][:, :W_DSA].astype(BF16).astype(F32)
    v_new = dnew_ref[0][:, W_DSA:].astype(BF16).astype(F32)
    s_new = (jnp.sum(qd.astype(F32) * k_new, axis=-1, keepdims=True)
             + bias_ref[0][:, npages * LANES:npages * LANES + 1])
    for j in range(npages):
        kp = dsa_pages[j][0][:, :W_DSA].astype(BF16)
        scores[j] = _nt_dot(qd, kp) + bias_ref[0][:, j * LANES:(j + 1) * LANES]
    m = s_new
    for j in range(npages):
        m = jnp.maximum(m, jnp.max(scores[j], axis=-1, keepdims=True))
    p_new = jnp.exp(s_new - m)
    l = p_new
    acc = p_new.astype(BF16).astype(F32) * v_new
    for j in range(npages):
        pr = jnp.exp(scores[j] - m)
        l = l + jnp.sum(pr, axis=-1, keepdims=True)
        acc = acc + jnp.dot(pr.astype(BF16), dsa_pages[j][0][:, W_DSA:].astype(BF16),
                            preferred_element_type=F32)
    od_ref[0] = head_diag(acc / l, mask_d)

    mask_s = _head_rows(W_SB)
    qs = (sq_ref[0].astype(F32) * mask_s).astype(BF16)
    run = jnp.zeros((8, 1), F32)
    acc = jnp.zeros((8, W_SB), F32)
    for j in range(npages - 1, -1, -1):
        z = _nt_dot(qs, sb_pages[j][0][:, :W_SB].astype(BF16))
        tl = _softplus_tail(z)
        log_beta = jnp.minimum(z, 0.0) - tl
        log_1mb = -jnp.maximum(z, 0.0) - tl
        a = jnp.exp(log_beta + (suffix_after(log_1mb) + run))
        acc = acc + jnp.dot(a.astype(BF16), sb_pages[j][0][:, W_SB:].astype(BF16),
                            preferred_element_type=F32)
        run = run + jnp.sum(log_1mb, axis=-1, keepdims=True)
    os_ref[0] = head_diag(acc, mask_s)

    mask_m = _head_rows(W_MEM)
    qm = (mq_ref[0].astype(F32) * mask_m).astype(BF16)
    s = _nt_dot(qm, mem_ref[0][:, :W_MEM].astype(BF16))
    pr = jnp.exp(s - jnp.max(s, axis=-1, keepdims=True))
    l = jnp.sum(pr, axis=-1, keepdims=True)
    o = jnp.dot(pr.astype(BF16), mem_ref[0][:, W_MEM:].astype(BF16), preferred_element_type=F32)
    om_ref[0] = head_diag(o / l, mask_m)


def _dec_attention(pt_flat, fq, dq, sq, mq, fnew, dnew, lnew, bias, mem, fox_pool, logf_pool,
                   dsa_pool, sb_pool, nb, npages):
    seq = lambda bi, pt: (bi, 0, 0)

    def page_map(j):
        return lambda bi, pt: (pt[bi * npages + j], 0, 0)

    def pages(width, rows=PAGE_SIZE):
        return [pl.BlockSpec((1, rows, width), page_map(j)) for j in range(npages)]

    def seq_spec(a):
        return pl.BlockSpec((1,) + a.shape[1:], seq)

    grid_spec = pltpu.PrefetchScalarGridSpec(
        num_scalar_prefetch=1, grid=(nb,),
        in_specs=[seq_spec(a) for a in (fq, dq, sq, mq, fnew, dnew, lnew, bias, mem)]
        + pages(2 * W_FOX) + pages(LANES, 8) + pages(2 * W_DSA) + pages(2 * W_SB),
        out_specs=[pl.BlockSpec((1, 1, wd), seq) for wd in (W_FOX, W_DSA, W_SB, W_MEM)])
    return pl.pallas_call(
        functools.partial(_dec_attn_kernel, npages=npages),
        grid_spec=grid_spec,
        out_shape=[jax.ShapeDtypeStruct((nb, 1, wd), F32) for wd in (W_FOX, W_DSA, W_SB, W_MEM)],
        compiler_params=_cparams(("arbitrary",)),
    )(pt_flat, fq, dq, sq, mq, fnew, dnew, lnew, bias, mem,
      *([fox_pool] * npages), *([logf_pool] * npages), *([dsa_pool] * npages),
      *([sb_pool] * npages))


def _rope_tables(pos):
    rd = HEAD_DIM // 4
    half = rd // 2
    inv_freq = ROPE_THETA ** (-jnp.arange(half, dtype=F32) * 2.0 / rd)
    ang = pos.astype(F32)[:, None] * inv_freq[None, :]
    cos, sin = jnp.cos(ang), jnp.sin(ang)
    n = pos.shape[0]
    one = jnp.ones((n, HEAD_DIM - rd), F32)
    zero = jnp.zeros((n, HEAD_DIM - rd), F32)
    zh = jnp.zeros((n, half), F32)
    cos64 = jnp.concatenate([cos, cos, one], axis=1)
    sa64 = jnp.concatenate([-sin, zh, zero], axis=1)
    sb64 = jnp.concatenate([zh, sin, zero], axis=1)
    dup = lambda a: jnp.concatenate([a, a], axis=1)
    return dup(cos64), dup(sa64), dup(sb64)


def _prep_w_in(w_in_l, b_forget_l, d_model):
    offs = np.cumsum([0, W_FOX, W_FOX, W_FOX, H_FOX, W_DSA, W_DSA, W_DSA, W_IDX, IDX_DIM,
                      IDX_HEADS, W_SB, W_SB, W_SB, W_MEM])
    (o_fq, o_fk, o_fv, o_ff, o_dq, o_dk, o_dv, o_iq, o_ik, o_iw, o_sq, o_sk, o_sv, o_mq,
     o_g) = [int(v) for v in offs]
    sl = lambda o, n: w_in_l[:, o:o + n]
    ik = sl(o_ik, IDX_DIM)
    misc = jnp.concatenate([sl(o_ff, H_FOX), sl(o_iw, IDX_HEADS),
                            jnp.zeros((d_model, LANES - H_FOX - IDX_HEADS), F32)], axis=1)
    wp = jnp.concatenate([sl(o_fq, 3 * W_FOX), sl(o_dq, 3 * W_DSA), sl(o_iq, W_IDX),
                          sl(o_sq, 3 * W_SB), sl(o_mq, W_MEM), ik, ik, misc], axis=1)
    wg = w_in_l[:, o_g:]
    bfp = jnp.concatenate([b_forget_l, jnp.zeros((LANES - H_FOX,), F32)])[None, :]
    return wp.astype(BF16), wg.astype(BF16), bfp


def kernel(x_prompt, x_sample, cache_fox_kv, cache_fox_logf, cache_dsa_kv, cache_dsa_idxk,
           cache_sb_kv, cache_mem_kv, page_table, mem_prompt, w_in, b_forget, w_mem_kv,
           w_br_fox, w_br_dsa, w_br_sb, w_br_mem, w_out, w_ffn_in, w_ffn_out,
           g_mix_pre, g_mix_post, g_ffn_pre, g_ffn_post):
    bp, t_len, d = x_prompt.shape
    nb = x_sample.shape[0]
    depth = w_in.shape[0]
    npages = page_table.shape[1]
    past_len = npages * PAGE_SIZE
    n_pool = cache_fox_kv.shape[1]
    n_mem = mem_prompt.shape[1]
    n_p = bp * t_len

    tm = min(256, t_len)
    t_att = min(256, t_len)
    tq_dsa = min(128, t_len)
    tk_dsa = min(512, t_len)
    topk_p = min(DSA_TOPK_MAX, t_len // 4)
    topk_s = min(DSA_TOPK_MAX, (past_len + 1) // 4)
    assert tk_dsa >= topk_p and t_len % tk_dsa == 0 and t_len % tm == 0

    tabs_p = _rope_tables(jnp.arange(t_len, dtype=jnp.int32))
    tabs_s = _rope_tables(jnp.full((nb,), past_len, jnp.int32))
    pt_flat = page_table.reshape(-1).astype(jnp.int32)

    xp = x_prompt.reshape(n_p, d)
    xs = x_sample.reshape(nb, d)
    mem_flat = mem_prompt.reshape(bp * n_mem, d)

    rows_p, rows_s, mem_p = [], [], []
    for l in range(depth):
        wp, wg, bfp = _prep_w_in(w_in[l], b_forget[l], d)
        gpre, gpost = g_mix_pre[l][None, :], g_mix_post[l][None, :]
        fpre, fpost = g_ffn_pre[l][None, :], g_ffn_post[l][None, :]
        wf, wd_, ws, wm = (w.astype(BF16) for w in (w_br_fox[l], w_br_dsa[l], w_br_sb[l], w_br_mem[l]))
        wo = w_out[l].astype(BF16)
        wi, wfo = w_ffn_in[l].astype(BF16), w_ffn_out[l].astype(BF16)

        mem_kv = _matmul(mem_flat, w_mem_kv[l].astype(BF16))
        (fq, fkv, fk, fv, dq, dkv, dk, dv, iq, sq, skv, sk, sv, mq, ik32, ikb,
         misc) = _proj(xp, gpre, wp, tabs_p, bfp, tm)
        logf = misc[:, :H_FOX].reshape(bp, t_len, H_FOX)
        c = jnp.cumsum(logf, axis=1)
        ccol = jnp.repeat(c.reshape(n_p, H_FOX), HEAD_DIM, axis=1)
        crow = jnp.transpose(c, (0, 2, 1)).reshape(bp, H_FOX // 2, 2, t_len)
        crow = jnp.pad(crow, ((0, 0), (0, 0), (0, 6), (0, 0)))
        o_fox = _fox_attention(fq, fk, fv, ccol, crow, bp, t_len, t_att)
        o_dsa = _dsa_attention(dq, iq, misc, ikb, dk, dv, bp, t_len, tq_dsa, tk_dsa, topk_p)
        o_sb = _sb_attention(sq, sk, sv, bp, t_len, t_att)
        o_mem = _mem_attention(mq, mem_kv.reshape(bp, n_mem, 2 * W_MEM), bp, t_len, tm)
        xp = _merge(xp, o_fox, o_dsa, o_sb, o_mem, gpre, gpost, wg, wf, wd_, ws, wm, wo, tm)
        xp = _ffn(xp, fpre, fpost, wi, wfo, tm)
        rows_p.append((fkv.reshape(bp, t_len, 2, H_FOX, HEAD_DIM), logf,
                       dkv.reshape(bp, t_len, 2, H_DSA, HEAD_DIM),
                       ik32[:, :IDX_DIM].reshape(bp, t_len, IDX_DIM),
                       skv.reshape(bp, t_len, 2, H_SB, HEAD_DIM)))
        mem_p.append(mem_kv.reshape(bp, n_mem, 2, H_MEM, HEAD_DIM))

        (fq, fkv, fk, fv, dq, dkv, dk, dv, iq, sq, skv, sk, sv, mq, ik32, ikb,
         misc) = _proj(xs, gpre, wp, tabs_s, bfp, nb)
        r3 = lambda a: a.reshape(nb, 1, a.shape[1])
        iqh = jnp.pad(iq.reshape(nb, IDX_HEADS, IDX_DIM), ((0, 0), (0, 8 - IDX_HEADS), (0, 0)))
        iwb = jnp.pad(misc[:, MISC_IW:MISC_IW + IDX_HEADS], ((0, 0), (0, 8 - IDX_HEADS)))
        iwb = jnp.broadcast_to(iwb[:, :, None], (nb, 8, LANES))
        isc = _dec_scores(pt_flat, iqh, iwb, r3(ik32),
                          cache_dsa_idxk[l].reshape(n_pool, PAGE_SIZE, IDX_DIM), nb, npages)
        bias = _dec_select(isc.reshape(nb, -1), topk_s).reshape(nb, 1, -1)
        lnew = jnp.pad(misc[:, :H_FOX], ((0, 0), (0, 8 - H_FOX)))
        lnew = jnp.broadcast_to(lnew[:, :, None], (nb, 8, LANES))
        logf_pool = jnp.pad(jnp.transpose(cache_fox_logf[l], (0, 2, 1)), ((0, 0), (0, 8 - H_FOX), (0, 0)))
        o_fox, o_dsa, o_sb, o_mem = _dec_attention(
            pt_flat, r3(fq), r3(dq), r3(sq), r3(mq), r3(fkv), r3(dkv), lnew, bias,
            cache_mem_kv[l].reshape(nb, n_mem, 2 * W_MEM),
            cache_fox_kv[l].reshape(n_pool, PAGE_SIZE, 2 * W_FOX), logf_pool,
            cache_dsa_kv[l].reshape(n_pool, PAGE_SIZE, 2 * W_DSA),
            cache_sb_kv[l].reshape(n_pool, PAGE_SIZE, 2 * W_SB), nb, npages)
        sq2 = lambda a: a.reshape(nb, a.shape[2])
        xs = _merge(xs, sq2(o_fox), sq2(o_dsa), sq2(o_sb), sq2(o_mem), gpre, gpost, wg, wf, wd_,
                    ws, wm, wo, nb)
        xs = _ffn(xs, fpre, fpost, wi, wfo, nb)
        rows_s.append((fkv.reshape(nb, 1, 2, H_FOX, HEAD_DIM), misc[:, :H_FOX].reshape(nb, 1, H_FOX),
                       dkv.reshape(nb, 1, 2, H_DSA, HEAD_DIM),
                       ik32[:, :IDX_DIM].reshape(nb, 1, IDX_DIM),
                       skv.reshape(nb, 1, 2, H_SB, HEAD_DIM)))

    stk = lambda rows, i: jnp.stack([r[i] for r in rows], axis=0)
    return (xp.reshape(bp, t_len, d), xs.reshape(nb, 1, d),
            stk(rows_p, 0), stk(rows_p, 1), stk(rows_p, 2), stk(rows_p, 3), stk(rows_p, 4),
            jnp.stack(mem_p, axis=0),
            stk(rows_s, 0), stk(rows_s, 1), stk(rows_s, 2), stk(rows_s, 3), stk(rows_s, 4))
```

```python
import functools

import jax
import jax.numpy as jnp
import numpy as np
from jax import lax
from jax.experimental import pallas as pl
from jax.experimental.pallas import tpu as pltpu

HEAD_DIM = 64
H_FOX = 6
H_DSA = 6
H_SB = 4
H_MEM = 4
IDX_HEADS = 4
IDX_DIM = 64
DSA_TOPK_MAX = 256
ROPE_THETA = 500000.0
N_BRANCH = 4
EPS = 1e-6
PAGE_SIZE = 128

LANES = 128
VMEM_LIMIT = 56 * 1024 * 1024

F32 = jnp.float32
BF16 = jnp.bfloat16
NEG_INF = float("-inf")
INT_MIN = -2 ** 31
SB_DEAD = -120.0

W_FOX = H_FOX * HEAD_DIM
W_DSA = H_DSA * HEAD_DIM
W_SB = H_SB * HEAD_DIM
W_MEM = H_MEM * HEAD_DIM
W_IDX = IDX_HEADS * IDX_DIM

C_FQ, C_FKV = 0, W_FOX
C_DQ = C_FKV + 2 * W_FOX
C_DK = C_DQ + W_DSA
C_DV = C_DK + W_DSA
C_IQ = C_DV + W_DSA
C_SQ = C_IQ + W_IDX
C_SKV = C_SQ + W_SB
C_MQ = C_SKV + 2 * W_SB
C_IK = C_MQ + W_MEM
C_MISC = C_IK + LANES
N_PROJ = C_MISC + LANES
MISC_IW = H_FOX


def _cparams(sem):
    return pltpu.CompilerParams(dimension_semantics=sem, vmem_limit_bytes=VMEM_LIMIT)


def _nt_dot(a, b):
    return lax.dot_general(a, b, (((1,), (1,)), ((), ())), preferred_element_type=F32)


def _softplus_tail(z):
    return jnp.log1p(jnp.exp(-jnp.abs(z)))


def _split3(x):
    hi = x.astype(BF16)
    r1 = x - hi.astype(F32)
    mid = r1.astype(BF16)
    lo = (r1 - mid.astype(F32)).astype(BF16)
    return hi, mid, lo


def _proj_kernel(x_ref, g_ref, w_ref, cos_ref, sa_ref, sb_ref, bf_ref,
                 fq_ref, fkv_ref, fk_ref, fv_ref, dq_ref, dkv_ref, dk_ref, dv_ref,
                 iq_ref, sq_ref, skv_ref, sk_ref, sv_ref, mq_ref, ik32_ref, ikb_ref, misc_ref):
    x = x_ref[...]
    h = x * lax.rsqrt(jnp.mean(x * x, axis=-1, keepdims=True) + EPS)
    hb = (h * g_ref[...]).astype(BF16)
    cosf, sa, sb = cos_ref[...], sa_ref[...], sb_ref[...]
    scale = HEAD_DIM ** -0.5

    def mm(c0, n):
        return _nt_dot(hb, w_ref[c0:c0 + n, :])

    def rope(z):
        outs = []
        for j in range(z.shape[1] // LANES):
            zj = z[:, j * LANES:(j + 1) * LANES]
            outs.append(zj * cosf + pltpu.roll(zj, LANES - 8, 1) * sa + pltpu.roll(zj, 8, 1) * sb)
        return outs[0] if len(outs) == 1 else jnp.concatenate(outs, axis=1)

    fq_ref[...] = (mm(C_FQ, W_FOX) * scale).astype(BF16)
    fkv = mm(C_FKV, 2 * W_FOX)
    fkv_ref[...] = fkv
    fk_ref[...] = fkv[:, :W_FOX].astype(BF16)
    fv_ref[...] = fkv[:, W_FOX:].astype(BF16)

    dq_ref[...] = (rope(mm(C_DQ, W_DSA)) * scale).astype(BF16)
    dk = rope(mm(C_DK, W_DSA))
    dv = mm(C_DV, W_DSA)
    dkv_ref[:, :W_DSA] = dk
    dkv_ref[:, W_DSA:] = dv
    dk_ref[...] = dk.astype(BF16)
    dv_ref[...] = dv.astype(BF16)

    iq_ref[...] = rope(mm(C_IQ, W_IDX)).astype(BF16)

    sq_ref[...] = (mm(C_SQ, W_SB) * scale).astype(BF16)
    skv = mm(C_SKV, 2 * W_SB)
    skv_ref[...] = skv
    sk_ref[...] = skv[:, :W_SB].astype(BF16)
    sv_ref[...] = skv[:, W_SB:].astype(BF16)

    mq_ref[...] = (mm(C_MQ, W_MEM) * scale).astype(BF16)

    ik = rope(mm(C_IK, LANES))
    ik32_ref[...] = ik
    ikb_ref[...] = ik.astype(BF16)

    zm = mm(C_MISC, LANES)
    ff = zm + bf_ref[...]
    logf = -(jnp.maximum(-ff, 0.0) + _softplus_tail(ff))
    lane = lax.broadcasted_iota(jnp.int32, zm.shape, 1)
    misc_ref[...] = jnp.where(lane < H_FOX, logf, zm)


def _proj(x, g, w, tabs, bfp, tm):
    n, d = x.shape
    cosf, sa, sb = tabs
    nt = cosf.shape[0] // tm
    row = lambda i: (i, 0)
    tab = lambda i: (i % nt, 0)
    const = lambda i: (0, 0)
    widths = [(W_FOX, BF16), (2 * W_FOX, F32), (W_FOX, BF16), (W_FOX, BF16),
              (W_DSA, BF16), (2 * W_DSA, F32), (W_DSA, BF16), (W_DSA, BF16),
              (W_IDX, BF16), (W_SB, BF16), (2 * W_SB, F32), (W_SB, BF16), (W_SB, BF16),
              (W_MEM, BF16), (LANES, F32), (LANES, BF16), (LANES, F32)]
    return pl.pallas_call(
        _proj_kernel,
        name="input_projection",
        grid=(n // tm,),
        in_specs=[pl.BlockSpec((tm, d), row), pl.BlockSpec((1, d), const),
                  pl.BlockSpec((N_PROJ, d), const),
                  pl.BlockSpec((tm, LANES), tab), pl.BlockSpec((tm, LANES), tab),
                  pl.BlockSpec((tm, LANES), tab), pl.BlockSpec((1, LANES), const)],
        out_specs=[pl.BlockSpec((tm, wd), row) for wd, _ in widths],
        out_shape=[jax.ShapeDtypeStruct((n, wd), dt) for wd, dt in widths],
        compiler_params=_cparams(("parallel",)),
    )(x, g, w, cosf, sa, sb, bfp)


def _matmul_kernel(a_ref, b_ref, o_ref):
    o_ref[...] = jnp.dot(a_ref[...].astype(BF16), b_ref[...], preferred_element_type=F32)


def _matmul(a, b):
    return pl.pallas_call(
        _matmul_kernel,
        out_shape=jax.ShapeDtypeStruct((a.shape[0], b.shape[1]), F32),
        compiler_params=pltpu.CompilerParams(vmem_limit_bytes=VMEM_LIMIT),
    )(a, b)


def _pair_masks(rows):
    lane = lax.broadcasted_iota(jnp.int32, (rows, LANES), 1)
    lo = (lane < HEAD_DIM).astype(F32)
    return lo, 1.0 - lo


def _fold_lanes(x):
    part = x[:, 0:LANES]
    for u in range(1, x.shape[1] // LANES):
        part = part + x[:, u * LANES:(u + 1) * LANES]
    return part


def _flash_update(s, m, l, acc, vc, guard):
    m_new = jnp.maximum(m, jnp.max(s, axis=-1, keepdims=True))
    m_use = jnp.where(m_new == NEG_INF, 0.0, m_new) if guard else m_new
    alpha = jnp.exp(m - m_use)
    p = jnp.exp(s - m_use)
    l = alpha * l + _fold_lanes(p)
    acc = alpha * acc + jnp.dot(p.astype(BF16), vc, preferred_element_type=F32)
    return m_new, l, acc


def _flash_init(rows):
    return (jnp.full((rows, 1), NEG_INF, F32), jnp.zeros((rows, LANES), F32),
            jnp.zeros((rows, LANES), F32))


def _flash_out(l, acc):
    return acc / jnp.sum(l, axis=-1, keepdims=True)


def _fox_kernel(q_ref, k_ref, v_ref, ccol_ref, crow_ref, o_ref, *, tq, tk):
    qi = pl.program_id(2)
    q0 = qi * tq
    jd = q0 // tk
    q2 = q_ref[...].astype(F32)
    lo, hi = _pair_masks(tq)
    ccol = ccol_ref[...]
    qa, qb = (q2 * lo).astype(BF16), (q2 * hi).astype(BF16)
    cqa, cqb = ccol[:, 0:1], ccol[:, HEAD_DIM:HEAD_DIM + 1]
    row_pos = q0 + lax.broadcasted_iota(jnp.int32, (tq, tk), 0)
    col_in = lax.broadcasted_iota(jnp.int32, (tq, tk), 1)

    def chunk(j, carry, diag):
        ca, cb = carry
        k0 = pl.multiple_of(j * tk, tk)
        kc = k_ref[pl.ds(k0, tk), :]
        vc = v_ref[pl.ds(k0, tk), :]
        sa = _nt_dot(qa, kc) + (cqa - crow_ref[0, 0, 0:1, pl.ds(k0, tk)])
        sb = _nt_dot(qb, kc) + (cqb - crow_ref[0, 0, 1:2, pl.ds(k0, tk)])
        if diag:
            causal = k0 + col_in <= row_pos
            sa = jnp.where(causal, sa, NEG_INF)
            sb = jnp.where(causal, sb, NEG_INF)
        return _flash_update(sa, *ca, vc, False), _flash_update(sb, *cb, vc, False)

    carry = lax.fori_loop(0, jd, lambda j, c: chunk(j, c, False),
                          (_flash_init(tq), _flash_init(tq)))
    (_, la, acca), (_, lb, accb) = chunk(jd, carry, True)
    o_ref[...] = (_flash_out(la, acca) * lo + _flash_out(lb, accb) * hi).astype(BF16)


def _fox_attention(q, k, v, ccol, crow, b, t_len, tq, tk):
    npair = H_FOX // 2
    nq = t_len // tq
    return pl.pallas_call(
        functools.partial(_fox_kernel, tq=tq, tk=tk),
        name="fox_attention",
        grid=(b, npair, nq),
        in_specs=[pl.BlockSpec((tq, LANES), lambda bi, p, i: (bi * nq + i, p)),
                  pl.BlockSpec((t_len, LANES), lambda bi, p, i: (bi, p)),
                  pl.BlockSpec((t_len, LANES), lambda bi, p, i: (bi, p)),
                  pl.BlockSpec((tq, LANES), lambda bi, p, i: (bi * nq + i, p)),
                  pl.BlockSpec((1, 1, 8, t_len), lambda bi, p, i: (bi, p, 0, 0))],
        out_specs=pl.BlockSpec((tq, LANES), lambda bi, p, i: (bi * nq + i, p)),
        out_shape=jax.ShapeDtypeStruct(q.shape, BF16),
        compiler_params=_cparams(("parallel", "parallel", "arbitrary")),
    )(q, k, v, ccol, crow)


def _sb_kernel(q_ref, k_ref, v_ref, o_ref, *, t):
    qi = pl.program_id(2)
    q2 = q_ref[...].astype(F32)
    lo, hi = _pair_masks(t)
    row = lax.broadcasted_iota(jnp.int32, (t, t), 0)
    col = lax.broadcasted_iota(jnp.int32, (t, t), 1)
    strict = col < row
    after = (row > col).astype(BF16)

    qa, qb = (q2 * lo).astype(BF16), (q2 * hi).astype(BF16)

    def one(qh, kc, vc, run, acc, diag):
        z = _nt_dot(qh, kc)
        tl = _softplus_tail(z)
        log_beta = jnp.minimum(z, 0.0) - tl
        log_1mb = -jnp.maximum(z, 0.0) - tl
        if diag:
            log_1mb = jnp.where(strict, log_1mb, 0.0)
        h1, h2, h3 = _split3(log_1mb)
        tail = (jnp.dot(h1, after, preferred_element_type=F32)
                + jnp.dot(h2, after, preferred_element_type=F32)
                + jnp.dot(h3, after, preferred_element_type=F32))
        a = jnp.exp(log_beta + (tail + run))
        if diag:
            a = jnp.where(strict, a, 0.0)
        acc = acc + jnp.dot(a.astype(BF16), vc, preferred_element_type=F32)
        run = run + (tail[:, 0:1] + log_1mb[:, 0:1])
        return run, acc

    def chunk(j, carry, diag):
        (ra, acca), (rb, accb) = carry
        k0 = pl.multiple_of(j * t, t)
        kc = k_ref[pl.ds(k0, t), :]
        vc = v_ref[pl.ds(k0, t), :]
        return one(qa, kc, vc, ra, acca, diag), one(qb, kc, vc, rb, accb, diag)

    zero = (jnp.zeros((t, 1), F32), jnp.zeros((t, LANES), F32))
    carry = chunk(qi, (zero, zero), True)

    def live(state):
        j, ((ra, _), (rb, _)) = state
        return jnp.logical_and(j >= 0, jnp.max(jnp.maximum(ra, rb)) > SB_DEAD)

    def older(state):
        j, c = state
        return j - 1, chunk(j, c, False)

    _, ((_, acca), (_, accb)) = lax.while_loop(live, older, (qi - 1, carry))
    o_ref[...] = (acca * lo + accb * hi).astype(BF16)


def _sb_attention(q, k, v, b, t_len, t):
    npair = H_SB // 2
    nq = t_len // t
    return pl.pallas_call(
        functools.partial(_sb_kernel, t=t),
        name="stick_breaking_attention",
        grid=(b, npair, nq),
        in_specs=[pl.BlockSpec((t, LANES), lambda bi, p, i: (bi * nq + i, p)),
                  pl.BlockSpec((t_len, LANES), lambda bi, p, i: (bi, p)),
                  pl.BlockSpec((t_len, LANES), lambda bi, p, i: (bi, p))],
        out_specs=pl.BlockSpec((t, LANES), lambda bi, p, i: (bi * nq + i, p)),
        out_shape=jax.ShapeDtypeStruct(q.shape, BF16),
        compiler_params=_cparams(("parallel", "parallel", "arbitrary")),
    )(q, k, v)


def _sort_key(x):
    x = jnp.where(x == 0.0, 0.0, x)
    bits = pltpu.bitcast(x, jnp.int32)
    return bits ^ ((bits >> 31) & 0x7FFFFFFF)


KEY_NEG_INF = int(np.int32(np.array(-np.inf, np.float32).view(np.int32)) ^ np.int32(0x7FFFFFFF))


def _select_topk_bias(key_ref, nch, tk, rows, topk):
    def fold(mask_f):
        part = mask_f[:, 0:LANES]
        for u in range(1, tk // LANES):
            part = part + mask_f[:, u * LANES:(u + 1) * LANES]
        return part

    def count(pred):
        def body(c, acc):
            k0 = pl.multiple_of(c * tk, tk)
            blk = key_ref[:, pl.ds(k0, tk)]
            return acc + fold(jnp.where(pred(blk, k0), 1.0, 0.0))
        acc = lax.fori_loop(0, nch, body, jnp.zeros((rows, LANES), F32))
        return jnp.sum(acc, axis=-1, keepdims=True)

    kf = float(topk)

    def value_bit(i, ans):
        cand = ans | jnp.left_shift(jnp.int32(1), 31 - i)
        cand_s = cand ^ INT_MIN
        cnt = count(lambda blk, k0: blk >= cand_s)
        return jnp.where(cnt >= kf, cand, ans)

    ans = lax.fori_loop(0, 32, value_bit, jnp.zeros((rows, 1), jnp.int32))
    thr = ans ^ INT_MIN
    cnt_gt = count(lambda blk, k0: blk > thr)
    cnt_ge = count(lambda blk, k0: blk >= thr)
    need = kf - cnt_gt
    tie = jnp.logical_and(cnt_ge - cnt_gt > need, thr != KEY_NEG_INF)
    any_tie = jnp.max(jnp.where(tie, 1.0, 0.0)) > 0.0

    nbits = max(1, int(np.ceil(np.log2(key_ref.shape[1] + 1))))

    def col_index(k0):
        return k0 + lax.broadcasted_iota(jnp.int32, (rows, tk), 1)

    def tie_cut():
        def index_bit(i, x):
            cand = x | jnp.left_shift(jnp.int32(1), nbits - 1 - i)
            cnt = count(lambda blk, k0: jnp.logical_and(blk == thr, col_index(k0) < cand))
            return jnp.where(cnt < need, cand, x)
        return lax.fori_loop(0, nbits, index_bit, jnp.zeros((rows, 1), jnp.int32))

    cut = lax.cond(any_tie, tie_cut, lambda: jnp.full((rows, 1), 2 ** 30, jnp.int32))

    def write(c, _):
        k0 = pl.multiple_of(c * tk, tk)
        blk = key_ref[:, pl.ds(k0, tk)]
        sel = jnp.logical_or(blk > thr, jnp.logical_and(blk == thr, col_index(k0) <= cut))
        key_ref[:, pl.ds(k0, tk)] = pltpu.bitcast(jnp.where(sel, 0.0, NEG_INF), jnp.int32)
        return 0

    lax.fori_loop(0, nch, write, 0)


def _dsa_kernel(q_ref, iq_ref, misc_ref, ik_ref, k_ref, v_ref, o_ref, key_ref, *, tq, tk, topk):
    qi = pl.program_id(1)
    q0 = qi * tq
    jd = q0 // tk
    nch = jd + 1
    lo, hi = _pair_masks(tq)
    row_pos = q0 + lax.broadcasted_iota(jnp.int32, (tq, tk), 0)
    col_in = lax.broadcasted_iota(jnp.int32, (tq, tk), 1)

    misc = misc_ref[...]
    iq = iq_ref[...].astype(F32)
    iq_heads = []
    for hh in range(IDX_HEADS):
        pair = iq[:, (hh // 2) * LANES:(hh // 2 + 1) * LANES]
        iq_heads.append((pair * (lo if hh % 2 == 0 else hi)).astype(BF16))
    iws = [misc[:, MISC_IW + hh:MISC_IW + hh + 1] for hh in range(IDX_HEADS)]

    def score_chunk(c, diag):
        k0 = pl.multiple_of(c * tk, tk)
        ikc = ik_ref[pl.ds(k0, tk), :]
        isc = jnp.zeros((tq, tk), F32)
        for hh in range(IDX_HEADS):
            isc = isc + iws[hh] * jnp.maximum(_nt_dot(iq_heads[hh], ikc), 0.0)
        if diag:
            isc = jnp.where(k0 + col_in <= row_pos, isc, NEG_INF)
        key_ref[:, pl.ds(k0, tk)] = _sort_key(isc)

    def score_body(c, _):
        score_chunk(c, False)
        return 0

    lax.fori_loop(0, jd, score_body, 0)
    score_chunk(jd, True)

    _select_topk_bias(key_ref, nch, tk, tq, topk)

    qs = []
    for p in range(H_DSA // 2):
        q2 = q_ref[:, p * LANES:(p + 1) * LANES].astype(F32)
        qs += [(q2 * lo).astype(BF16), (q2 * hi).astype(BF16)]

    def chunk(c, carry, diag):
        k0 = pl.multiple_of(c * tk, tk)
        bias = pltpu.bitcast(key_ref[:, pl.ds(k0, tk)], F32)
        if diag:
            bias = jnp.where(k0 + col_in <= row_pos, bias, NEG_INF)
        out = []
        for hh in range(H_DSA):
            cols = slice((hh // 2) * LANES, (hh // 2 + 1) * LANES)
            s = _nt_dot(qs[hh], k_ref[pl.ds(k0, tk), cols]) + bias
            out.append(_flash_update(s, *carry[hh], v_ref[pl.ds(k0, tk), cols], True))
        return tuple(out)

    carry = lax.fori_loop(0, jd, lambda c, cr: chunk(c, cr, False),
                          tuple(_flash_init(tq) for _ in range(H_DSA)))
    carry = chunk(jd, carry, True)
    outs = []
    for p in range(H_DSA // 2):
        (_, la, acca), (_, lb, accb) = carry[2 * p], carry[2 * p + 1]
        outs.append(_flash_out(la, acca) * lo + _flash_out(lb, accb) * hi)
    o_ref[...] = jnp.concatenate(outs, axis=1).astype(BF16)


def _dsa_attention(q, iq, misc, ikb, k, v, b, t_len, tq, tk, topk):
    nq = t_len // tq
    rowblk = lambda bi, i: (bi * nq + i, 0)
    per_b = lambda bi, i: (bi, 0)
    return pl.pallas_call(
        functools.partial(_dsa_kernel, tq=tq, tk=tk, topk=topk),
        name="dsa_attention",
        grid=(b, nq),
        in_specs=[pl.BlockSpec((tq, W_DSA), rowblk), pl.BlockSpec((tq, W_IDX), rowblk),
                  pl.BlockSpec((tq, LANES), rowblk), pl.BlockSpec((t_len, LANES), per_b),
                  pl.BlockSpec((t_len, W_DSA), per_b), pl.BlockSpec((t_len, W_DSA), per_b)],
        out_specs=pl.BlockSpec((tq, W_DSA), rowblk),
        out_shape=jax.ShapeDtypeStruct(q.shape, BF16),
        scratch_shapes=[pltpu.VMEM((tq, t_len), jnp.int32)],
        compiler_params=_cparams(("parallel", "arbitrary")),
    )(q, iq, misc, ikb, k, v)


def _mem_kernel(q_ref, mkv_ref, o_ref, *, tq):
    lo, hi = _pair_masks(tq)
    outs = []
    for p in range(H_MEM // 2):
        q2 = q_ref[:, p * LANES:(p + 1) * LANES].astype(F32)
        mk = mkv_ref[0, :, p * LANES:(p + 1) * LANES].astype(BF16)
        mv = mkv_ref[0, :, W_MEM + p * LANES:W_MEM + (p + 1) * LANES].astype(BF16)

        def head(qh):
            s = _nt_dot(qh, mk)
            pr = jnp.exp(s - jnp.max(s, axis=-1, keepdims=True))
            l = jnp.sum(pr, axis=-1, keepdims=True)
            return jnp.dot(pr.astype(BF16), mv, preferred_element_type=F32) / l

        outs.append(head((q2 * lo).astype(BF16)) * lo + head((q2 * hi).astype(BF16)) * hi)
    o_ref[...] = jnp.concatenate(outs, axis=1).astype(BF16)


def _mem_attention(q, mkv, b, t_len, tq):
    nq = t_len // tq
    n_mem = mkv.shape[1]
    return pl.pallas_call(
        functools.partial(_mem_kernel, tq=tq),
        grid=(b, nq),
        in_specs=[pl.BlockSpec((tq, W_MEM), lambda bi, i: (bi * nq + i, 0)),
                  pl.BlockSpec((1, n_mem, 2 * W_MEM), lambda bi, i: (bi, 0, 0))],
        out_specs=pl.BlockSpec((tq, W_MEM), lambda bi, i: (bi * nq + i, 0)),
        out_shape=jax.ShapeDtypeStruct(q.shape, BF16),
        compiler_params=_cparams(("parallel", "arbitrary")),
    )(q, mkv)


def _rms(x, g):
    return x * lax.rsqrt(jnp.mean(x * x, axis=-1, keepdims=True) + EPS) * g


def _merge_kernel(x_ref, of_ref, od_ref, os_ref, om_ref, gpre_ref, gpost_ref, wg_ref,
                  wf_ref, wd_ref, ws_ref, wm_ref, wo_ref, y_ref):
    x = x_ref[...]
    d = x.shape[1]
    hb = _rms(x, gpre_ref[...]).astype(BF16)
    merged = None
    for i, (o_ref, w_ref) in enumerate(((of_ref, wf_ref), (od_ref, wd_ref),
                                        (os_ref, ws_ref), (om_ref, wm_ref))):
        gate = jax.nn.sigmoid(_nt_dot(hb, wg_ref[i * d:(i + 1) * d, :]))
        br = jnp.dot(o_ref[...].astype(BF16), w_ref[...], preferred_element_type=F32)
        merged = gate * br if merged is None else merged + gate * br
    y = jnp.dot(merged.astype(BF16), wo_ref[...], preferred_element_type=F32)
    y_ref[...] = x + _rms(y, gpost_ref[...])


def _merge(x, o_fox, o_dsa, o_sb, o_mem, gpre, gpost, wg, wf, wd, ws, wm, wo, tm):
    n, d = x.shape
    row = lambda i: (i, 0)
    const = lambda i: (0, 0)
    full = lambda a: pl.BlockSpec(a.shape, const)
    return pl.pallas_call(
        _merge_kernel,
        name="branch_merge",
        grid=(n // tm,),
        in_specs=[pl.BlockSpec((tm, d), row)]
        + [pl.BlockSpec((tm, o.shape[1]), row) for o in (o_fox, o_dsa, o_sb, o_mem)]
        + [full(a) for a in (gpre, gpost, wg, wf, wd, ws, wm, wo)],
        out_specs=pl.BlockSpec((tm, d), row),
        out_shape=jax.ShapeDtypeStruct((n, d), F32),
        compiler_params=_cparams(("parallel",)),
    )(x, o_fox, o_dsa, o_sb, o_mem, gpre, gpost, wg, wf, wd, ws, wm, wo)


def _ffn_kernel(x_ref, gpre_ref, gpost_ref, wi_ref, wo_ref, y_ref, *, d_ff, tc):
    x = x_ref[...]
    hb = _rms(x, gpre_ref[...]).astype(BF16)
    y = jnp.zeros(x.shape, F32)
    for c0 in range(0, d_ff, tc):
        gate = jnp.dot(hb, wi_ref[:, c0:c0 + tc], preferred_element_type=F32)
        up = jnp.dot(hb, wi_ref[:, d_ff + c0:d_ff + c0 + tc], preferred_element_type=F32)
        act = (gate * jax.nn.sigmoid(gate)) * up
        y = y + jnp.dot(act.astype(BF16), wo_ref[c0:c0 + tc, :], preferred_element_type=F32)
    y_ref[...] = x + _rms(y, gpost_ref[...])


def _ffn(x, gpre, gpost, wi, wo, tm):
    n, d = x.shape
    d_ff = wo.shape[0]
    tc = 256 if d_ff % 256 == 0 else d_ff
    row = lambda i: (i, 0)
    const = lambda i: (0, 0)
    return pl.pallas_call(
        functools.partial(_ffn_kernel, d_ff=d_ff, tc=tc),
        name="swiglu_ffn",
        grid=(n // tm,),
        in_specs=[pl.BlockSpec((tm, d), row), pl.BlockSpec(gpre.shape, const),
                  pl.BlockSpec(gpost.shape, const), pl.BlockSpec(wi.shape, const),
                  pl.BlockSpec(wo.shape, const)],
        out_specs=pl.BlockSpec((tm, d), row),
        out_shape=jax.ShapeDtypeStruct((n, d), F32),
        compiler_params=_cparams(("parallel",)),
    )(x, gpre, gpost, wi, wo)


def _dec_score_kernel(pt_ref, iqh_ref, iwb_ref, iknew_ref, *rest, npages):
    pages = rest[:npages]
    o_ref = rest[npages]
    iqh = iqh_ref[0]
    iwb = iwb_ref[0]
    for j in range(npages):
        s = jnp.dot(iqh, pages[j][0, 0].astype(BF16), preferred_element_type=F32)
        o_ref[0, :, j * LANES:(j + 1) * LANES] = jnp.sum(iwb * jnp.maximum(s, 0.0), axis=0,
                                                         keepdims=True)
    ik_new = iknew_ref[0][:, :IDX_DIM].astype(BF16).astype(F32)
    s_new = jnp.sum(iqh.astype(F32) * ik_new, axis=-1, keepdims=True)
    isc_new = jnp.sum(iwb[:, 0:1] * jnp.maximum(s_new, 0.0), axis=0, keepdims=True)
    lane = lax.broadcasted_iota(jnp.int32, (1, LANES), 1)
    o_ref[0, :, npages * LANES:] = jnp.where(lane == 0, isc_new, NEG_INF)


def _dec_scores(pt_flat, iqh, iwb, ik_new, idxk_pool, layer, nb, npages):
    width = (npages + 1) * LANES
    seq = lambda bi, pt: (bi, 0, 0)

    def page_map(j):
        return lambda bi, pt: (layer, pt[bi * npages + j], 0, 0)

    grid_spec = pltpu.PrefetchScalarGridSpec(
        num_scalar_prefetch=1, grid=(nb,),
        in_specs=[pl.BlockSpec((1, 8, IDX_DIM), seq), pl.BlockSpec((1, 8, LANES), seq),
                  pl.BlockSpec((1, 1, LANES), seq)]
        + [pl.BlockSpec((1, 1, IDX_DIM, PAGE_SIZE), page_map(j)) for j in range(npages)],
        out_specs=pl.BlockSpec((1, 1, width), seq))
    return pl.pallas_call(
        functools.partial(_dec_score_kernel, npages=npages),
        name="decode_index_scores",
        grid_spec=grid_spec,
        out_shape=jax.ShapeDtypeStruct((nb, 1, width), F32),
        compiler_params=_cparams(("arbitrary",)),
    )(pt_flat, iqh, iwb, ik_new, *([idxk_pool] * npages))


def _dec_select_kernel(isc_ref, o_ref, key_ref, *, topk):
    rows, width = isc_ref.shape
    key_ref[...] = _sort_key(isc_ref[...])
    _select_topk_bias(key_ref, width // LANES, LANES, rows, topk)
    o_ref[...] = pltpu.bitcast(key_ref[...], F32)


def _dec_select(isc, topk):
    return pl.pallas_call(
        functools.partial(_dec_select_kernel, topk=topk),
        out_shape=jax.ShapeDtypeStruct(isc.shape, F32),
        scratch_shapes=[pltpu.VMEM(isc.shape, jnp.int32)],
        compiler_params=pltpu.CompilerParams(vmem_limit_bytes=VMEM_LIMIT),
    )(isc)


def _head_rows(width):
    sub = lax.broadcasted_iota(jnp.int32, (8, width), 0)
    lane = lax.broadcasted_iota(jnp.int32, (8, width), 1)
    return ((lane >> 6) == sub).astype(F32)


def _dec_attn_kernel(pt_ref, fq_ref, dq_ref, sq_ref, mq_ref, fnew_ref, dnew_ref, lnew_ref,
                     bias_ref, mem_ref, *rest, npages):
    fox_pages = rest[0:npages]
    logf_pages = rest[npages:2 * npages]
    dsa_pages = rest[2 * npages:3 * npages]
    sb_pages = rest[3 * npages:4 * npages]
    of_ref, od_ref, os_ref, om_ref = rest[4 * npages:]

    row = lax.broadcasted_iota(jnp.int32, (PAGE_SIZE, PAGE_SIZE), 0)
    col = lax.broadcasted_iota(jnp.int32, (PAGE_SIZE, PAGE_SIZE), 1)
    after = (row > col).astype(BF16)

    def suffix_after(x):
        h1, h2, h3 = _split3(x)
        return (jnp.dot(h1, after, preferred_element_type=F32)
                + jnp.dot(h2, after, preferred_element_type=F32)
                + jnp.dot(h3, after, preferred_element_type=F32))

    def head_diag(o, mask):
        return jnp.sum(o * mask, axis=0, keepdims=True)

    mask_f = _head_rows(W_FOX)
    qf = (fq_ref[0].astype(F32) * mask_f).astype(BF16)
    k_new = fnew_ref[0][:, :W_FOX].astype(BF16).astype(F32)
    v_new = fnew_ref[0][:, W_FOX:].astype(BF16).astype(F32)
    s_new = jnp.sum(qf.astype(F32) * k_new, axis=-1, keepdims=True)
    run = lnew_ref[0][:, 0:1]
    scores = [None] * npages
    for j in range(npages - 1, -1, -1):
        kt = fox_pages[j][0, 0, 0].astype(BF16)
        lf = logf_pages[j][0, 0]
        scores[j] = jnp.dot(qf, kt, preferred_element_type=F32) + (suffix_after(lf) + run)
        run = run + jnp.sum(lf, axis=-1, keepdims=True)
    m = s_new
    for j in range(npages):
        m = jnp.maximum(m, jnp.max(scores[j], axis=-1, keepdims=True))
    p_new = jnp.exp(s_new - m)
    l = p_new
    acc = p_new.astype(BF16).astype(F32) * v_new
    for j in range(npages):
        pr = jnp.exp(scores[j] - m)
        l = l + jnp.sum(pr, axis=-1, keepdims=True)
        acc = acc + _nt_dot(pr.astype(BF16), fox_pages[j][0, 0, 1].astype(BF16))
    of_ref[0] = head_diag(acc / l, mask_f)

    mask_d = _head_rows(W_DSA)
    qd = (dq_ref[0].astype(F32) * mask_d).astype(BF16)
    k_new = dnew_ref[0][:, :W_DSA].astype(BF16).astype(F32)
    v_new = dnew_ref[0][:, W_DSA:].astype(BF16).astype(F32)
    s_new = (jnp.sum(qd.astype(F32) * k_new, axis=-1, keepdims=True)
             + bias_ref[0][:, npages * LANES:npages * LANES + 1])
    for j in range(npages):
        kt = dsa_pages[j][0, 0, 0].astype(BF16)
        scores[j] = (jnp.dot(qd, kt, preferred_element_type=F32)
                     + bias_ref[0][:, j * LANES:(j + 1) * LANES])
    m = s_new
    for j in range(npages):
        m = jnp.maximum(m, jnp.max(scores[j], axis=-1, keepdims=True))
    p_new = jnp.exp(s_new - m)
    l = p_new
    acc = p_new.astype(BF16).astype(F32) * v_new
    for j in range(npages):
        pr = jnp.exp(scores[j] - m)
        l = l + jnp.sum(pr, axis=-1, keepdims=True)
        acc = acc + _nt_dot(pr.astype(BF16), dsa_pages[j][0, 0, 1].astype(BF16))
    od_ref[0] = head_diag(acc / l, mask_d)

    mask_s = _head_rows(W_SB)
    qs = (sq_ref[0].astype(F32) * mask_s).astype(BF16)
    run = jnp.zeros((8, 1), F32)
    acc = jnp.zeros((8, W_SB), F32)
    for j in range(npages - 1, -1, -1):
        z = jnp.dot(qs, sb_pages[j][0, 0, 0].astype(BF16), preferred_element_type=F32)
        tl = _softplus_tail(z)
        log_beta = jnp.minimum(z, 0.0) - tl
        log_1mb = -jnp.maximum(z, 0.0) - tl
        a = jnp.exp(log_beta + (suffix_after(log_1mb) + run))
        acc = acc + _nt_dot(a.astype(BF16), sb_pages[j][0, 0, 1].astype(BF16))
        run = run + jnp.sum(log_1mb, axis=-1, keepdims=True)
    os_ref[0] = head_diag(acc, mask_s)

    mask_m = _head_rows(W_MEM)
    qm = (mq_ref[0].astype(F32) * mask_m).astype(BF16)
    s = jnp.dot(qm, mem_ref[0, 0, 0].astype(BF16), preferred_element_type=F32)
    pr = jnp.exp(s - jnp.max(s, axis=-1, keepdims=True))
    l = jnp.sum(pr, axis=-1, keepdims=True)
    o = _nt_dot(pr.astype(BF16), mem_ref[0, 0, 1].astype(BF16))
    om_ref[0] = head_diag(o / l, mask_m)


def _dec_attention(pt_flat, fq, dq, sq, mq, fnew, dnew, lnew, bias, mem, fox_pool, logf_pool,
                   dsa_pool, sb_pool, layer, nb, npages):
    seq = lambda bi, pt: (bi, 0, 0)

    def page_map(j, nd):
        return lambda bi, pt: (layer, pt[bi * npages + j]) + (0,) * nd

    def pages(pool):
        blk = (1, 1) + pool.shape[2:]
        return [pl.BlockSpec(blk, page_map(j, len(blk) - 2)) for j in range(npages)]

    def seq_spec(a):
        return pl.BlockSpec((1,) + a.shape[1:], seq)

    mem_spec = pl.BlockSpec((1, 1) + mem.shape[2:], lambda bi, pt: (layer, bi, 0, 0, 0))
    grid_spec = pltpu.PrefetchScalarGridSpec(
        num_scalar_prefetch=1, grid=(nb,),
        in_specs=[seq_spec(a) for a in (fq, dq, sq, mq, fnew, dnew, lnew, bias)] + [mem_spec]
        + pages(fox_pool) + pages(logf_pool) + pages(dsa_pool) + pages(sb_pool),
        out_specs=[pl.BlockSpec((1, 1, wd), seq) for wd in (W_FOX, W_DSA, W_SB, W_MEM)])
    return pl.pallas_call(
        functools.partial(_dec_attn_kernel, npages=npages),
        name="decode_attention",
        grid_spec=grid_spec,
        out_shape=[jax.ShapeDtypeStruct((nb, 1, wd), F32) for wd in (W_FOX, W_DSA, W_SB, W_MEM)],
        compiler_params=_cparams(("arbitrary",)),
    )(pt_flat, fq, dq, sq, mq, fnew, dnew, lnew, bias, mem,
      *([fox_pool] * npages), *([logf_pool] * npages), *([dsa_pool] * npages),
      *([sb_pool] * npages))


def _rope_tables(pos):
    rd = HEAD_DIM // 4
    half = rd // 2
    inv_freq = ROPE_THETA ** (-jnp.arange(half, dtype=F32) * 2.0 / rd)
    ang = pos.astype(F32)[:, None] * inv_freq[None, :]
    cos, sin = jnp.cos(ang), jnp.sin(ang)
    n = pos.shape[0]
    one = jnp.ones((n, HEAD_DIM - rd), F32)
    zero = jnp.zeros((n, HEAD_DIM - rd), F32)
    zh = jnp.zeros((n, half), F32)
    cos64 = jnp.concatenate([cos, cos, one], axis=1)
    sa64 = jnp.concatenate([-sin, zh, zero], axis=1)
    sb64 = jnp.concatenate([zh, sin, zero], axis=1)
    dup = lambda a: jnp.concatenate([a, a], axis=1)
    return dup(cos64), dup(sa64), dup(sb64)


def _prep_w_in(w_in_l, b_forget_l, d_model):
    offs = np.cumsum([0, W_FOX, W_FOX, W_FOX, H_FOX, W_DSA, W_DSA, W_DSA, W_IDX, IDX_DIM,
                      IDX_HEADS, W_SB, W_SB, W_SB, W_MEM])
    (o_fq, o_fk, o_fv, o_ff, o_dq, o_dk, o_dv, o_iq, o_ik, o_iw, o_sq, o_sk, o_sv, o_mq,
     o_g) = [int(v) for v in offs]
    wt = jnp.transpose(w_in_l)
    sl = lambda o, n: wt[o:o + n]
    ik = sl(o_ik, IDX_DIM)
    misc = jnp.concatenate([sl(o_ff, H_FOX), sl(o_iw, IDX_HEADS),
                            jnp.zeros((LANES - H_FOX - IDX_HEADS, d_model), F32)], axis=0)
    wp = jnp.concatenate([sl(o_fq, 3 * W_FOX), sl(o_dq, 3 * W_DSA), sl(o_iq, W_IDX),
                          sl(o_sq, 3 * W_SB), sl(o_mq, W_MEM), ik, ik, misc], axis=0)
    wg = wt[o_g:]
    bfp = jnp.concatenate([b_forget_l, jnp.zeros((LANES - H_FOX,), F32)])[None, :]
    return wp.astype(BF16), wg.astype(BF16), bfp


def kernel(x_prompt, x_sample, cache_fox_kv, cache_fox_logf, cache_dsa_kv, cache_dsa_idxk,
           cache_sb_kv, cache_mem_kv, page_table, mem_prompt, w_in, b_forget, w_mem_kv,
           w_br_fox, w_br_dsa, w_br_sb, w_br_mem, w_out, w_ffn_in, w_ffn_out,
           g_mix_pre, g_mix_post, g_ffn_pre, g_ffn_post):
    bp, t_len, d = x_prompt.shape
    nb = x_sample.shape[0]
    depth = w_in.shape[0]
    npages = page_table.shape[1]
    past_len = npages * PAGE_SIZE
    n_pool = cache_fox_kv.shape[1]
    n_mem = mem_prompt.shape[1]
    n_p = bp * t_len

    tm = min(256, t_len)
    t_att = min(256, t_len)
    tq_dsa = min(128, t_len)
    tk_dsa = min(512, t_len)
    topk_p = min(DSA_TOPK_MAX, t_len // 4)
    topk_s = min(DSA_TOPK_MAX, (past_len + 1) // 4)
    assert tk_dsa >= topk_p and t_len % tk_dsa == 0 and t_len % tm == 0

    tabs_p = _rope_tables(jnp.arange(t_len, dtype=jnp.int32))
    tabs_s = _rope_tables(jnp.full((nb,), past_len, jnp.int32))
    pt_flat = page_table.reshape(-1).astype(jnp.int32)

    def kv_view(cache, width):
        view = jnp.transpose(cache, (0, 1, 3, 4, 5, 2))
        return view.reshape(cache.shape[:2] + (2, width, cache.shape[2]))

    fox_t = kv_view(cache_fox_kv, W_FOX)
    dsa_t = kv_view(cache_dsa_kv, W_DSA)
    sb_t = kv_view(cache_sb_kv, W_SB)
    mem_t = kv_view(cache_mem_kv, W_MEM)
    idxk_t = jnp.transpose(cache_dsa_idxk, (0, 1, 3, 2))
    logf_t = jnp.pad(jnp.transpose(cache_fox_logf, (0, 1, 3, 2)),
                     ((0, 0), (0, 0), (0, 8 - H_FOX), (0, 0)))

    xp = x_prompt.reshape(n_p, d)
    xs = x_sample.reshape(nb, d)
    mem_flat = mem_prompt.reshape(bp * n_mem, d)

    rows_p, rows_s, mem_p = [], [], []
    for l in range(depth):
        wp, wg, bfp = _prep_w_in(w_in[l], b_forget[l], d)
        gpre, gpost = g_mix_pre[l][None, :], g_mix_post[l][None, :]
        fpre, fpost = g_ffn_pre[l][None, :], g_ffn_post[l][None, :]
        wf, wd_, ws, wm = (w.astype(BF16) for w in (w_br_fox[l], w_br_dsa[l], w_br_sb[l], w_br_mem[l]))
        wo = w_out[l].astype(BF16)
        wi, wfo = w_ffn_in[l].astype(BF16), w_ffn_out[l].astype(BF16)

        mem_kv = _matmul(mem_flat, w_mem_kv[l].astype(BF16))
        (fq, fkv, fk, fv, dq, dkv, dk, dv, iq, sq, skv, sk, sv, mq, ik32, ikb,
         misc) = _proj(xp, gpre, wp, tabs_p, bfp, tm)
        logf = misc[:, :H_FOX].reshape(bp, t_len, H_FOX)
        c = jnp.cumsum(logf, axis=1)
        ccol = jnp.repeat(c.reshape(n_p, H_FOX), HEAD_DIM, axis=1)
        crow = jnp.transpose(c, (0, 2, 1)).reshape(bp, H_FOX // 2, 2, t_len)
        crow = jnp.pad(crow, ((0, 0), (0, 0), (0, 6), (0, 0)))
        o_fox = _fox_attention(fq, fk, fv, ccol, crow, bp, t_len, t_att, tk_dsa)
        o_dsa = _dsa_attention(dq, iq, misc, ikb, dk, dv, bp, t_len, tq_dsa, tk_dsa, topk_p)
        o_sb = _sb_attention(sq, sk, sv, bp, t_len, t_att)
        o_mem = _mem_attention(mq, mem_kv.reshape(bp, n_mem, 2 * W_MEM), bp, t_len, tm)
        xp = _merge(xp, o_fox, o_dsa, o_sb, o_mem, gpre, gpost, wg, wf, wd_, ws, wm, wo, tm)
        xp = _ffn(xp, fpre, fpost, wi, wfo, tm)
        rows_p.append((fkv.reshape(bp, t_len, 2, H_FOX, HEAD_DIM), logf,
                       dkv.reshape(bp, t_len, 2, H_DSA, HEAD_DIM),
                       ik32[:, :IDX_DIM].reshape(bp, t_len, IDX_DIM),
                       skv.reshape(bp, t_len, 2, H_SB, HEAD_DIM)))
        mem_p.append(mem_kv.reshape(bp, n_mem, 2, H_MEM, HEAD_DIM))

        (fq, fkv, fk, fv, dq, dkv, dk, dv, iq, sq, skv, sk, sv, mq, ik32, ikb,
         misc) = _proj(xs, gpre, wp, tabs_s, bfp, nb)
        r3 = lambda a: a.reshape(nb, 1, a.shape[1])
        iqh = jnp.pad(iq.reshape(nb, IDX_HEADS, IDX_DIM), ((0, 0), (0, 8 - IDX_HEADS), (0, 0)))
        iwb = jnp.pad(misc[:, MISC_IW:MISC_IW + IDX_HEADS], ((0, 0), (0, 8 - IDX_HEADS)))
        iwb = jnp.broadcast_to(iwb[:, :, None], (nb, 8, LANES))
        isc = _dec_scores(pt_flat, iqh, iwb, r3(ik32), idxk_t, l, nb, npages)
        bias = _dec_select(isc.reshape(nb, -1), topk_s).reshape(nb, 1, -1)
        lnew = jnp.pad(misc[:, :H_FOX], ((0, 0), (0, 8 - H_FOX)))
        lnew = jnp.broadcast_to(lnew[:, :, None], (nb, 8, LANES))
        o_fox, o_dsa, o_sb, o_mem = _dec_attention(
            pt_flat, r3(fq), r3(dq), r3(sq), r3(mq), r3(fkv), r3(dkv), lnew, bias,
            mem_t, fox_t, logf_t, dsa_t, sb_t, l, nb, npages)
        sq2 = lambda a: a.reshape(nb, a.shape[2])
        xs = _merge(xs, sq2(o_fox), sq2(o_dsa), sq2(o_sb), sq2(o_mem), gpre, gpost, wg, wf, wd_,
                    ws, wm, wo, nb)
        xs = _ffn(xs, fpre, fpost, wi, wfo, nb)
        rows_s.append((fkv.reshape(nb, 1, 2, H_FOX, HEAD_DIM), misc[:, :H_FOX].reshape(nb, 1, H_FOX),
                       dkv.reshape(nb, 1, 2, H_DSA, HEAD_DIM),
                       ik32[:, :IDX_DIM].reshape(nb, 1, IDX_DIM),
                       skv.reshape(nb, 1, 2, H_SB, HEAD_DIM)))

    stk = lambda rows, i: jnp.stack([r[i] for r in rows], axis=0)
    return (xp.reshape(bp, t_len, d), xs.reshape(nb, 1, d),
            stk(rows_p, 0), stk(rows_p, 1), stk(rows_p, 2), stk(rows_p, 3), stk(rows_p, 4),
            jnp.stack(mem_p, axis=0),
            stk(rows_s, 0), stk(rows_s, 1), stk(rows_s, 2), stk(rows_s, 3), stk(rows_s, 4))
```

```python
import functools

import jax
import jax.numpy as jnp
import numpy as np
from jax import lax
from jax.experimental import pallas as pl
from jax.experimental.pallas import tpu as pltpu

HEAD_DIM = 64
H_FOX = 6
H_DSA = 6
H_SB = 4
H_MEM = 4
IDX_HEADS = 4
IDX_DIM = 64
DSA_TOPK_MAX = 256
ROPE_THETA = 500000.0
N_BRANCH = 4
EPS = 1e-6
PAGE_SIZE = 128

LANES = 128
VMEM_LIMIT = 56 * 1024 * 1024

F32 = jnp.float32
BF16 = jnp.bfloat16
NEG_INF = float("-inf")
INT_MIN = -2 ** 31
SB_DEAD = -120.0

W_FOX = H_FOX * HEAD_DIM
W_DSA = H_DSA * HEAD_DIM
W_SB = H_SB * HEAD_DIM
W_MEM = H_MEM * HEAD_DIM
W_IDX = IDX_HEADS * IDX_DIM

C_FQ, C_FKV = 0, W_FOX
C_DQ = C_FKV + 2 * W_FOX
C_DK = C_DQ + W_DSA
C_DV = C_DK + W_DSA
C_IQ = C_DV + W_DSA
C_SQ = C_IQ + W_IDX
C_SKV = C_SQ + W_SB
C_MQ = C_SKV + 2 * W_SB
C_IK = C_MQ + W_MEM
C_MISC = C_IK + LANES
N_PROJ = C_MISC + LANES
MISC_IW = H_FOX


def _cparams(sem):
    return pltpu.CompilerParams(dimension_semantics=sem, vmem_limit_bytes=VMEM_LIMIT)


def _nt_dot(a, b):
    return lax.dot_general(a, b, (((1,), (1,)), ((), ())), preferred_element_type=F32)


def _softplus_tail(z):
    return jnp.log1p(jnp.exp(-jnp.abs(z)))


def _split3(x):
    hi = x.astype(BF16)
    r1 = x - hi.astype(F32)
    mid = r1.astype(BF16)
    lo = (r1 - mid.astype(F32)).astype(BF16)
    return hi, mid, lo


def _proj_kernel(x_ref, g_ref, w_ref, cos_ref, sa_ref, sb_ref, bf_ref, cost_ref, sint_ref,
                 fq_ref, fkv_ref, fk_ref, fv_ref, dq_ref, dkv_ref, dk_ref, dv_ref,
                 iq_ref, sq_ref, skv_ref, sk_ref, sv_ref, mq_ref, ik32_ref, ikb_ref, misc_ref,
                 *extra, transposed):
    x = x_ref[...]
    h = x * lax.rsqrt(jnp.mean(x * x, axis=-1, keepdims=True) + EPS)
    hb = (h * g_ref[...]).astype(BF16)
    cosf, sa, sb = cos_ref[...], sa_ref[...], sb_ref[...]
    scale = HEAD_DIM ** -0.5

    def mm(c0, n):
        return _nt_dot(hb, w_ref[c0:c0 + n, :])

    def mm_t(c0, n):
        return _nt_dot(w_ref[c0:c0 + n, :], hb)

    def rope(z):
        outs = []
        for j in range(z.shape[1] // LANES):
            zj = z[:, j * LANES:(j + 1) * LANES]
            outs.append(zj * cosf + pltpu.roll(zj, LANES - 8, 1) * sa + pltpu.roll(zj, 8, 1) * sb)
        return outs[0] if len(outs) == 1 else jnp.concatenate(outs, axis=1)

    def rope_t(zt):
        cos_t, sin_t = cost_ref[...], sint_ref[...]
        half = HEAD_DIM // 8
        parts = []
        for hh in range(zt.shape[0] // HEAD_DIM):
            base = hh * HEAD_DIM
            x1, x2 = zt[base:base + half], zt[base + half:base + 2 * half]
            parts += [x1 * cos_t - x2 * sin_t, x2 * cos_t + x1 * sin_t,
                      zt[base + 2 * half:base + HEAD_DIM]]
        return jnp.concatenate(parts, axis=0)

    fkv = mm(C_FKV, 2 * W_FOX)
    fkv_ref[...] = fkv
    fk_ref[...] = fkv[:, :W_FOX].astype(BF16)
    dk = rope(mm(C_DK, W_DSA))
    dv = mm(C_DV, W_DSA)
    dkv_ref[:, :W_DSA] = dk
    dkv_ref[:, W_DSA:] = dv
    dk_ref[...] = dk.astype(BF16)
    if transposed:
        fq_ref[...] = (mm_t(C_FQ, W_FOX) * scale).astype(BF16)
        fv_ref[...] = mm_t(C_FKV + W_FOX, W_FOX).astype(BF16)
        dq_ref[...] = (rope_t(mm_t(C_DQ, W_DSA)) * scale).astype(BF16)
        dv_ref[...] = mm_t(C_DV, W_DSA).astype(BF16)
        iq_ref[...] = rope_t(mm_t(C_IQ, W_IDX)).astype(BF16)
        extra[0][...] = mm_t(C_MISC, LANES)
    else:
        fq_ref[...] = (mm(C_FQ, W_FOX) * scale).astype(BF16)
        fv_ref[...] = fkv[:, W_FOX:].astype(BF16)
        dq_ref[...] = (rope(mm(C_DQ, W_DSA)) * scale).astype(BF16)
        dv_ref[...] = dv.astype(BF16)
        iq_ref[...] = rope(mm(C_IQ, W_IDX)).astype(BF16)

    sq_ref[...] = (mm(C_SQ, W_SB) * scale).astype(BF16)
    skv = mm(C_SKV, 2 * W_SB)
    skv_ref[...] = skv
    sk_ref[...] = skv[:, :W_SB].astype(BF16)
    sv_ref[...] = skv[:, W_SB:].astype(BF16)

    mq_ref[...] = (mm(C_MQ, W_MEM) * scale).astype(BF16)

    ik = rope(mm(C_IK, LANES))
    ik32_ref[...] = ik
    ikb_ref[...] = ik.astype(BF16)

    zm = mm(C_MISC, LANES)
    ff = zm + bf_ref[...]
    logf = -(jnp.maximum(-ff, 0.0) + _softplus_tail(ff))
    lane = lax.broadcasted_iota(jnp.int32, zm.shape, 1)
    misc_ref[...] = jnp.where(lane < H_FOX, logf, zm)


def _proj(x, g, w, tabs, bfp, tm, transposed):
    n, d = x.shape
    cosf, sa, sb, cos_t, sin_t = tabs
    nt = cosf.shape[0] // tm
    row = lambda i: (i, 0)
    col = lambda i: (0, i)
    tab = lambda i: (i % nt, 0)
    tab_t = lambda i: (0, i % nt)
    const = lambda i: (0, 0)
    widths = [(W_FOX, BF16, True), (2 * W_FOX, F32, False), (W_FOX, BF16, False), (W_FOX, BF16, True),
              (W_DSA, BF16, True), (2 * W_DSA, F32, False), (W_DSA, BF16, False), (W_DSA, BF16, True),
              (W_IDX, BF16, True), (W_SB, BF16, False), (2 * W_SB, F32, False), (W_SB, BF16, False),
              (W_SB, BF16, False), (W_MEM, BF16, False), (LANES, F32, False), (LANES, BF16, False),
              (LANES, F32, False)]
    if transposed:
        widths.append((LANES, F32, True))
    out_specs, out_shape = [], []
    for wd, dt, can_t in widths:
        if transposed and can_t:
            out_specs.append(pl.BlockSpec((wd, tm), col))
            out_shape.append(jax.ShapeDtypeStruct((wd, n), dt))
        else:
            out_specs.append(pl.BlockSpec((tm, wd), row))
            out_shape.append(jax.ShapeDtypeStruct((n, wd), dt))
    return pl.pallas_call(
        functools.partial(_proj_kernel, transposed=transposed),
        name="input_projection",
        grid=(n // tm,),
        in_specs=[pl.BlockSpec((tm, d), row), pl.BlockSpec((1, d), const),
                  pl.BlockSpec((N_PROJ, d), const),
                  pl.BlockSpec((tm, LANES), tab), pl.BlockSpec((tm, LANES), tab),
                  pl.BlockSpec((tm, LANES), tab), pl.BlockSpec((1, LANES), const),
                  pl.BlockSpec((8, tm), tab_t), pl.BlockSpec((8, tm), tab_t)],
        out_specs=out_specs,
        out_shape=out_shape,
        compiler_params=_cparams(("parallel",)),
    )(x, g, w, cosf, sa, sb, bfp, cos_t, sin_t)


def _matmul_kernel(a_ref, b_ref, o_ref):
    o_ref[...] = jnp.dot(a_ref[...].astype(BF16), b_ref[...], preferred_element_type=F32)


def _matmul(a, b):
    return pl.pallas_call(
        _matmul_kernel,
        out_shape=jax.ShapeDtypeStruct((a.shape[0], b.shape[1]), F32),
        compiler_params=pltpu.CompilerParams(vmem_limit_bytes=VMEM_LIMIT),
    )(a, b)


def _pair_masks(rows):
    lane = lax.broadcasted_iota(jnp.int32, (rows, LANES), 1)
    lo = (lane < HEAD_DIM).astype(F32)
    return lo, 1.0 - lo


def _fold_lanes(x):
    part = x[:, 0:LANES]
    for u in range(1, x.shape[1] // LANES):
        part = part + x[:, u * LANES:(u + 1) * LANES]
    return part


def _flash_update(s, m, l, acc, vc, guard):
    m_new = jnp.maximum(m, jnp.max(s, axis=-1, keepdims=True))
    m_use = jnp.where(m_new == NEG_INF, 0.0, m_new) if guard else m_new
    alpha = jnp.exp(m - m_use)
    p = jnp.exp(s - m_use)
    l = alpha * l + _fold_lanes(p)
    acc = alpha * acc + jnp.dot(p.astype(BF16), vc, preferred_element_type=F32)
    return m_new, l, acc


def _flash_init(rows):
    return (jnp.full((rows, 1), NEG_INF, F32), jnp.zeros((rows, LANES), F32),
            jnp.zeros((rows, LANES), F32))


def _flash_out(l, acc):
    return acc / jnp.sum(l, axis=-1, keepdims=True)


def _pair_masks_t(cols):
    sub = lax.broadcasted_iota(jnp.int32, (LANES, cols), 0)
    lo = (sub < HEAD_DIM).astype(F32)
    return lo, 1.0 - lo


def _tree_rows(x, op, nacc=4):
    parts = [x[i * 8:(i + 1) * 8] for i in range(x.shape[0] // 8)]
    accs = parts[:nacc]
    for i, part in enumerate(parts[nacc:]):
        accs[i % nacc] = op(accs[i % nacc], part)
    while len(accs) > 1:
        accs = [op(a, b) for a, b in zip(accs[0::2], accs[1::2])] + (accs[-1:] if len(accs) % 2 else [])
    return accs[0]


def _fold_rows(x):
    return _tree_rows(x, jnp.add)


def _max_rows(x):
    return _tree_rows(x, jnp.maximum)


def _flash_t_update(st, pmax, m, l, acc, vt, guard):
    m_new = jnp.maximum(m, jnp.max(pmax, axis=0, keepdims=True))
    m_use = jnp.where(m_new == NEG_INF, 0.0, m_new) if guard else m_new
    alpha = jnp.exp(m - m_use)
    p = jnp.exp(st - m_use)
    l = alpha * l + _fold_rows(p)
    acc = alpha * acc + jnp.dot(vt, p.astype(BF16), preferred_element_type=F32)
    return m_new, l, acc


def _flash_t_init(cols):
    return (jnp.full((1, cols), NEG_INF, F32), jnp.zeros((8, cols), F32),
            jnp.zeros((LANES, cols), F32))


def _flash_t_out(l, acc):
    return acc / jnp.sum(l, axis=0, keepdims=True)


def _fox_kernel(qt_ref, k_ref, vt_ref, crow_ref, ckb_ref, o_ref, *, tq, tk):
    qi = pl.program_id(2)
    q0 = pl.multiple_of(qi * tq, tq)
    jd = q0 // tk
    qt = qt_ref[...].astype(F32)
    lo, hi = _pair_masks_t(tq)
    qta, qtb = (qt * lo).astype(BF16), (qt * hi).astype(BF16)
    cqa = crow_ref[0, 0, 0:1, pl.ds(q0, tq)]
    cqb = crow_ref[0, 0, 1:2, pl.ds(q0, tq)]
    key_in = lax.broadcasted_iota(jnp.int32, (tk, tq), 0)
    q_pos = q0 + lax.broadcasted_iota(jnp.int32, (tk, tq), 1)
    reps = tq // LANES

    def scores(j):
        k0 = pl.multiple_of(j * tk, tk)
        kc = k_ref[pl.ds(k0, tk), :]
        ck = ckb_ref[pl.ds(k0, tk), :]
        cka = jnp.concatenate([ck[:, :LANES]] * reps, axis=1)
        ckb = jnp.concatenate([ck[:, LANES:]] * reps, axis=1)
        sa = jnp.dot(kc, qta, preferred_element_type=F32) + (cqa - cka)
        sb = jnp.dot(kc, qtb, preferred_element_type=F32) + (cqb - ckb)
        return (sa, _max_rows(sa)), (sb, _max_rows(sb))

    def consume(s, j, carry):
        ca, cb = carry
        vt = vt_ref[:, pl.ds(pl.multiple_of(j * tk, tk), tk)]
        return _flash_t_update(*s[0], *ca, vt, False), _flash_t_update(*s[1], *cb, vt, False)

    def trip(j, state):
        s, carry = state
        s_next = scores(j + 1)
        return s_next, consume(s, j, carry)

    s, carry = lax.fori_loop(0, jd, trip, (scores(0), (_flash_t_init(tq), _flash_t_init(tq))))
    causal = jd * tk + key_in <= q_pos
    s = tuple(jnp.where(causal, sh, NEG_INF) for sh, _ in s)
    (_, la, acca), (_, lb, accb) = consume(tuple((sh, _max_rows(sh)) for sh in s), jd, carry)
    o_ref[...] = (_flash_t_out(la, acca) * lo + _flash_t_out(lb, accb) * hi).astype(BF16)


def _fox_attention(qt, k, vt, crow, ckb, b, t_len, tq, tk):
    npair = H_FOX // 2
    nq = t_len // tq
    return pl.pallas_call(
        functools.partial(_fox_kernel, tq=tq, tk=tk),
        name="fox_attention",
        grid=(b, npair, nq),
        in_specs=[pl.BlockSpec((LANES, tq), lambda bi, p, i: (p, bi * nq + i)),
                  pl.BlockSpec((t_len, LANES), lambda bi, p, i: (bi, p)),
                  pl.BlockSpec((LANES, t_len), lambda bi, p, i: (p, bi)),
                  pl.BlockSpec((1, 1, 8, t_len), lambda bi, p, i: (bi, p, 0, 0)),
                  pl.BlockSpec((t_len, 2 * LANES), lambda bi, p, i: (bi, p))],
        out_specs=pl.BlockSpec((LANES, tq), lambda bi, p, i: (p, bi * nq + i)),
        out_shape=jax.ShapeDtypeStruct(qt.shape, BF16),
        compiler_params=_cparams(("parallel", "parallel", "arbitrary")),
    )(qt, k, vt, crow, ckb)


def _sb_kernel(q_ref, k_ref, v_ref, o_ref, *, t):
    qi = pl.program_id(2)
    q2 = q_ref[...].astype(F32)
    lo, hi = _pair_masks(t)
    row = lax.broadcasted_iota(jnp.int32, (t, t), 0)
    col = lax.broadcasted_iota(jnp.int32, (t, t), 1)
    strict = col < row
    after = (row > col).astype(BF16)

    qa, qb = (q2 * lo).astype(BF16), (q2 * hi).astype(BF16)

    def one(qh, kc, vc, run, acc, diag):
        z = _nt_dot(qh, kc)
        tl = _softplus_tail(z)
        log_beta = jnp.minimum(z, 0.0) - tl
        log_1mb = -jnp.maximum(z, 0.0) - tl
        if diag:
            log_1mb = jnp.where(strict, log_1mb, 0.0)
        h1, h2, h3 = _split3(log_1mb)
        tail = (jnp.dot(h1, after, preferred_element_type=F32)
                + jnp.dot(h2, after, preferred_element_type=F32)
                + jnp.dot(h3, after, preferred_element_type=F32))
        a = jnp.exp(log_beta + (tail + run))
        if diag:
            a = jnp.where(strict, a, 0.0)
        acc = acc + jnp.dot(a.astype(BF16), vc, preferred_element_type=F32)
        run = run + (tail[:, 0:1] + log_1mb[:, 0:1])
        return run, acc

    def chunk(j, carry, diag):
        (ra, acca), (rb, accb) = carry
        k0 = pl.multiple_of(j * t, t)
        kc = k_ref[pl.ds(k0, t), :]
        vc = v_ref[pl.ds(k0, t), :]
        return one(qa, kc, vc, ra, acca, diag), one(qb, kc, vc, rb, accb, diag)

    zero = (jnp.zeros((t, 1), F32), jnp.zeros((t, LANES), F32))
    carry = chunk(qi, (zero, zero), True)

    def live(state):
        j, ((ra, _), (rb, _)) = state
        return jnp.logical_and(j >= 0, jnp.max(jnp.maximum(ra, rb)) > SB_DEAD)

    def older(state):
        j, c = state
        return j - 1, chunk(j, c, False)

    _, ((_, acca), (_, accb)) = lax.while_loop(live, older, (qi - 1, carry))
    o_ref[...] = (acca * lo + accb * hi).astype(BF16)


def _sb_attention(q, k, v, b, t_len, t):
    npair = H_SB // 2
    nq = t_len // t
    return pl.pallas_call(
        functools.partial(_sb_kernel, t=t),
        name="stick_breaking_attention",
        grid=(b, npair, nq),
        in_specs=[pl.BlockSpec((t, LANES), lambda bi, p, i: (bi * nq + i, p)),
                  pl.BlockSpec((t_len, LANES), lambda bi, p, i: (bi, p)),
                  pl.BlockSpec((t_len, LANES), lambda bi, p, i: (bi, p))],
        out_specs=pl.BlockSpec((t, LANES), lambda bi, p, i: (bi * nq + i, p)),
        out_shape=jax.ShapeDtypeStruct(q.shape, BF16),
        compiler_params=_cparams(("parallel", "parallel", "arbitrary")),
    )(q, k, v)


def _sort_key(x):
    x = jnp.where(x == 0.0, 0.0, x)
    bits = pltpu.bitcast(x, jnp.int32)
    return bits ^ ((bits >> 31) & 0x7FFFFFFF)


KEY_NEG_INF = int(np.int32(np.array(-np.inf, np.float32).view(np.int32)) ^ np.int32(0x7FFFFFFF))


def _select_topk_bias(key_ref, nch, tk, nq, topk, keys_on_lanes):
    key_axis = 1 if keys_on_lanes else 0
    vec = (nq, 1) if keys_on_lanes else (1, nq)
    blk_shape = (nq, tk) if keys_on_lanes else (tk, nq)

    def load(k0):
        return key_ref[:, pl.ds(k0, tk)] if keys_on_lanes else key_ref[pl.ds(k0, tk), :]

    def store(k0, val):
        if keys_on_lanes:
            key_ref[:, pl.ds(k0, tk)] = val
        else:
            key_ref[pl.ds(k0, tk), :] = val

    fold = _fold_lanes if keys_on_lanes else _fold_rows

    def count(pred):
        def body(c, acc):
            k0 = pl.multiple_of(c * tk, tk)
            return acc + fold(jnp.where(pred(load(k0), k0), 1.0, 0.0))
        part = (nq, LANES) if keys_on_lanes else (8, nq)
        acc = lax.fori_loop(0, nch, body, jnp.zeros(part, F32))
        return jnp.sum(acc, axis=key_axis, keepdims=True)

    kf = float(topk)

    def unsettled(state):
        i, _, cnt_ans = state
        return jnp.logical_and(i < 32, jnp.max(jnp.abs(cnt_ans - kf)) > 0.0)

    def value_bit(state):
        i, ans, cnt_ans = state
        cand = ans | jnp.left_shift(jnp.int32(1), 31 - i)
        cand_s = cand ^ INT_MIN
        cnt = count(lambda blk, k0: blk >= cand_s)
        take = cnt >= kf
        return i + 1, jnp.where(take, cand, ans), jnp.where(take, cnt, cnt_ans)

    _, ans, cnt_ans = lax.while_loop(
        unsettled, value_bit,
        (jnp.int32(0), jnp.zeros(vec, jnp.int32),
         jnp.zeros(vec, F32) + jnp.asarray(nch * tk, F32)))
    thr = ans ^ INT_MIN
    settled = jnp.max(jnp.abs(cnt_ans - kf)) == 0.0

    nbits = max(1, int(np.ceil(np.log2(key_ref.shape[key_axis] + 1))))
    no_cut = lambda: jnp.full(vec, 2 ** 30, jnp.int32)

    def key_index(k0):
        return k0 + lax.broadcasted_iota(jnp.int32, blk_shape, key_axis)

    def resolve_ties():
        cnt_gt = count(lambda blk, k0: blk > thr)
        cnt_ge = count(lambda blk, k0: blk >= thr)
        need = kf - cnt_gt
        tie = jnp.logical_and(cnt_ge - cnt_gt > need, thr != KEY_NEG_INF)

        def tie_cut():
            def index_bit(i, x):
                cand = x | jnp.left_shift(jnp.int32(1), nbits - 1 - i)
                cnt = count(lambda blk, k0: jnp.logical_and(blk == thr, key_index(k0) < cand))
                return jnp.where(cnt < need, cand, x)
            return lax.fori_loop(0, nbits, index_bit, jnp.zeros(vec, jnp.int32))

        return lax.cond(jnp.max(jnp.where(tie, 1.0, 0.0)) > 0.0, tie_cut, no_cut)

    cut = lax.cond(settled, no_cut, resolve_ties)

    def write(c, _):
        k0 = pl.multiple_of(c * tk, tk)
        blk = load(k0)
        sel = jnp.logical_or(blk > thr, jnp.logical_and(blk == thr, key_index(k0) <= cut))
        store(k0, pltpu.bitcast(jnp.where(sel, 0.0, NEG_INF), jnp.int32))
        return 0

    lax.fori_loop(0, nch, write, 0)


def _dsa_kernel(qt_ref, iqt_ref, misct_ref, ik_ref, k_ref, vt_ref, o_ref, key_ref, *, tq, tk, topk):
    qi = pl.program_id(1)
    q0 = qi * tq
    jd = q0 // tk
    nch = jd + 1
    lo, hi = _pair_masks_t(tq)
    key_in = lax.broadcasted_iota(jnp.int32, (tk, tq), 0)
    q_pos = q0 + lax.broadcasted_iota(jnp.int32, (tk, tq), 1)

    misct = misct_ref[...]
    iqt = iqt_ref[...].astype(F32)
    iq_heads = []
    for hh in range(IDX_HEADS):
        pair = iqt[(hh // 2) * LANES:(hh // 2 + 1) * LANES]
        iq_heads.append((pair * (lo if hh % 2 == 0 else hi)).astype(BF16))
    iws = [misct[MISC_IW + hh:MISC_IW + hh + 1] for hh in range(IDX_HEADS)]

    def score_chunk(c, diag):
        k0 = pl.multiple_of(c * tk, tk)
        ikc = ik_ref[pl.ds(k0, tk), :]
        isc = jnp.zeros((tk, tq), F32)
        for hh in range(IDX_HEADS):
            s = jnp.dot(ikc, iq_heads[hh], preferred_element_type=F32)
            isc = isc + iws[hh] * jnp.maximum(s, 0.0)
        if diag:
            isc = jnp.where(k0 + key_in <= q_pos, isc, NEG_INF)
        key_ref[pl.ds(k0, tk), :] = _sort_key(isc)

    def score_body(c, _):
        score_chunk(c, False)
        return 0

    lax.fori_loop(0, jd, score_body, 0)
    score_chunk(jd, True)

    _select_topk_bias(key_ref, nch, tk, tq, topk, keys_on_lanes=False)

    qs = []
    for p in range(H_DSA // 2):
        qt = qt_ref[p * LANES:(p + 1) * LANES, :].astype(F32)
        qs += [(qt * lo).astype(BF16), (qt * hi).astype(BF16)]

    def pair_cols(hh):
        return slice((hh // 2) * LANES, (hh // 2 + 1) * LANES)

    def scores(c):
        k0 = pl.multiple_of(c * tk, tk)
        bias = pltpu.bitcast(key_ref[pl.ds(k0, tk), :], F32)
        out = []
        for hh in range(H_DSA):
            st = jnp.dot(k_ref[pl.ds(k0, tk), pair_cols(hh)], qs[hh],
                         preferred_element_type=F32) + bias
            out.append((st, _max_rows(st)))
        return tuple(out)

    def consume(s, c, carry):
        k0 = pl.multiple_of(c * tk, tk)
        return tuple(_flash_t_update(*s[hh], *carry[hh], vt_ref[pair_cols(hh), pl.ds(k0, tk)], True)
                     for hh in range(H_DSA))

    def trip(c, state):
        s, carry = state
        s_next = scores(c + 1)
        return s_next, consume(s, c, carry)

    s, carry = lax.fori_loop(0, jd, trip,
                             (scores(0), tuple(_flash_t_init(tq) for _ in range(H_DSA))))
    causal = jd * tk + key_in <= q_pos
    s = tuple(jnp.where(causal, sh, NEG_INF) for sh, _ in s)
    carry = consume(tuple((sh, _max_rows(sh)) for sh in s), jd, carry)
    outs = []
    for p in range(H_DSA // 2):
        (_, la, acca), (_, lb, accb) = carry[2 * p], carry[2 * p + 1]
        outs.append(_flash_t_out(la, acca) * lo + _flash_t_out(lb, accb) * hi)
    o_ref[...] = jnp.concatenate(outs, axis=0).astype(BF16)


def _dsa_attention(qt, iqt, misct, ikb, k, vt, b, t_len, tq, tk, topk):
    nq = t_len // tq
    colblk = lambda bi, i: (0, bi * nq + i)
    per_b = lambda bi, i: (bi, 0)
    return pl.pallas_call(
        functools.partial(_dsa_kernel, tq=tq, tk=tk, topk=topk),
        name="dsa_attention",
        grid=(b, nq),
        in_specs=[pl.BlockSpec((W_DSA, tq), colblk), pl.BlockSpec((W_IDX, tq), colblk),
                  pl.BlockSpec((LANES, tq), colblk), pl.BlockSpec((t_len, LANES), per_b),
                  pl.BlockSpec((t_len, W_DSA), per_b),
                  pl.BlockSpec((W_DSA, t_len), lambda bi, i: (0, bi))],
        out_specs=pl.BlockSpec((W_DSA, tq), colblk),
        out_shape=jax.ShapeDtypeStruct(qt.shape, BF16),
        scratch_shapes=[pltpu.VMEM((t_len, tq), jnp.int32)],
        compiler_params=_cparams(("parallel", "arbitrary")),
    )(qt, iqt, misct, ikb, k, vt)


def _mem_kernel(q_ref, mkv_ref, o_ref, *, tq):
    lo, hi = _pair_masks(tq)
    outs = []
    for p in range(H_MEM // 2):
        q2 = q_ref[:, p * LANES:(p + 1) * LANES].astype(F32)
        mk = mkv_ref[0, :, p * LANES:(p + 1) * LANES].astype(BF16)
        mv = mkv_ref[0, :, W_MEM + p * LANES:W_MEM + (p + 1) * LANES].astype(BF16)

        def head(qh):
            s = _nt_dot(qh, mk)
            pr = jnp.exp(s - jnp.max(s, axis=-1, keepdims=True))
            l = jnp.sum(pr, axis=-1, keepdims=True)
            return jnp.dot(pr.astype(BF16), mv, preferred_element_type=F32) / l

        outs.append(head((q2 * lo).astype(BF16)) * lo + head((q2 * hi).astype(BF16)) * hi)
    o_ref[...] = jnp.concatenate(outs, axis=1).astype(BF16)


def _mem_attention(q, mkv, b, t_len, tq):
    nq = t_len // tq
    n_mem = mkv.shape[1]
    return pl.pallas_call(
        functools.partial(_mem_kernel, tq=tq),
        grid=(b, nq),
        in_specs=[pl.BlockSpec((tq, W_MEM), lambda bi, i: (bi * nq + i, 0)),
                  pl.BlockSpec((1, n_mem, 2 * W_MEM), lambda bi, i: (bi, 0, 0))],
        out_specs=pl.BlockSpec((tq, W_MEM), lambda bi, i: (bi * nq + i, 0)),
        out_shape=jax.ShapeDtypeStruct(q.shape, BF16),
        compiler_params=_cparams(("parallel", "arbitrary")),
    )(q, mkv)


def _rms(x, g):
    return x * lax.rsqrt(jnp.mean(x * x, axis=-1, keepdims=True) + EPS) * g


def _merge_kernel(x_ref, of_ref, od_ref, os_ref, om_ref, gpre_ref, gpost_ref, wg_ref,
                  wf_ref, wd_ref, ws_ref, wm_ref, wo_ref, y_ref, *, transposed):
    x = x_ref[...]
    d = x.shape[1]
    hb = _rms(x, gpre_ref[...]).astype(BF16)
    merged = None
    for i, (o_ref, w_ref) in enumerate(((of_ref, wf_ref), (od_ref, wd_ref),
                                        (os_ref, ws_ref), (om_ref, wm_ref))):
        gate = jax.nn.sigmoid(_nt_dot(hb, wg_ref[i * d:(i + 1) * d, :]))
        o = o_ref[...]
        if transposed and i < 2:
            o = o.astype(F32).T
        br = jnp.dot(o.astype(BF16), w_ref[...], preferred_element_type=F32)
        merged = gate * br if merged is None else merged + gate * br
    y = jnp.dot(merged.astype(BF16), wo_ref[...], preferred_element_type=F32)
    y_ref[...] = x + _rms(y, gpost_ref[...])


def _merge(x, o_fox, o_dsa, o_sb, o_mem, gpre, gpost, wg, wf, wd, ws, wm, wo, tm, transposed):
    n, d = x.shape
    row = lambda i: (i, 0)
    const = lambda i: (0, 0)
    full = lambda a: pl.BlockSpec(a.shape, const)

    def o_spec(o, can_t):
        if transposed and can_t:
            return pl.BlockSpec((o.shape[0], tm), lambda i: (0, i))
        return pl.BlockSpec((tm, o.shape[1]), row)

    return pl.pallas_call(
        functools.partial(_merge_kernel, transposed=transposed),
        name="branch_merge",
        grid=(n // tm,),
        in_specs=[pl.BlockSpec((tm, d), row), o_spec(o_fox, True), o_spec(o_dsa, True),
                  o_spec(o_sb, False), o_spec(o_mem, False)]
        + [full(a) for a in (gpre, gpost, wg, wf, wd, ws, wm, wo)],
        out_specs=pl.BlockSpec((tm, d), row),
        out_shape=jax.ShapeDtypeStruct((n, d), F32),
        compiler_params=_cparams(("parallel",)),
    )(x, o_fox, o_dsa, o_sb, o_mem, gpre, gpost, wg, wf, wd, ws, wm, wo)


def _ffn_kernel(x_ref, gpre_ref, gpost_ref, wi_ref, wo_ref, y_ref, *, d_ff, tc):
    x = x_ref[...]
    hb = _rms(x, gpre_ref[...]).astype(BF16)
    y = jnp.zeros(x.shape, F32)
    for c0 in range(0, d_ff, tc):
        gate = jnp.dot(hb, wi_ref[:, c0:c0 + tc], preferred_element_type=F32)
        up = jnp.dot(hb, wi_ref[:, d_ff + c0:d_ff + c0 + tc], preferred_element_type=F32)
        act = (gate * jax.nn.sigmoid(gate)) * up
        y = y + jnp.dot(act.astype(BF16), wo_ref[c0:c0 + tc, :], preferred_element_type=F32)
    y_ref[...] = x + _rms(y, gpost_ref[...])


def _ffn(x, gpre, gpost, wi, wo, tm):
    n, d = x.shape
    d_ff = wo.shape[0]
    tc = 256 if d_ff % 256 == 0 else d_ff
    row = lambda i: (i, 0)
    const = lambda i: (0, 0)
    return pl.pallas_call(
        functools.partial(_ffn_kernel, d_ff=d_ff, tc=tc),
        name="swiglu_ffn",
        grid=(n // tm,),
        in_specs=[pl.BlockSpec((tm, d), row), pl.BlockSpec(gpre.shape, const),
                  pl.BlockSpec(gpost.shape, const), pl.BlockSpec(wi.shape, const),
                  pl.BlockSpec(wo.shape, const)],
        out_specs=pl.BlockSpec((tm, d), row),
        out_shape=jax.ShapeDtypeStruct((n, d), F32),
        compiler_params=_cparams(("parallel",)),
    )(x, gpre, gpost, wi, wo)


def _dec_score_kernel(pt_ref, iqh_ref, iwb_ref, iknew_ref, *rest, npages):
    pages = rest[:npages]
    o_ref = rest[npages]
    iqh = iqh_ref[0]
    iwb = iwb_ref[0]
    for j in range(npages):
        s = jnp.dot(iqh, pages[j][0, 0].astype(BF16), preferred_element_type=F32)
        o_ref[0, :, j * LANES:(j + 1) * LANES] = jnp.sum(iwb * jnp.maximum(s, 0.0), axis=0,
                                                         keepdims=True)
    ik_new = iknew_ref[0][:, :IDX_DIM].astype(BF16).astype(F32)
    s_new = jnp.sum(iqh.astype(F32) * ik_new, axis=-1, keepdims=True)
    isc_new = jnp.sum(iwb[:, 0:1] * jnp.maximum(s_new, 0.0), axis=0, keepdims=True)
    lane = lax.broadcasted_iota(jnp.int32, (1, LANES), 1)
    o_ref[0, :, npages * LANES:] = jnp.where(lane == 0, isc_new, NEG_INF)


def _dec_scores(pt_flat, iqh, iwb, ik_new, idxk_pool, layer, nb, npages):
    width = (npages + 1) * LANES
    seq = lambda bi, pt: (bi, 0, 0)

    def page_map(j):
        return lambda bi, pt: (layer, pt[bi * npages + j], 0, 0)

    grid_spec = pltpu.PrefetchScalarGridSpec(
        num_scalar_prefetch=1, grid=(nb,),
        in_specs=[pl.BlockSpec((1, 8, IDX_DIM), seq), pl.BlockSpec((1, 8, LANES), seq),
                  pl.BlockSpec((1, 1, LANES), seq)]
        + [pl.BlockSpec((1, 1, IDX_DIM, PAGE_SIZE), page_map(j)) for j in range(npages)],
        out_specs=pl.BlockSpec((1, 1, width), seq))
    return pl.pallas_call(
        functools.partial(_dec_score_kernel, npages=npages),
        name="decode_index_scores",
        grid_spec=grid_spec,
        out_shape=jax.ShapeDtypeStruct((nb, 1, width), F32),
        compiler_params=_cparams(("arbitrary",)),
    )(pt_flat, iqh, iwb, ik_new, *([idxk_pool] * npages))


def _dec_select_kernel(isc_ref, o_ref, key_ref, *, topk):
    rows, width = isc_ref.shape
    key_ref[...] = _sort_key(isc_ref[...])
    _select_topk_bias(key_ref, width // LANES, LANES, rows, topk, keys_on_lanes=True)
    o_ref[...] = pltpu.bitcast(key_ref[...], F32)


def _dec_select(isc, topk):
    return pl.pallas_call(
        functools.partial(_dec_select_kernel, topk=topk),
        out_shape=jax.ShapeDtypeStruct(isc.shape, F32),
        scratch_shapes=[pltpu.VMEM(isc.shape, jnp.int32)],
        compiler_params=pltpu.CompilerParams(vmem_limit_bytes=VMEM_LIMIT),
    )(isc)


def _head_rows(width):
    sub = lax.broadcasted_iota(jnp.int32, (8, width), 0)
    lane = lax.broadcasted_iota(jnp.int32, (8, width), 1)
    return ((lane >> 6) == sub).astype(F32)


def _dec_attn_kernel(pt_ref, fq_ref, dq_ref, sq_ref, mq_ref, fnew_ref, dnew_ref, lnew_ref,
                     bias_ref, mem_ref, *rest, npages):
    fox_pages = rest[0:npages]
    logf_pages = rest[npages:2 * npages]
    dsa_pages = rest[2 * npages:3 * npages]
    sb_pages = rest[3 * npages:4 * npages]
    of_ref, od_ref, os_ref, om_ref = rest[4 * npages:]

    row = lax.broadcasted_iota(jnp.int32, (PAGE_SIZE, PAGE_SIZE), 0)
    col = lax.broadcasted_iota(jnp.int32, (PAGE_SIZE, PAGE_SIZE), 1)
    after = (row > col).astype(BF16)

    def suffix_after(x):
        h1, h2, h3 = _split3(x)
        return (jnp.dot(h1, after, preferred_element_type=F32)
                + jnp.dot(h2, after, preferred_element_type=F32)
                + jnp.dot(h3, after, preferred_element_type=F32))

    def head_diag(o, mask):
        return jnp.sum(o * mask, axis=0, keepdims=True)

    mask_f = _head_rows(W_FOX)
    qf = (fq_ref[0].astype(F32) * mask_f).astype(BF16)
    k_new = fnew_ref[0][:, :W_FOX].astype(BF16).astype(F32)
    v_new = fnew_ref[0][:, W_FOX:].astype(BF16).astype(F32)
    s_new = jnp.sum(qf.astype(F32) * k_new, axis=-1, keepdims=True)
    run = lnew_ref[0][:, 0:1]
    scores = [None] * npages
    for j in range(npages - 1, -1, -1):
        kt = fox_pages[j][0, 0, 0].astype(BF16)
        lf = logf_pages[j][0, 0]
        scores[j] = jnp.dot(qf, kt, preferred_element_type=F32) + (suffix_after(lf) + run)
        run = run + jnp.sum(lf, axis=-1, keepdims=True)
    m = s_new
    for j in range(npages):
        m = jnp.maximum(m, jnp.max(scores[j], axis=-1, keepdims=True))
    p_new = jnp.exp(s_new - m)
    l = p_new
    acc = p_new.astype(BF16).astype(F32) * v_new
    for j in range(npages):
        pr = jnp.exp(scores[j] - m)
        l = l + jnp.sum(pr, axis=-1, keepdims=True)
        acc = acc + _nt_dot(pr.astype(BF16), fox_pages[j][0, 0, 1].astype(BF16))
    of_ref[0] = head_diag(acc / l, mask_f)

    mask_d = _head_rows(W_DSA)
    qd = (dq_ref[0].astype(F32) * mask_d).astype(BF16)
    k_new = dnew_ref[0][:, :W_DSA].astype(BF16).astype(F32)
    v_new = dnew_ref[0][:, W_DSA:].astype(BF16).astype(F32)
    s_new = (jnp.sum(qd.astype(F32) * k_new, axis=-1, keepdims=True)
             + bias_ref[0][:, npages * LANES:npages * LANES + 1])
    for j in range(npages):
        kt = dsa_pages[j][0, 0, 0].astype(BF16)
        scores[j] = (jnp.dot(qd, kt, preferred_element_type=F32)
                     + bias_ref[0][:, j * LANES:(j + 1) * LANES])
    m = s_new
    for j in range(npages):
        m = jnp.maximum(m, jnp.max(scores[j], axis=-1, keepdims=True))
    p_new = jnp.exp(s_new - m)
    l = p_new
    acc = p_new.astype(BF16).astype(F32) * v_new
    for j in range(npages):
        pr = jnp.exp(scores[j] - m)
        l = l + jnp.sum(pr, axis=-1, keepdims=True)
        acc = acc + _nt_dot(pr.astype(BF16), dsa_pages[j][0, 0, 1].astype(BF16))
    od_ref[0] = head_diag(acc / l, mask_d)

    mask_s = _head_rows(W_SB)
    qs = (sq_ref[0].astype(F32) * mask_s).astype(BF16)
    run = jnp.zeros((8, 1), F32)
    acc = jnp.zeros((8, W_SB), F32)
    for j in range(npages - 1, -1, -1):
        z = jnp.dot(qs, sb_pages[j][0, 0, 0].astype(BF16), preferred_element_type=F32)
        tl = _softplus_tail(z)
        log_beta = jnp.minimum(z, 0.0) - tl
        log_1mb = -jnp.maximum(z, 0.0) - tl
        a = jnp.exp(log_beta + (suffix_after(log_1mb) + run))
        acc = acc + _nt_dot(a.astype(BF16), sb_pages[j][0, 0, 1].astype(BF16))
        run = run + jnp.sum(log_1mb, axis=-1, keepdims=True)
    os_ref[0] = head_diag(acc, mask_s)

    mask_m = _head_rows(W_MEM)
    qm = (mq_ref[0].astype(F32) * mask_m).astype(BF16)
    s = jnp.dot(qm, mem_ref[0, 0, 0].astype(BF16), preferred_element_type=F32)
    pr = jnp.exp(s - jnp.max(s, axis=-1, keepdims=True))
    l = jnp.sum(pr, axis=-1, keepdims=True)
    o = _nt_dot(pr.astype(BF16), mem_ref[0, 0, 1].astype(BF16))
    om_ref[0] = head_diag(o / l, mask_m)


def _dec_attention(pt_flat, fq, dq, sq, mq, fnew, dnew, lnew, bias, mem, fox_pool, logf_pool,
                   dsa_pool, sb_pool, layer, nb, npages):
    seq = lambda bi, pt: (bi, 0, 0)

    def page_map(j, nd):
        return lambda bi, pt: (layer, pt[bi * npages + j]) + (0,) * nd

    def pages(pool):
        blk = (1, 1) + pool.shape[2:]
        return [pl.BlockSpec(blk, page_map(j, len(blk) - 2)) for j in range(npages)]

    def seq_spec(a):
        return pl.BlockSpec((1,) + a.shape[1:], seq)

    mem_spec = pl.BlockSpec((1, 1) + mem.shape[2:], lambda bi, pt: (layer, bi, 0, 0, 0))
    grid_spec = pltpu.PrefetchScalarGridSpec(
        num_scalar_prefetch=1, grid=(nb,),
        in_specs=[seq_spec(a) for a in (fq, dq, sq, mq, fnew, dnew, lnew, bias)] + [mem_spec]
        + pages(fox_pool) + pages(logf_pool) + pages(dsa_pool) + pages(sb_pool),
        out_specs=[pl.BlockSpec((1, 1, wd), seq) for wd in (W_FOX, W_DSA, W_SB, W_MEM)])
    return pl.pallas_call(
        functools.partial(_dec_attn_kernel, npages=npages),
        name="decode_attention",
        grid_spec=grid_spec,
        out_shape=[jax.ShapeDtypeStruct((nb, 1, wd), F32) for wd in (W_FOX, W_DSA, W_SB, W_MEM)],
        compiler_params=_cparams(("arbitrary",)),
    )(pt_flat, fq, dq, sq, mq, fnew, dnew, lnew, bias, mem,
      *([fox_pool] * npages), *([logf_pool] * npages), *([dsa_pool] * npages),
      *([sb_pool] * npages))


def _rope_tables(pos):
    rd = HEAD_DIM // 4
    half = rd // 2
    inv_freq = ROPE_THETA ** (-jnp.arange(half, dtype=F32) * 2.0 / rd)
    ang = pos.astype(F32)[:, None] * inv_freq[None, :]
    cos, sin = jnp.cos(ang), jnp.sin(ang)
    n = pos.shape[0]
    one = jnp.ones((n, HEAD_DIM - rd), F32)
    zero = jnp.zeros((n, HEAD_DIM - rd), F32)
    zh = jnp.zeros((n, half), F32)
    cos64 = jnp.concatenate([cos, cos, one], axis=1)
    sa64 = jnp.concatenate([-sin, zh, zero], axis=1)
    sb64 = jnp.concatenate([zh, sin, zero], axis=1)
    dup = lambda a: jnp.concatenate([a, a], axis=1)
    return dup(cos64), dup(sa64), dup(sb64), cos.T, sin.T


def _prep_w_in(w_in_l, b_forget_l, d_model):
    offs = np.cumsum([0, W_FOX, W_FOX, W_FOX, H_FOX, W_DSA, W_DSA, W_DSA, W_IDX, IDX_DIM,
                      IDX_HEADS, W_SB, W_SB, W_SB, W_MEM])
    (o_fq, o_fk, o_fv, o_ff, o_dq, o_dk, o_dv, o_iq, o_ik, o_iw, o_sq, o_sk, o_sv, o_mq,
     o_g) = [int(v) for v in offs]
    wt = jnp.transpose(w_in_l)
    sl = lambda o, n: wt[o:o + n]
    ik = sl(o_ik, IDX_DIM)
    misc = jnp.concatenate([sl(o_ff, H_FOX), sl(o_iw, IDX_HEADS),
                            jnp.zeros((LANES - H_FOX - IDX_HEADS, d_model), F32)], axis=0)
    wp = jnp.concatenate([sl(o_fq, 3 * W_FOX), sl(o_dq, 3 * W_DSA), sl(o_iq, W_IDX),
                          sl(o_sq, 3 * W_SB), sl(o_mq, W_MEM), ik, ik, misc], axis=0)
    wg = wt[o_g:]
    bfp = jnp.concatenate([b_forget_l, jnp.zeros((LANES - H_FOX,), F32)])[None, :]
    return wp.astype(BF16), wg.astype(BF16), bfp


def kernel(x_prompt, x_sample, cache_fox_kv, cache_fox_logf, cache_dsa_kv, cache_dsa_idxk,
           cache_sb_kv, cache_mem_kv, page_table, mem_prompt, w_in, b_forget, w_mem_kv,
           w_br_fox, w_br_dsa, w_br_sb, w_br_mem, w_out, w_ffn_in, w_ffn_out,
           g_mix_pre, g_mix_post, g_ffn_pre, g_ffn_post):
    bp, t_len, d = x_prompt.shape
    nb = x_sample.shape[0]
    depth = w_in.shape[0]
    npages = page_table.shape[1]
    past_len = npages * PAGE_SIZE
    n_pool = cache_fox_kv.shape[1]
    n_mem = mem_prompt.shape[1]
    n_p = bp * t_len

    tm = min(256, t_len)
    t_att = min(256, t_len)
    tq_dsa = min(256, t_len)
    tk_dsa = min(512, t_len)
    topk_p = min(DSA_TOPK_MAX, t_len // 4)
    topk_s = min(DSA_TOPK_MAX, (past_len + 1) // 4)
    assert tk_dsa >= topk_p and t_len % tk_dsa == 0 and t_len % tm == 0

    tabs_p = _rope_tables(jnp.arange(t_len, dtype=jnp.int32))
    tabs_s = _rope_tables(jnp.full((nb,), past_len, jnp.int32))
    pt_flat = page_table.reshape(-1).astype(jnp.int32)

    def kv_view(cache, width):
        view = jnp.transpose(cache, (0, 1, 3, 4, 5, 2))
        return view.reshape(cache.shape[:2] + (2, width, cache.shape[2]))

    fox_t = kv_view(cache_fox_kv, W_FOX)
    dsa_t = kv_view(cache_dsa_kv, W_DSA)
    sb_t = kv_view(cache_sb_kv, W_SB)
    mem_t = kv_view(cache_mem_kv, W_MEM)
    idxk_t = jnp.transpose(cache_dsa_idxk, (0, 1, 3, 2))
    logf_t = jnp.pad(jnp.transpose(cache_fox_logf, (0, 1, 3, 2)),
                     ((0, 0), (0, 0), (0, 8 - H_FOX), (0, 0)))

    xp = x_prompt.reshape(n_p, d)
    xs = x_sample.reshape(nb, d)
    mem_flat = mem_prompt.reshape(bp * n_mem, d)

    rows_p, rows_s, mem_p = [], [], []
    for l in range(depth):
        wp, wg, bfp = _prep_w_in(w_in[l], b_forget[l], d)
        gpre, gpost = g_mix_pre[l][None, :], g_mix_post[l][None, :]
        fpre, fpost = g_ffn_pre[l][None, :], g_ffn_post[l][None, :]
        wf, wd_, ws, wm = (w.astype(BF16) for w in (w_br_fox[l], w_br_dsa[l], w_br_sb[l], w_br_mem[l]))
        wo = w_out[l].astype(BF16)
        wi, wfo = w_ffn_in[l].astype(BF16), w_ffn_out[l].astype(BF16)

        mem_kv = _matmul(mem_flat, w_mem_kv[l].astype(BF16))
        (fqt, fkv, fk, fvt, dqt, dkv, dk, dvt, iqt, sq, skv, sk, sv, mq, ik32, ikb,
         misc, misct) = _proj(xp, gpre, wp, tabs_p, bfp, tm, True)
        logf = misc[:, :H_FOX].reshape(bp, t_len, H_FOX)
        c = jnp.cumsum(logf, axis=1)
        ckb = jnp.repeat(c.reshape(n_p, H_FOX), LANES, axis=1)
        crow = jnp.transpose(c, (0, 2, 1)).reshape(bp, H_FOX // 2, 2, t_len)
        crow = jnp.pad(crow, ((0, 0), (0, 0), (0, 6), (0, 0)))
        o_fox = _fox_attention(fqt, fk, fvt, crow, ckb, bp, t_len, t_att, tk_dsa)
        o_dsa = _dsa_attention(dqt, iqt, misct, ikb, dk, dvt, bp, t_len, tq_dsa, tk_dsa, topk_p)
        o_sb = _sb_attention(sq, sk, sv, bp, t_len, t_att)
        o_mem = _mem_attention(mq, mem_kv.reshape(bp, n_mem, 2 * W_MEM), bp, t_len, tm)
        xp = _merge(xp, o_fox, o_dsa, o_sb, o_mem, gpre, gpost, wg, wf, wd_, ws, wm, wo, tm, True)
        xp = _ffn(xp, fpre, fpost, wi, wfo, tm)
        rows_p.append((fkv.reshape(bp, t_len, 2, H_FOX, HEAD_DIM), logf,
                       dkv.reshape(bp, t_len, 2, H_DSA, HEAD_DIM),
                       ik32[:, :IDX_DIM].reshape(bp, t_len, IDX_DIM),
                       skv.reshape(bp, t_len, 2, H_SB, HEAD_DIM)))
        mem_p.append(mem_kv.reshape(bp, n_mem, 2, H_MEM, HEAD_DIM))

        (fq, fkv, fk, fv, dq, dkv, dk, dv, iq, sq, skv, sk, sv, mq, ik32, ikb,
         misc) = _proj(xs, gpre, wp, tabs_s, bfp, nb, False)
        r3 = lambda a: a.reshape(nb, 1, a.shape[1])
        iqh = jnp.pad(iq.reshape(nb, IDX_HEADS, IDX_DIM), ((0, 0), (0, 8 - IDX_HEADS), (0, 0)))
        iwb = jnp.pad(misc[:, MISC_IW:MISC_IW + IDX_HEADS], ((0, 0), (0, 8 - IDX_HEADS)))
        iwb = jnp.broadcast_to(iwb[:, :, None], (nb, 8, LANES))
        isc = _dec_scores(pt_flat, iqh, iwb, r3(ik32), idxk_t, l, nb, npages)
        bias = _dec_select(isc.reshape(nb, -1), topk_s).reshape(nb, 1, -1)
        lnew = jnp.pad(misc[:, :H_FOX], ((0, 0), (0, 8 - H_FOX)))
        lnew = jnp.broadcast_to(lnew[:, :, None], (nb, 8, LANES))
        o_fox, o_dsa, o_sb, o_mem = _dec_attention(
            pt_flat, r3(fq), r3(dq), r3(sq), r3(mq), r3(fkv), r3(dkv), lnew, bias,
            mem_t, fox_t, logf_t, dsa_t, sb_t, l, nb, npages)
        sq2 = lambda a: a.reshape(nb, a.shape[2])
        xs = _merge(xs, sq2(o_fox), sq2(o_dsa), sq2(o_sb), sq2(o_mem), gpre, gpost, wg, wf, wd_,
                    ws, wm, wo, nb, False)
        xs = _ffn(xs, fpre, fpost, wi, wfo, nb)
        rows_s.append((fkv.reshape(nb, 1, 2, H_FOX, HEAD_DIM), misc[:, :H_FOX].reshape(nb, 1, H_FOX),
                       dkv.reshape(nb, 1, 2, H_DSA, HEAD_DIM),
                       ik32[:, :IDX_DIM].reshape(nb, 1, IDX_DIM),
                       skv.reshape(nb, 1, 2, H_SB, HEAD_DIM)))

    stk = lambda rows, i: jnp.stack([r[i] for r in rows], axis=0)
    return (xp.reshape(bp, t_len, d), xs.reshape(nb, 1, d),
            stk(rows_p, 0), stk(rows_p, 1), stk(rows_p, 2), stk(rows_p, 3), stk(rows_p, 4),
            jnp.stack(mem_p, axis=0),
            stk(rows_s, 0), stk(rows_s, 1), stk(rows_s, 2), stk(rows_s, 3), stk(rows_s, 4))
```

```python
import functools

import jax
import jax.numpy as jnp
import numpy as np
from jax import lax
from jax.experimental import pallas as pl
from jax.experimental.pallas import tpu as pltpu

HEAD_DIM = 64
H_FOX = 6
H_DSA = 6
H_SB = 4
H_MEM = 4
IDX_HEADS = 4
IDX_DIM = 64
DSA_TOPK_MAX = 256
ROPE_THETA = 500000.0
N_BRANCH = 4
EPS = 1e-6
PAGE_SIZE = 128

LANES = 128
VMEM_LIMIT = 56 * 1024 * 1024

F32 = jnp.float32
BF16 = jnp.bfloat16
NEG_INF = float("-inf")
INT_MIN = -2 ** 31
SB_DEAD = -120.0

W_FOX = H_FOX * HEAD_DIM
W_DSA = H_DSA * HEAD_DIM
W_SB = H_SB * HEAD_DIM
W_MEM = H_MEM * HEAD_DIM
W_IDX = IDX_HEADS * IDX_DIM

C_FQ, C_FKV = 0, W_FOX
C_DQ = C_FKV + 2 * W_FOX
C_DK = C_DQ + W_DSA
C_DV = C_DK + W_DSA
C_IQ = C_DV + W_DSA
C_SQ = C_IQ + W_IDX
C_SKV = C_SQ + W_SB
C_MQ = C_SKV + 2 * W_SB
C_IK = C_MQ + W_MEM
C_MISC = C_IK + LANES
N_PROJ = C_MISC + LANES
MISC_IW = H_FOX


def _cparams(sem):
    return pltpu.CompilerParams(dimension_semantics=sem, vmem_limit_bytes=VMEM_LIMIT)


def _nt_dot(a, b):
    return lax.dot_general(a, b, (((1,), (1,)), ((), ())), preferred_element_type=F32)


def _softplus_tail(z):
    return jnp.log1p(jnp.exp(-jnp.abs(z)))


def _split3(x):
    hi = x.astype(BF16)
    r1 = x - hi.astype(F32)
    mid = r1.astype(BF16)
    lo = (r1 - mid.astype(F32)).astype(BF16)
    return hi, mid, lo


def _proj_kernel(x_ref, g_ref, w_ref, cos_ref, sa_ref, sb_ref, bf_ref, cost_ref, sint_ref,
                 fq_ref, fkv_ref, fk_ref, fv_ref, dq_ref, dkv_ref, dk_ref, dv_ref,
                 iq_ref, sq_ref, skv_ref, sk_ref, sv_ref, mq_ref, ik32_ref, ikb_ref, misc_ref,
                 *extra, transposed):
    x = x_ref[...]
    h = x * lax.rsqrt(jnp.mean(x * x, axis=-1, keepdims=True) + EPS)
    hb = (h * g_ref[...]).astype(BF16)
    cosf, sa, sb = cos_ref[...], sa_ref[...], sb_ref[...]
    scale = HEAD_DIM ** -0.5

    def mm(c0, n):
        return _nt_dot(hb, w_ref[c0:c0 + n, :])

    def mm_t(c0, n):
        return _nt_dot(w_ref[c0:c0 + n, :], hb)

    def rope(z):
        outs = []
        for j in range(z.shape[1] // LANES):
            zj = z[:, j * LANES:(j + 1) * LANES]
            outs.append(zj * cosf + pltpu.roll(zj, LANES - 8, 1) * sa + pltpu.roll(zj, 8, 1) * sb)
        return outs[0] if len(outs) == 1 else jnp.concatenate(outs, axis=1)

    def rope_t(zt):
        cos_t, sin_t = cost_ref[...], sint_ref[...]
        half = HEAD_DIM // 8
        parts = []
        for hh in range(zt.shape[0] // HEAD_DIM):
            base = hh * HEAD_DIM
            x1, x2 = zt[base:base + half], zt[base + half:base + 2 * half]
            parts += [x1 * cos_t - x2 * sin_t, x2 * cos_t + x1 * sin_t,
                      zt[base + 2 * half:base + HEAD_DIM]]
        return jnp.concatenate(parts, axis=0)

    fkv = mm(C_FKV, 2 * W_FOX)
    fkv_ref[...] = fkv
    fk_ref[...] = fkv[:, :W_FOX].astype(BF16)
    dk = rope(mm(C_DK, W_DSA))
    dv = mm(C_DV, W_DSA)
    dkv_ref[:, :W_DSA] = dk
    dkv_ref[:, W_DSA:] = dv
    dk_ref[...] = dk.astype(BF16)
    if transposed:
        fq_ref[...] = (mm_t(C_FQ, W_FOX) * scale).astype(BF16)
        fv_ref[...] = mm_t(C_FKV + W_FOX, W_FOX).astype(BF16)
        dq_ref[...] = (rope_t(mm_t(C_DQ, W_DSA)) * scale).astype(BF16)
        dv_ref[...] = mm_t(C_DV, W_DSA).astype(BF16)
        iq_ref[...] = rope_t(mm_t(C_IQ, W_IDX)).astype(BF16)
        extra[0][...] = mm_t(C_MISC, LANES)
    else:
        fq_ref[...] = (mm(C_FQ, W_FOX) * scale).astype(BF16)
        fv_ref[...] = fkv[:, W_FOX:].astype(BF16)
        dq_ref[...] = (rope(mm(C_DQ, W_DSA)) * scale).astype(BF16)
        dv_ref[...] = dv.astype(BF16)
        iq_ref[...] = rope(mm(C_IQ, W_IDX)).astype(BF16)

    sq_ref[...] = (mm(C_SQ, W_SB) * scale).astype(BF16)
    skv = mm(C_SKV, 2 * W_SB)
    skv_ref[...] = skv
    sk_ref[...] = skv[:, :W_SB].astype(BF16)
    sv_ref[...] = skv[:, W_SB:].astype(BF16)

    mq_ref[...] = (mm(C_MQ, W_MEM) * scale).astype(BF16)

    ik = rope(mm(C_IK, LANES))
    ik32_ref[...] = ik
    ikb_ref[...] = ik.astype(BF16)

    zm = mm(C_MISC, LANES)
    ff = zm + bf_ref[...]
    logf = -(jnp.maximum(-ff, 0.0) + _softplus_tail(ff))
    lane = lax.broadcasted_iota(jnp.int32, zm.shape, 1)
    misc_ref[...] = jnp.where(lane < H_FOX, logf, zm)


def _proj(x, g, w, tabs, bfp, tm, transposed):
    n, d = x.shape
    cosf, sa, sb, cos_t, sin_t = tabs
    nt = cosf.shape[0] // tm
    row = lambda i: (i, 0)
    col = lambda i: (0, i)
    tab = lambda i: (i % nt, 0)
    tab_t = lambda i: (0, i % nt)
    const = lambda i: (0, 0)
    widths = [(W_FOX, BF16, True), (2 * W_FOX, F32, False), (W_FOX, BF16, False), (W_FOX, BF16, True),
              (W_DSA, BF16, True), (2 * W_DSA, F32, False), (W_DSA, BF16, False), (W_DSA, BF16, True),
              (W_IDX, BF16, True), (W_SB, BF16, False), (2 * W_SB, F32, False), (W_SB, BF16, False),
              (W_SB, BF16, False), (W_MEM, BF16, False), (LANES, F32, False), (LANES, BF16, False),
              (LANES, F32, False)]
    if transposed:
        widths.append((LANES, F32, True))
    out_specs, out_shape = [], []
    for wd, dt, can_t in widths:
        if transposed and can_t:
            out_specs.append(pl.BlockSpec((wd, tm), col))
            out_shape.append(jax.ShapeDtypeStruct((wd, n), dt))
        else:
            out_specs.append(pl.BlockSpec((tm, wd), row))
            out_shape.append(jax.ShapeDtypeStruct((n, wd), dt))
    return pl.pallas_call(
        functools.partial(_proj_kernel, transposed=transposed),
        name="input_projection",
        grid=(n // tm,),
        in_specs=[pl.BlockSpec((tm, d), row), pl.BlockSpec((1, d), const),
                  pl.BlockSpec((N_PROJ, d), const),
                  pl.BlockSpec((tm, LANES), tab), pl.BlockSpec((tm, LANES), tab),
                  pl.BlockSpec((tm, LANES), tab), pl.BlockSpec((1, LANES), const),
                  pl.BlockSpec((8, tm), tab_t), pl.BlockSpec((8, tm), tab_t)],
        out_specs=out_specs,
        out_shape=out_shape,
        compiler_params=_cparams(("parallel",)),
    )(x, g, w, cosf, sa, sb, bfp, cos_t, sin_t)


def _matmul_kernel(a_ref, b_ref, o_ref):
    o_ref[...] = jnp.dot(a_ref[...].astype(BF16), b_ref[...], preferred_element_type=F32)


def _matmul(a, b):
    return pl.pallas_call(
        _matmul_kernel,
        out_shape=jax.ShapeDtypeStruct((a.shape[0], b.shape[1]), F32),
        compiler_params=pltpu.CompilerParams(vmem_limit_bytes=VMEM_LIMIT),
    )(a, b)


def _pair_masks(rows):
    lane = lax.broadcasted_iota(jnp.int32, (rows, LANES), 1)
    lo = (lane < HEAD_DIM).astype(F32)
    return lo, 1.0 - lo


def _fold_lanes(x):
    part = x[:, 0:LANES]
    for u in range(1, x.shape[1] // LANES):
        part = part + x[:, u * LANES:(u + 1) * LANES]
    return part


def _pair_masks_t(cols):
    sub = lax.broadcasted_iota(jnp.int32, (LANES, cols), 0)
    lo = (sub < HEAD_DIM).astype(F32)
    return lo, 1.0 - lo


def _tree_rows(x, op, nacc=4):
    parts = [x[i * 8:(i + 1) * 8] for i in range(x.shape[0] // 8)]
    accs = parts[:nacc]
    for i, part in enumerate(parts[nacc:]):
        accs[i % nacc] = op(accs[i % nacc], part)
    while len(accs) > 1:
        accs = [op(a, b) for a, b in zip(accs[0::2], accs[1::2])] + (accs[-1:] if len(accs) % 2 else [])
    return accs[0]


def _fold_rows(x):
    return _tree_rows(x, jnp.add)


def _max_rows(x):
    return _tree_rows(x, jnp.maximum)


def _flash_t_update(st, pmax, m, acc, vt1, guard):
    m_new = jnp.maximum(m, jnp.max(pmax, axis=0, keepdims=True))
    m_use = jnp.where(m_new == NEG_INF, 0.0, m_new) if guard else m_new
    alpha = jnp.exp(m - m_use)
    p = jnp.exp(st - m_use)
    acc = alpha * acc + jnp.dot(vt1, p.astype(BF16), preferred_element_type=F32)
    return m_new, acc


def _flash_t_init(cols):
    return jnp.full((1, cols), NEG_INF, F32), jnp.zeros((LANES, cols), F32)


def _flash_t_pair_out(acca, accb):
    return jnp.concatenate([acca[:HEAD_DIM] / acca[HEAD_DIM:HEAD_DIM + 1],
                            accb[HEAD_DIM:] / accb[0:1]], axis=0)


def _values_with_ones(vt, lo_b, hi_b):
    return vt * lo_b + hi_b, vt * hi_b + lo_b


def _fox_kernel(qt_ref, k_ref, vt_ref, crow_ref, ckb_ref, o_ref, *, tq, tk):
    qi = pl.program_id(2)
    q0 = pl.multiple_of(qi * tq, tq)
    jd = q0 // tk
    qt = qt_ref[...].astype(F32)
    lo, hi = _pair_masks_t(tq)
    qta, qtb = (qt * lo).astype(BF16), (qt * hi).astype(BF16)
    cqa = crow_ref[0, 0, 0:1, pl.ds(q0, tq)]
    cqb = crow_ref[0, 0, 1:2, pl.ds(q0, tq)]
    key_in = lax.broadcasted_iota(jnp.int32, (tk, tq), 0)
    q_pos = q0 + lax.broadcasted_iota(jnp.int32, (tk, tq), 1)
    reps = tq // LANES

    def scores(j):
        k0 = pl.multiple_of(j * tk, tk)
        kc = k_ref[pl.ds(k0, tk), :]
        ck = ckb_ref[pl.ds(k0, tk), :]
        cka = jnp.concatenate([ck[:, :LANES]] * reps, axis=1)
        ckb = jnp.concatenate([ck[:, LANES:]] * reps, axis=1)
        sa = jnp.dot(kc, qta, preferred_element_type=F32) + (cqa - cka)
        sb = jnp.dot(kc, qtb, preferred_element_type=F32) + (cqb - ckb)
        return (sa, _max_rows(sa)), (sb, _max_rows(sb))

    lo_b = (lax.broadcasted_iota(jnp.int32, (LANES, tk), 0) < HEAD_DIM).astype(F32).astype(BF16)
    hi_b = (1.0 - lo_b.astype(F32)).astype(BF16)

    def consume(s, j, carry):
        ca, cb = carry
        vta, vtb = _values_with_ones(vt_ref[:, pl.ds(pl.multiple_of(j * tk, tk), tk)], lo_b, hi_b)
        return _flash_t_update(*s[0], *ca, vta, False), _flash_t_update(*s[1], *cb, vtb, False)

    def trip(j, state):
        s, carry = state
        s_next = scores(j + 1)
        return s_next, consume(s, j, carry)

    s, carry = lax.fori_loop(0, jd, trip, (scores(0), (_flash_t_init(tq), _flash_t_init(tq))))
    causal = jd * tk + key_in <= q_pos
    s = tuple(jnp.where(causal, sh, NEG_INF) for sh, _ in s)
    (_, acca), (_, accb) = consume(tuple((sh, _max_rows(sh)) for sh in s), jd, carry)
    o_ref[...] = _flash_t_pair_out(acca, accb).astype(BF16)


def _fox_attention(qt, k, vt, crow, ckb, b, t_len, tq, tk):
    npair = H_FOX // 2
    nq = t_len // tq
    return pl.pallas_call(
        functools.partial(_fox_kernel, tq=tq, tk=tk),
        name="fox_attention",
        grid=(b, npair, nq),
        in_specs=[pl.BlockSpec((LANES, tq), lambda bi, p, i: (p, bi * nq + i)),
                  pl.BlockSpec((t_len, LANES), lambda bi, p, i: (bi, p)),
                  pl.BlockSpec((LANES, t_len), lambda bi, p, i: (p, bi)),
                  pl.BlockSpec((1, 1, 8, t_len), lambda bi, p, i: (bi, p, 0, 0)),
                  pl.BlockSpec((t_len, 2 * LANES), lambda bi, p, i: (bi, p))],
        out_specs=pl.BlockSpec((LANES, tq), lambda bi, p, i: (p, bi * nq + i)),
        out_shape=jax.ShapeDtypeStruct(qt.shape, BF16),
        compiler_params=_cparams(("parallel", "parallel", "arbitrary")),
    )(qt, k, vt, crow, ckb)


def _sb_kernel(q_ref, k_ref, v_ref, o_ref, *, t):
    qi = pl.program_id(2)
    q2 = q_ref[...].astype(F32)
    lo, hi = _pair_masks(t)
    row = lax.broadcasted_iota(jnp.int32, (t, t), 0)
    col = lax.broadcasted_iota(jnp.int32, (t, t), 1)
    strict = col < row
    after = (row > col).astype(BF16)

    qa, qb = (q2 * lo).astype(BF16), (q2 * hi).astype(BF16)

    def one(qh, kc, vc, run, acc, diag):
        z = _nt_dot(qh, kc)
        tl = _softplus_tail(z)
        log_beta = jnp.minimum(z, 0.0) - tl
        log_1mb = -jnp.maximum(z, 0.0) - tl
        if diag:
            log_1mb = jnp.where(strict, log_1mb, 0.0)
        h1, h2, h3 = _split3(log_1mb)
        tail = (jnp.dot(h1, after, preferred_element_type=F32)
                + jnp.dot(h2, after, preferred_element_type=F32)
                + jnp.dot(h3, after, preferred_element_type=F32))
        a = jnp.exp(log_beta + (tail + run))
        if diag:
            a = jnp.where(strict, a, 0.0)
        acc = acc + jnp.dot(a.astype(BF16), vc, preferred_element_type=F32)
        run = run + (tail[:, 0:1] + log_1mb[:, 0:1])
        return run, acc

    def chunk(j, carry, diag):
        (ra, acca), (rb, accb) = carry
        k0 = pl.multiple_of(j * t, t)
        kc = k_ref[pl.ds(k0, t), :]
        vc = v_ref[pl.ds(k0, t), :]
        return one(qa, kc, vc, ra, acca, diag), one(qb, kc, vc, rb, accb, diag)

    zero = (jnp.zeros((t, 1), F32), jnp.zeros((t, LANES), F32))
    carry = chunk(qi, (zero, zero), True)

    def live(state):
        j, ((ra, _), (rb, _)) = state
        return jnp.logical_and(j >= 0, jnp.max(jnp.maximum(ra, rb)) > SB_DEAD)

    def older(state):
        j, c = state
        return j - 1, chunk(j, c, False)

    _, ((_, acca), (_, accb)) = lax.while_loop(live, older, (qi - 1, carry))
    o_ref[...] = (acca * lo + accb * hi).astype(BF16)


def _sb_attention(q, k, v, b, t_len, t):
    npair = H_SB // 2
    nq = t_len // t
    return pl.pallas_call(
        functools.partial(_sb_kernel, t=t),
        name="stick_breaking_attention",
        grid=(b, npair, nq),
        in_specs=[pl.BlockSpec((t, LANES), lambda bi, p, i: (bi * nq + i, p)),
                  pl.BlockSpec((t_len, LANES), lambda bi, p, i: (bi, p)),
                  pl.BlockSpec((t_len, LANES), lambda bi, p, i: (bi, p))],
        out_specs=pl.BlockSpec((t, LANES), lambda bi, p, i: (bi * nq + i, p)),
        out_shape=jax.ShapeDtypeStruct(q.shape, BF16),
        compiler_params=_cparams(("parallel", "parallel", "arbitrary")),
    )(q, k, v)


def _sort_key(x):
    x = jnp.where(x == 0.0, 0.0, x)
    bits = pltpu.bitcast(x, jnp.int32)
    return bits ^ ((bits >> 31) & 0x7FFFFFFF)


def _select_topk_bias(key_ref, nch, tk, nq, topk, keys_on_lanes):
    key_axis = 1 if keys_on_lanes else 0
    vec = (nq, 1) if keys_on_lanes else (1, nq)

    def load(k0):
        return key_ref[:, pl.ds(k0, tk)] if keys_on_lanes else key_ref[pl.ds(k0, tk), :]

    def store(k0, val):
        if keys_on_lanes:
            key_ref[:, pl.ds(k0, tk)] = val
        else:
            key_ref[pl.ds(k0, tk), :] = val

    fold = _fold_lanes if keys_on_lanes else _fold_rows

    def count(pred):
        def body(c, acc):
            k0 = pl.multiple_of(c * tk, tk)
            return acc + fold(jnp.where(pred(load(k0), k0), 1.0, 0.0))
        part = (nq, LANES) if keys_on_lanes else (8, nq)
        acc = lax.fori_loop(0, nch, body, jnp.zeros(part, F32))
        return jnp.sum(acc, axis=key_axis, keepdims=True)

    kf = float(topk)

    def value_bit(i, ans):
        cand = ans | jnp.left_shift(jnp.int32(1), 31 - i)
        cand_s = cand ^ INT_MIN
        cnt = count(lambda blk, k0: blk >= cand_s)
        return jnp.where(cnt >= kf, cand, ans)

    ans = lax.fori_loop(0, 32, value_bit, jnp.zeros(vec, jnp.int32))
    thr = ans ^ INT_MIN
    need = kf - count(lambda blk, k0: blk > thr)

    r_i = lax.broadcasted_iota(jnp.int32, (tk, tk), 0)
    c_i = lax.broadcasted_iota(jnp.int32, (tk, tk), 1)
    tri = jnp.where(r_i <= c_i if keys_on_lanes else c_i <= r_i, 1.0, 0.0).astype(BF16)

    def write(c, base):
        k0 = pl.multiple_of(c * tk, tk)
        blk = load(k0)
        eq = blk == thr
        eqb = jnp.where(eq, 1.0, 0.0).astype(BF16)
        if keys_on_lanes:
            rank = jnp.dot(eqb, tri, preferred_element_type=F32)
            total = rank[:, tk - 1:tk]
        else:
            rank = jnp.dot(tri, eqb, preferred_element_type=F32)
            total = rank[tk - 1:tk, :]
        order = jnp.where(blk > thr, 0.0, jnp.where(eq, base + rank, jnp.inf))
        store(k0, pltpu.bitcast(jnp.where(order <= need, 0.0, NEG_INF), jnp.int32))
        return base + total

    lax.fori_loop(0, nch, write, jnp.zeros(vec, F32))


def _dsa_kernel(qt_ref, iqt_ref, misct_ref, ik_ref, k_ref, vt_ref, o_ref, key_ref, *, tq, tk, topk):
    qi = pl.program_id(1)
    q0 = qi * tq
    jd = q0 // tk
    nch = jd + 1
    lo, hi = _pair_masks_t(tq)
    key_in = lax.broadcasted_iota(jnp.int32, (tk, tq), 0)
    q_pos = q0 + lax.broadcasted_iota(jnp.int32, (tk, tq), 1)

    misct = misct_ref[...]
    iqt = iqt_ref[...].astype(F32)
    iq_heads = []
    for hh in range(IDX_HEADS):
        pair = iqt[(hh // 2) * LANES:(hh // 2 + 1) * LANES]
        iq_heads.append((pair * (lo if hh % 2 == 0 else hi)).astype(BF16))
    iws = [misct[MISC_IW + hh:MISC_IW + hh + 1] for hh in range(IDX_HEADS)]

    def score_chunk(c, diag):
        k0 = pl.multiple_of(c * tk, tk)
        ikc = ik_ref[pl.ds(k0, tk), :]
        isc = jnp.zeros((tk, tq), F32)
        for hh in range(IDX_HEADS):
            s = jnp.dot(ikc, iq_heads[hh], preferred_element_type=F32)
            isc = isc + iws[hh] * jnp.maximum(s, 0.0)
        if diag:
            isc = jnp.where(k0 + key_in <= q_pos, isc, NEG_INF)
        key_ref[pl.ds(k0, tk), :] = _sort_key(isc)

    def score_body(c, _):
        score_chunk(c, False)
        return 0

    lax.fori_loop(0, jd, score_body, 0)
    score_chunk(jd, True)

    _select_topk_bias(key_ref, nch, tk, tq, topk, keys_on_lanes=False)

    qs = []
    for p in range(H_DSA // 2):
        qt = qt_ref[p * LANES:(p + 1) * LANES, :].astype(F32)
        qs += [(qt * lo).astype(BF16), (qt * hi).astype(BF16)]

    def pair_cols(hh):
        return slice((hh // 2) * LANES, (hh // 2 + 1) * LANES)

    def scores(c):
        k0 = pl.multiple_of(c * tk, tk)
        bias = pltpu.bitcast(key_ref[pl.ds(k0, tk), :], F32)
        out = []
        for hh in range(H_DSA):
            st = jnp.dot(k_ref[pl.ds(k0, tk), pair_cols(hh)], qs[hh],
                         preferred_element_type=F32) + bias
            out.append((st, _max_rows(st)))
        return tuple(out)

    lo_b = (lax.broadcasted_iota(jnp.int32, (LANES, tk), 0) < HEAD_DIM).astype(F32).astype(BF16)
    hi_b = (1.0 - lo_b.astype(F32)).astype(BF16)

    def consume(s, c, carry):
        k0 = pl.multiple_of(c * tk, tk)
        out = []
        for p in range(H_DSA // 2):
            vts = _values_with_ones(vt_ref[p * LANES:(p + 1) * LANES, pl.ds(k0, tk)], lo_b, hi_b)
            for e in range(2):
                hh = 2 * p + e
                out.append(_flash_t_update(*s[hh], *carry[hh], vts[e], True))
        return tuple(out)

    def trip(c, state):
        s, carry = state
        s_next = scores(c + 1)
        return s_next, consume(s, c, carry)

    s, carry = lax.fori_loop(0, jd, trip,
                             (scores(0), tuple(_flash_t_init(tq) for _ in range(H_DSA))))
    causal = jd * tk + key_in <= q_pos
    s = tuple(jnp.where(causal, sh, NEG_INF) for sh, _ in s)
    carry = consume(tuple((sh, _max_rows(sh)) for sh in s), jd, carry)
    outs = [_flash_t_pair_out(carry[2 * p][1], carry[2 * p + 1][1]) for p in range(H_DSA // 2)]
    o_ref[...] = jnp.concatenate(outs, axis=0).astype(BF16)


def _dsa_attention(qt, iqt, misct, ikb, k, vt, b, t_len, tq, tk, topk):
    nq = t_len // tq
    colblk = lambda bi, i: (0, bi * nq + i)
    per_b = lambda bi, i: (bi, 0)
    return pl.pallas_call(
        functools.partial(_dsa_kernel, tq=tq, tk=tk, topk=topk),
        name="dsa_attention",
        grid=(b, nq),
        in_specs=[pl.BlockSpec((W_DSA, tq), colblk), pl.BlockSpec((W_IDX, tq), colblk),
                  pl.BlockSpec((LANES, tq), colblk), pl.BlockSpec((t_len, LANES), per_b),
                  pl.BlockSpec((t_len, W_DSA), per_b),
                  pl.BlockSpec((W_DSA, t_len), lambda bi, i: (0, bi))],
        out_specs=pl.BlockSpec((W_DSA, tq), colblk),
        out_shape=jax.ShapeDtypeStruct(qt.shape, BF16),
        scratch_shapes=[pltpu.VMEM((t_len, tq), jnp.int32)],
        compiler_params=_cparams(("parallel", "arbitrary")),
    )(qt, iqt, misct, ikb, k, vt)


def _mem_kernel(q_ref, mkv_ref, o_ref, *, tq):
    lo, hi = _pair_masks(tq)
    outs = []
    for p in range(H_MEM // 2):
        q2 = q_ref[:, p * LANES:(p + 1) * LANES].astype(F32)
        mk = mkv_ref[0, :, p * LANES:(p + 1) * LANES].astype(BF16)
        mv = mkv_ref[0, :, W_MEM + p * LANES:W_MEM + (p + 1) * LANES].astype(BF16)

        def head(qh):
            s = _nt_dot(qh, mk)
            pr = jnp.exp(s - jnp.max(s, axis=-1, keepdims=True))
            l = jnp.sum(pr, axis=-1, keepdims=True)
            return jnp.dot(pr.astype(BF16), mv, preferred_element_type=F32) / l

        outs.append(head((q2 * lo).astype(BF16)) * lo + head((q2 * hi).astype(BF16)) * hi)
    o_ref[...] = jnp.concatenate(outs, axis=1).astype(BF16)


def _mem_attention(q, mkv, b, t_len, tq):
    nq = t_len // tq
    n_mem = mkv.shape[1]
    return pl.pallas_call(
        functools.partial(_mem_kernel, tq=tq),
        grid=(b, nq),
        in_specs=[pl.BlockSpec((tq, W_MEM), lambda bi, i: (bi * nq + i, 0)),
                  pl.BlockSpec((1, n_mem, 2 * W_MEM), lambda bi, i: (bi, 0, 0))],
        out_specs=pl.BlockSpec((tq, W_MEM), lambda bi, i: (bi * nq + i, 0)),
        out_shape=jax.ShapeDtypeStruct(q.shape, BF16),
        compiler_params=_cparams(("parallel", "arbitrary")),
    )(q, mkv)


def _rms(x, g):
    return x * lax.rsqrt(jnp.mean(x * x, axis=-1, keepdims=True) + EPS) * g


def _merge_kernel(x_ref, of_ref, od_ref, os_ref, om_ref, gpre_ref, gpost_ref, wg_ref,
                  wf_ref, wd_ref, ws_ref, wm_ref, wo_ref, y_ref, *, transposed):
    x = x_ref[...]
    d = x.shape[1]
    hb = _rms(x, gpre_ref[...]).astype(BF16)
    merged = None
    for i, (o_ref, w_ref) in enumerate(((of_ref, wf_ref), (od_ref, wd_ref),
                                        (os_ref, ws_ref), (om_ref, wm_ref))):
        gate = jax.nn.sigmoid(_nt_dot(hb, wg_ref[i * d:(i + 1) * d, :]))
        o = o_ref[...]
        if transposed and i < 2:
            o = o.astype(F32).T
        br = jnp.dot(o.astype(BF16), w_ref[...], preferred_element_type=F32)
        merged = gate * br if merged is None else merged + gate * br
    y = jnp.dot(merged.astype(BF16), wo_ref[...], preferred_element_type=F32)
    y_ref[...] = x + _rms(y, gpost_ref[...])


def _merge(x, o_fox, o_dsa, o_sb, o_mem, gpre, gpost, wg, wf, wd, ws, wm, wo, tm, transposed):
    n, d = x.shape
    row = lambda i: (i, 0)
    const = lambda i: (0, 0)
    full = lambda a: pl.BlockSpec(a.shape, const)

    def o_spec(o, can_t):
        if transposed and can_t:
            return pl.BlockSpec((o.shape[0], tm), lambda i: (0, i))
        return pl.BlockSpec((tm, o.shape[1]), row)

    return pl.pallas_call(
        functools.partial(_merge_kernel, transposed=transposed),
        name="branch_merge",
        grid=(n // tm,),
        in_specs=[pl.BlockSpec((tm, d), row), o_spec(o_fox, True), o_spec(o_dsa, True),
                  o_spec(o_sb, False), o_spec(o_mem, False)]
        + [full(a) for a in (gpre, gpost, wg, wf, wd, ws, wm, wo)],
        out_specs=pl.BlockSpec((tm, d), row),
        out_shape=jax.ShapeDtypeStruct((n, d), F32),
        compiler_params=_cparams(("parallel",)),
    )(x, o_fox, o_dsa, o_sb, o_mem, gpre, gpost, wg, wf, wd, ws, wm, wo)


def _ffn_kernel(x_ref, gpre_ref, gpost_ref, wi_ref, wo_ref, y_ref, *, d_ff, tc):
    x = x_ref[...]
    hb = _rms(x, gpre_ref[...]).astype(BF16)
    y = jnp.zeros(x.shape, F32)
    for c0 in range(0, d_ff, tc):
        gate = jnp.dot(hb, wi_ref[:, c0:c0 + tc], preferred_element_type=F32)
        up = jnp.dot(hb, wi_ref[:, d_ff + c0:d_ff + c0 + tc], preferred_element_type=F32)
        act = (gate * jax.nn.sigmoid(gate)) * up
        y = y + jnp.dot(act.astype(BF16), wo_ref[c0:c0 + tc, :], preferred_element_type=F32)
    y_ref[...] = x + _rms(y, gpost_ref[...])


def _ffn(x, gpre, gpost, wi, wo, tm):
    n, d = x.shape
    d_ff = wo.shape[0]
    tc = 256 if d_ff % 256 == 0 else d_ff
    row = lambda i: (i, 0)
    const = lambda i: (0, 0)
    return pl.pallas_call(
        functools.partial(_ffn_kernel, d_ff=d_ff, tc=tc),
        name="swiglu_ffn",
        grid=(n // tm,),
        in_specs=[pl.BlockSpec((tm, d), row), pl.BlockSpec(gpre.shape, const),
                  pl.BlockSpec(gpost.shape, const), pl.BlockSpec(wi.shape, const),
                  pl.BlockSpec(wo.shape, const)],
        out_specs=pl.BlockSpec((tm, d), row),
        out_shape=jax.ShapeDtypeStruct((n, d), F32),
        compiler_params=_cparams(("parallel",)),
    )(x, gpre, gpost, wi, wo)


def _dec_score_kernel(pt_ref, iqh_ref, iwb_ref, iknew_ref, *rest, npages):
    pages = rest[:npages]
    o_ref = rest[npages]
    iqh = iqh_ref[0]
    iwb = iwb_ref[0]
    for j in range(npages):
        s = jnp.dot(iqh, pages[j][0, 0].astype(BF16), preferred_element_type=F32)
        o_ref[0, :, j * LANES:(j + 1) * LANES] = jnp.sum(iwb * jnp.maximum(s, 0.0), axis=0,
                                                         keepdims=True)
    ik_new = iknew_ref[0][:, :IDX_DIM].astype(BF16).astype(F32)
    s_new = jnp.sum(iqh.astype(F32) * ik_new, axis=-1, keepdims=True)
    isc_new = jnp.sum(iwb[:, 0:1] * jnp.maximum(s_new, 0.0), axis=0, keepdims=True)
    lane = lax.broadcasted_iota(jnp.int32, (1, LANES), 1)
    o_ref[0, :, npages * LANES:] = jnp.where(lane == 0, isc_new, NEG_INF)


def _dec_scores(pt_flat, iqh, iwb, ik_new, idxk_pool, layer, nb, npages):
    width = (npages + 1) * LANES
    seq = lambda bi, pt: (bi, 0, 0)

    def page_map(j):
        return lambda bi, pt: (layer, pt[bi * npages + j], 0, 0)

    grid_spec = pltpu.PrefetchScalarGridSpec(
        num_scalar_prefetch=1, grid=(nb,),
        in_specs=[pl.BlockSpec((1, 8, IDX_DIM), seq), pl.BlockSpec((1, 8, LANES), seq),
                  pl.BlockSpec((1, 1, LANES), seq)]
        + [pl.BlockSpec((1, 1, IDX_DIM, PAGE_SIZE), page_map(j)) for j in range(npages)],
        out_specs=pl.BlockSpec((1, 1, width), seq))
    return pl.pallas_call(
        functools.partial(_dec_score_kernel, npages=npages),
        name="decode_index_scores",
        grid_spec=grid_spec,
        out_shape=jax.ShapeDtypeStruct((nb, 1, width), F32),
        compiler_params=_cparams(("arbitrary",)),
    )(pt_flat, iqh, iwb, ik_new, *([idxk_pool] * npages))


def _dec_select_kernel(isc_ref, o_ref, key_ref, *, topk):
    rows, width = isc_ref.shape
    key_ref[...] = _sort_key(isc_ref[...])
    _select_topk_bias(key_ref, width // LANES, LANES, rows, topk, keys_on_lanes=True)
    o_ref[...] = pltpu.bitcast(key_ref[...], F32)


def _dec_select(isc, topk):
    return pl.pallas_call(
        functools.partial(_dec_select_kernel, topk=topk),
        out_shape=jax.ShapeDtypeStruct(isc.shape, F32),
        scratch_shapes=[pltpu.VMEM(isc.shape, jnp.int32)],
        compiler_params=pltpu.CompilerParams(vmem_limit_bytes=VMEM_LIMIT),
    )(isc)


def _head_rows(width):
    sub = lax.broadcasted_iota(jnp.int32, (8, width), 0)
    lane = lax.broadcasted_iota(jnp.int32, (8, width), 1)
    return ((lane >> 6) == sub).astype(F32)


def _dec_attn_kernel(pt_ref, fq_ref, dq_ref, sq_ref, mq_ref, fnew_ref, dnew_ref, lnew_ref,
                     bias_ref, mem_ref, *rest, npages):
    fox_pages = rest[0:npages]
    logf_pages = rest[npages:2 * npages]
    dsa_pages = rest[2 * npages:3 * npages]
    sb_pages = rest[3 * npages:4 * npages]
    of_ref, od_ref, os_ref, om_ref = rest[4 * npages:]

    row = lax.broadcasted_iota(jnp.int32, (PAGE_SIZE, PAGE_SIZE), 0)
    col = lax.broadcasted_iota(jnp.int32, (PAGE_SIZE, PAGE_SIZE), 1)
    after = (row > col).astype(BF16)

    def suffix_after(x):
        h1, h2, h3 = _split3(x)
        return (jnp.dot(h1, after, preferred_element_type=F32)
                + jnp.dot(h2, after, preferred_element_type=F32)
                + jnp.dot(h3, after, preferred_element_type=F32))

    def head_diag(o, mask):
        return jnp.sum(o * mask, axis=0, keepdims=True)

    mask_f = _head_rows(W_FOX)
    qf = (fq_ref[0].astype(F32) * mask_f).astype(BF16)
    k_new = fnew_ref[0][:, :W_FOX].astype(BF16).astype(F32)
    v_new = fnew_ref[0][:, W_FOX:].astype(BF16).astype(F32)
    s_new = jnp.sum(qf.astype(F32) * k_new, axis=-1, keepdims=True)
    run = lnew_ref[0][:, 0:1]
    scores = [None] * npages
    for j in range(npages - 1, -1, -1):
        kt = fox_pages[j][0, 0, 0].astype(BF16)
        lf = logf_pages[j][0, 0]
        scores[j] = jnp.dot(qf, kt, preferred_element_type=F32) + (suffix_after(lf) + run)
        run = run + jnp.sum(lf, axis=-1, keepdims=True)
    m = s_new
    for j in range(npages):
        m = jnp.maximum(m, jnp.max(scores[j], axis=-1, keepdims=True))
    p_new = jnp.exp(s_new - m)
    l = p_new
    acc = p_new.astype(BF16).astype(F32) * v_new
    for j in range(npages):
        pr = jnp.exp(scores[j] - m)
        l = l + jnp.sum(pr, axis=-1, keepdims=True)
        acc = acc + _nt_dot(pr.astype(BF16), fox_pages[j][0, 0, 1].astype(BF16))
    of_ref[0] = head_diag(acc / l, mask_f)

    mask_d = _head_rows(W_DSA)
    qd = (dq_ref[0].astype(F32) * mask_d).astype(BF16)
    k_new = dnew_ref[0][:, :W_DSA].astype(BF16).astype(F32)
    v_new = dnew_ref[0][:, W_DSA:].astype(BF16).astype(F32)
    s_new = (jnp.sum(qd.astype(F32) * k_new, axis=-1, keepdims=True)
             + bias_ref[0][:, npages * LANES:npages * LANES + 1])
    for j in range(npages):
        kt = dsa_pages[j][0, 0, 0].astype(BF16)
        scores[j] = (jnp.dot(qd, kt, preferred_element_type=F32)
                     + bias_ref[0][:, j * LANES:(j + 1) * LANES])
    m = s_new
    for j in range(npages):
        m = jnp.maximum(m, jnp.max(scores[j], axis=-1, keepdims=True))
    p_new = jnp.exp(s_new - m)
    l = p_new
    acc = p_new.astype(BF16).astype(F32) * v_new
    for j in range(npages):
        pr = jnp.exp(scores[j] - m)
        l = l + jnp.sum(pr, axis=-1, keepdims=True)
        acc = acc + _nt_dot(pr.astype(BF16), dsa_pages[j][0, 0, 1].astype(BF16))
    od_ref[0] = head_diag(acc / l, mask_d)

    mask_s = _head_rows(W_SB)
    qs = (sq_ref[0].astype(F32) * mask_s).astype(BF16)
    run = jnp.zeros((8, 1), F32)
    acc = jnp.zeros((8, W_SB), F32)
    for j in range(npages - 1, -1, -1):
        z = jnp.dot(qs, sb_pages[j][0, 0, 0].astype(BF16), preferred_element_type=F32)
        tl = _softplus_tail(z)
        log_beta = jnp.minimum(z, 0.0) - tl
        log_1mb = -jnp.maximum(z, 0.0) - tl
        a = jnp.exp(log_beta + (suffix_after(log_1mb) + run))
        acc = acc + _nt_dot(a.astype(BF16), sb_pages[j][0, 0, 1].astype(BF16))
        run = run + jnp.sum(log_1mb, axis=-1, keepdims=True)
    os_ref[0] = head_diag(acc, mask_s)

    mask_m = _head_rows(W_MEM)
    qm = (mq_ref[0].astype(F32) * mask_m).astype(BF16)
    s = jnp.dot(qm, mem_ref[0, 0, 0].astype(BF16), preferred_element_type=F32)
    pr = jnp.exp(s - jnp.max(s, axis=-1, keepdims=True))
    l = jnp.sum(pr, axis=-1, keepdims=True)
    o = _nt_dot(pr.astype(BF16), mem_ref[0, 0, 1].astype(BF16))
    om_ref[0] = head_diag(o / l, mask_m)


def _dec_attention(pt_flat, fq, dq, sq, mq, fnew, dnew, lnew, bias, mem, fox_pool, logf_pool,
                   dsa_pool, sb_pool, layer, nb, npages):
    seq = lambda bi, pt: (bi, 0, 0)

    def page_map(j, nd):
        return lambda bi, pt: (layer, pt[bi * npages + j]) + (0,) * nd

    def pages(pool):
        blk = (1, 1) + pool.shape[2:]
        return [pl.BlockSpec(blk, page_map(j, len(blk) - 2)) for j in range(npages)]

    def seq_spec(a):
        return pl.BlockSpec((1,) + a.shape[1:], seq)

    mem_spec = pl.BlockSpec((1, 1) + mem.shape[2:], lambda bi, pt: (layer, bi, 0, 0, 0))
    grid_spec = pltpu.PrefetchScalarGridSpec(
        num_scalar_prefetch=1, grid=(nb,),
        in_specs=[seq_spec(a) for a in (fq, dq, sq, mq, fnew, dnew, lnew, bias)] + [mem_spec]
        + pages(fox_pool) + pages(logf_pool) + pages(dsa_pool) + pages(sb_pool),
        out_specs=[pl.BlockSpec((1, 1, wd), seq) for wd in (W_FOX, W_DSA, W_SB, W_MEM)])
    return pl.pallas_call(
        functools.partial(_dec_attn_kernel, npages=npages),
        name="decode_attention",
        grid_spec=grid_spec,
        out_shape=[jax.ShapeDtypeStruct((nb, 1, wd), F32) for wd in (W_FOX, W_DSA, W_SB, W_MEM)],
        compiler_params=_cparams(("arbitrary",)),
    )(pt_flat, fq, dq, sq, mq, fnew, dnew, lnew, bias, mem,
      *([fox_pool] * npages), *([logf_pool] * npages), *([dsa_pool] * npages),
      *([sb_pool] * npages))


def _rope_tables(pos):
    rd = HEAD_DIM // 4
    half = rd // 2
    inv_freq = ROPE_THETA ** (-jnp.arange(half, dtype=F32) * 2.0 / rd)
    ang = pos.astype(F32)[:, None] * inv_freq[None, :]
    cos, sin = jnp.cos(ang), jnp.sin(ang)
    n = pos.shape[0]
    one = jnp.ones((n, HEAD_DIM - rd), F32)
    zero = jnp.zeros((n, HEAD_DIM - rd), F32)
    zh = jnp.zeros((n, half), F32)
    cos64 = jnp.concatenate([cos, cos, one], axis=1)
    sa64 = jnp.concatenate([-sin, zh, zero], axis=1)
    sb64 = jnp.concatenate([zh, sin, zero], axis=1)
    dup = lambda a: jnp.concatenate([a, a], axis=1)
    return dup(cos64), dup(sa64), dup(sb64), cos.T, sin.T


def _prep_w_in(w_in_l, b_forget_l, d_model):
    offs = np.cumsum([0, W_FOX, W_FOX, W_FOX, H_FOX, W_DSA, W_DSA, W_DSA, W_IDX, IDX_DIM,
                      IDX_HEADS, W_SB, W_SB, W_SB, W_MEM])
    (o_fq, o_fk, o_fv, o_ff, o_dq, o_dk, o_dv, o_iq, o_ik, o_iw, o_sq, o_sk, o_sv, o_mq,
     o_g) = [int(v) for v in offs]
    wt = jnp.transpose(w_in_l)
    sl = lambda o, n: wt[o:o + n]
    ik = sl(o_ik, IDX_DIM)
    misc = jnp.concatenate([sl(o_ff, H_FOX), sl(o_iw, IDX_HEADS),
                            jnp.zeros((LANES - H_FOX - IDX_HEADS, d_model), F32)], axis=0)
    wp = jnp.concatenate([sl(o_fq, 3 * W_FOX), sl(o_dq, 3 * W_DSA), sl(o_iq, W_IDX),
                          sl(o_sq, 3 * W_SB), sl(o_mq, W_MEM), ik, ik, misc], axis=0)
    wg = wt[o_g:]
    bfp = jnp.concatenate([b_forget_l, jnp.zeros((LANES - H_FOX,), F32)])[None, :]
    return wp.astype(BF16), wg.astype(BF16), bfp


def kernel(x_prompt, x_sample, cache_fox_kv, cache_fox_logf, cache_dsa_kv, cache_dsa_idxk,
           cache_sb_kv, cache_mem_kv, page_table, mem_prompt, w_in, b_forget, w_mem_kv,
           w_br_fox, w_br_dsa, w_br_sb, w_br_mem, w_out, w_ffn_in, w_ffn_out,
           g_mix_pre, g_mix_post, g_ffn_pre, g_ffn_post):
    bp, t_len, d = x_prompt.shape
    nb = x_sample.shape[0]
    depth = w_in.shape[0]
    npages = page_table.shape[1]
    past_len = npages * PAGE_SIZE
    n_pool = cache_fox_kv.shape[1]
    n_mem = mem_prompt.shape[1]
    n_p = bp * t_len

    tm = min(256, t_len)
    t_att = min(256, t_len)
    tq_dsa = min(256, t_len)
    tk_dsa = min(512, t_len)
    topk_p = min(DSA_TOPK_MAX, t_len // 4)
    topk_s = min(DSA_TOPK_MAX, (past_len + 1) // 4)
    assert tk_dsa >= topk_p and t_len % tk_dsa == 0 and t_len % tm == 0

    tabs_p = _rope_tables(jnp.arange(t_len, dtype=jnp.int32))
    tabs_s = _rope_tables(jnp.full((nb,), past_len, jnp.int32))
    pt_flat = page_table.reshape(-1).astype(jnp.int32)

    def kv_view(cache, width):
        view = jnp.transpose(cache, (0, 1, 3, 4, 5, 2))
        return view.reshape(cache.shape[:2] + (2, width, cache.shape[2]))

    fox_t = kv_view(cache_fox_kv, W_FOX)
    dsa_t = kv_view(cache_dsa_kv, W_DSA)
    sb_t = kv_view(cache_sb_kv, W_SB)
    mem_t = kv_view(cache_mem_kv, W_MEM)
    idxk_t = jnp.transpose(cache_dsa_idxk, (0, 1, 3, 2))
    logf_t = jnp.pad(jnp.transpose(cache_fox_logf, (0, 1, 3, 2)),
                     ((0, 0), (0, 0), (0, 8 - H_FOX), (0, 0)))

    xp = x_prompt.reshape(n_p, d)
    xs = x_sample.reshape(nb, d)
    mem_flat = mem_prompt.reshape(bp * n_mem, d)

    rows_p, rows_s, mem_p = [], [], []
    for l in range(depth):
        wp, wg, bfp = _prep_w_in(w_in[l], b_forget[l], d)
        gpre, gpost = g_mix_pre[l][None, :], g_mix_post[l][None, :]
        fpre, fpost = g_ffn_pre[l][None, :], g_ffn_post[l][None, :]
        wf, wd_, ws, wm = (w.astype(BF16) for w in (w_br_fox[l], w_br_dsa[l], w_br_sb[l], w_br_mem[l]))
        wo = w_out[l].astype(BF16)
        wi, wfo = w_ffn_in[l].astype(BF16), w_ffn_out[l].astype(BF16)

        mem_kv = _matmul(mem_flat, w_mem_kv[l].astype(BF16))
        (fqt, fkv, fk, fvt, dqt, dkv, dk, dvt, iqt, sq, skv, sk, sv, mq, ik32, ikb,
         misc, misct) = _proj(xp, gpre, wp, tabs_p, bfp, tm, True)
        logf = misc[:, :H_FOX].reshape(bp, t_len, H_FOX)
        c = jnp.cumsum(logf, axis=1)
        ckb = jnp.repeat(c.reshape(n_p, H_FOX), LANES, axis=1)
        crow = jnp.transpose(c, (0, 2, 1)).reshape(bp, H_FOX // 2, 2, t_len)
        crow = jnp.pad(crow, ((0, 0), (0, 0), (0, 6), (0, 0)))
        o_fox = _fox_attention(fqt, fk, fvt, crow, ckb, bp, t_len, t_att, tk_dsa)
        o_dsa = _dsa_attention(dqt, iqt, misct, ikb, dk, dvt, bp, t_len, tq_dsa, tk_dsa, topk_p)
        o_sb = _sb_attention(sq, sk, sv, bp, t_len, t_att)
        o_mem = _mem_attention(mq, mem_kv.reshape(bp, n_mem, 2 * W_MEM), bp, t_len, tm)
        xp = _merge(xp, o_fox, o_dsa, o_sb, o_mem, gpre, gpost, wg, wf, wd_, ws, wm, wo, tm, True)
        xp = _ffn(xp, fpre, fpost, wi, wfo, tm)
        rows_p.append((fkv.reshape(bp, t_len, 2, H_FOX, HEAD_DIM), logf,
                       dkv.reshape(bp, t_len, 2, H_DSA, HEAD_DIM),
                       ik32[:, :IDX_DIM].reshape(bp, t_len, IDX_DIM),
                       skv.reshape(bp, t_len, 2, H_SB, HEAD_DIM)))
        mem_p.append(mem_kv.reshape(bp, n_mem, 2, H_MEM, HEAD_DIM))

        (fq, fkv, fk, fv, dq, dkv, dk, dv, iq, sq, skv, sk, sv, mq, ik32, ikb,
         misc) = _proj(xs, gpre, wp, tabs_s, bfp, nb, False)
        r3 = lambda a: a.reshape(nb, 1, a.shape[1])
        iqh = jnp.pad(iq.reshape(nb, IDX_HEADS, IDX_DIM), ((0, 0), (0, 8 - IDX_HEADS), (0, 0)))
        iwb = jnp.pad(misc[:, MISC_IW:MISC_IW + IDX_HEADS], ((0, 0), (0, 8 - IDX_HEADS)))
        iwb = jnp.broadcast_to(iwb[:, :, None], (nb, 8, LANES))
        isc = _dec_scores(pt_flat, iqh, iwb, r3(ik32), idxk_t, l, nb, npages)
        bias = _dec_select(isc.reshape(nb, -1), topk_s).reshape(nb, 1, -1)
        lnew = jnp.pad(misc[:, :H_FOX], ((0, 0), (0, 8 - H_FOX)))
        lnew = jnp.broadcast_to(lnew[:, :, None], (nb, 8, LANES))
        o_fox, o_dsa, o_sb, o_mem = _dec_attention(
            pt_flat, r3(fq), r3(dq), r3(sq), r3(mq), r3(fkv), r3(dkv), lnew, bias,
            mem_t, fox_t, logf_t, dsa_t, sb_t, l, nb, npages)
        sq2 = lambda a: a.reshape(nb, a.shape[2])
        xs = _merge(xs, sq2(o_fox), sq2(o_dsa), sq2(o_sb), sq2(o_mem), gpre, gpost, wg, wf, wd_,
                    ws, wm, wo, nb, False)
        xs = _ffn(xs, fpre, fpost, wi, wfo, nb)
        rows_s.append((fkv.reshape(nb, 1, 2, H_FOX, HEAD_DIM), misc[:, :H_FOX].reshape(nb, 1, H_FOX),
                       dkv.reshape(nb, 1, 2, H_DSA, HEAD_DIM),
                       ik32[:, :IDX_DIM].reshape(nb, 1, IDX_DIM),
                       skv.reshape(nb, 1, 2, H_SB, HEAD_DIM)))

    stk = lambda rows, i: jnp.stack([r[i] for r in rows], axis=0)
    return (xp.reshape(bp, t_len, d), xs.reshape(nb, 1, d),
            stk(rows_p, 0), stk(rows_p, 1), stk(rows_p, 2), stk(rows_p, 3), stk(rows_p, 4),
            jnp.stack(mem_p, axis=0),
            stk(rows_s, 0), stk(rows_s, 1), stk(rows_s, 2), stk(rows_s, 3), stk(rows_s, 4))
```

```python
import functools

import jax
import jax.numpy as jnp
import numpy as np
from jax import lax
from jax.experimental import pallas as pl
from jax.experimental.pallas import tpu as pltpu

HEAD_DIM = 64
H_FOX = 6
H_DSA = 6
H_SB = 4
H_MEM = 4
IDX_HEADS = 4
IDX_DIM = 64
DSA_TOPK_MAX = 256
ROPE_THETA = 500000.0
N_BRANCH = 4
EPS = 1e-6
PAGE_SIZE = 128

LANES = 128
VMEM_LIMIT = 56 * 1024 * 1024

F32 = jnp.float32
BF16 = jnp.bfloat16
NEG_INF = float("-inf")
INT_MIN = -2 ** 31
SB_DEAD = -120.0
LOG2_E = 1.4426950408889634

W_FOX = H_FOX * HEAD_DIM
W_DSA = H_DSA * HEAD_DIM
W_SB = H_SB * HEAD_DIM
W_MEM = H_MEM * HEAD_DIM
W_IDX = IDX_HEADS * IDX_DIM

C_FQ, C_FKV = 0, W_FOX
C_DQ = C_FKV + 2 * W_FOX
C_DK = C_DQ + W_DSA
C_DV = C_DK + W_DSA
C_IQ = C_DV + W_DSA
C_SQ = C_IQ + W_IDX
C_SKV = C_SQ + W_SB
C_MQ = C_SKV + 2 * W_SB
C_IK = C_MQ + W_MEM
C_MISC = C_IK + LANES
N_PROJ = C_MISC + LANES
MISC_IW = H_FOX


def _cparams(sem):
    return pltpu.CompilerParams(dimension_semantics=sem, vmem_limit_bytes=VMEM_LIMIT)


def _nt_dot(a, b):
    return lax.dot_general(a, b, (((1,), (1,)), ((), ())), preferred_element_type=F32)


def _softplus_tail(z):
    return jnp.log1p(jnp.exp(-jnp.abs(z)))


def _split3(x):
    hi = x.astype(BF16)
    r1 = x - hi.astype(F32)
    mid = r1.astype(BF16)
    lo = (r1 - mid.astype(F32)).astype(BF16)
    return hi, mid, lo


def _proj_kernel(x_ref, g_ref, w_ref, cos_ref, sa_ref, sb_ref, bf_ref, cost_ref, sint_ref,
                 fq_ref, fkv_ref, fk_ref, fv_ref, dq_ref, dkv_ref, dk_ref, dv_ref,
                 iq_ref, sq_ref, skv_ref, sk_ref, sv_ref, mq_ref, ik32_ref, ikb_ref, misc_ref,
                 *extra, transposed):
    x = x_ref[...]
    h = x * lax.rsqrt(jnp.mean(x * x, axis=-1, keepdims=True) + EPS)
    hb = (h * g_ref[...]).astype(BF16)
    cosf, sa, sb = cos_ref[...], sa_ref[...], sb_ref[...]
    scale = HEAD_DIM ** -0.5

    def mm(c0, n):
        return _nt_dot(hb, w_ref[c0:c0 + n, :])

    def mm_t(c0, n):
        return _nt_dot(w_ref[c0:c0 + n, :], hb)

    def rope(z):
        outs = []
        for j in range(z.shape[1] // LANES):
            zj = z[:, j * LANES:(j + 1) * LANES]
            outs.append(zj * cosf + pltpu.roll(zj, LANES - 8, 1) * sa + pltpu.roll(zj, 8, 1) * sb)
        return outs[0] if len(outs) == 1 else jnp.concatenate(outs, axis=1)

    def rope_t(zt):
        cos_t, sin_t = cost_ref[...], sint_ref[...]
        half = HEAD_DIM // 8
        parts = []
        for hh in range(zt.shape[0] // HEAD_DIM):
            base = hh * HEAD_DIM
            x1, x2 = zt[base:base + half], zt[base + half:base + 2 * half]
            parts += [x1 * cos_t - x2 * sin_t, x2 * cos_t + x1 * sin_t,
                      zt[base + 2 * half:base + HEAD_DIM]]
        return jnp.concatenate(parts, axis=0)

    fkv = mm(C_FKV, 2 * W_FOX)
    fkv_ref[...] = fkv
    fk_ref[...] = fkv[:, :W_FOX].astype(BF16)
    dk = rope(mm(C_DK, W_DSA))
    dv = mm(C_DV, W_DSA)
    dkv_ref[:, :W_DSA] = dk
    dkv_ref[:, W_DSA:] = dv
    dk_ref[...] = dk.astype(BF16)
    if transposed:
        scale2 = scale * LOG2_E
        fq_ref[...] = (mm_t(C_FQ, W_FOX) * scale2).astype(BF16)
        fv_ref[...] = mm_t(C_FKV + W_FOX, W_FOX).astype(BF16)
        dq_ref[...] = (rope_t(mm_t(C_DQ, W_DSA)) * scale2).astype(BF16)
        dv_ref[...] = mm_t(C_DV, W_DSA).astype(BF16)
        iq_ref[...] = rope_t(mm_t(C_IQ, W_IDX)).astype(BF16)
        extra[0][...] = mm_t(C_MISC, LANES)
    else:
        fq_ref[...] = (mm(C_FQ, W_FOX) * scale).astype(BF16)
        fv_ref[...] = fkv[:, W_FOX:].astype(BF16)
        dq_ref[...] = (rope(mm(C_DQ, W_DSA)) * scale).astype(BF16)
        dv_ref[...] = dv.astype(BF16)
        iq_ref[...] = rope(mm(C_IQ, W_IDX)).astype(BF16)

    sq_ref[...] = (mm(C_SQ, W_SB) * scale).astype(BF16)
    skv = mm(C_SKV, 2 * W_SB)
    skv_ref[...] = skv
    sk_ref[...] = skv[:, :W_SB].astype(BF16)
    sv_ref[...] = skv[:, W_SB:].astype(BF16)

    mq_ref[...] = (mm(C_MQ, W_MEM) * scale).astype(BF16)

    ik = rope(mm(C_IK, LANES))
    ik32_ref[...] = ik
    ikb_ref[...] = ik.astype(BF16)

    zm = mm(C_MISC, LANES)
    ff = zm + bf_ref[...]
    logf = -(jnp.maximum(-ff, 0.0) + _softplus_tail(ff))
    lane = lax.broadcasted_iota(jnp.int32, zm.shape, 1)
    misc_ref[...] = jnp.where(lane < H_FOX, logf, zm)


def _proj(x, g, w, tabs, bfp, tm, transposed):
    n, d = x.shape
    cosf, sa, sb, cos_t, sin_t = tabs
    nt = cosf.shape[0] // tm
    row = lambda i: (i, 0)
    col = lambda i: (0, i)
    tab = lambda i: (i % nt, 0)
    tab_t = lambda i: (0, i % nt)
    const = lambda i: (0, 0)
    widths = [(W_FOX, BF16, True), (2 * W_FOX, F32, False), (W_FOX, BF16, False), (W_FOX, BF16, True),
              (W_DSA, BF16, True), (2 * W_DSA, F32, False), (W_DSA, BF16, False), (W_DSA, BF16, True),
              (W_IDX, BF16, True), (W_SB, BF16, False), (2 * W_SB, F32, False), (W_SB, BF16, False),
              (W_SB, BF16, False), (W_MEM, BF16, False), (LANES, F32, False), (LANES, BF16, False),
              (LANES, F32, False)]
    if transposed:
        widths.append((LANES, F32, True))
    out_specs, out_shape = [], []
    for wd, dt, can_t in widths:
        if transposed and can_t:
            out_specs.append(pl.BlockSpec((wd, tm), col))
            out_shape.append(jax.ShapeDtypeStruct((wd, n), dt))
        else:
            out_specs.append(pl.BlockSpec((tm, wd), row))
            out_shape.append(jax.ShapeDtypeStruct((n, wd), dt))
    return pl.pallas_call(
        functools.partial(_proj_kernel, transposed=transposed),
        name="input_projection",
        grid=(n // tm,),
        in_specs=[pl.BlockSpec((tm, d), row), pl.BlockSpec((1, d), const),
                  pl.BlockSpec((N_PROJ, d), const),
                  pl.BlockSpec((tm, LANES), tab), pl.BlockSpec((tm, LANES), tab),
                  pl.BlockSpec((tm, LANES), tab), pl.BlockSpec((1, LANES), const),
                  pl.BlockSpec((8, tm), tab_t), pl.BlockSpec((8, tm), tab_t)],
        out_specs=out_specs,
        out_shape=out_shape,
        compiler_params=_cparams(("parallel",)),
    )(x, g, w, cosf, sa, sb, bfp, cos_t, sin_t)


def _matmul_kernel(a_ref, b_ref, o_ref):
    o_ref[...] = jnp.dot(a_ref[...].astype(BF16), b_ref[...], preferred_element_type=F32)


def _matmul(a, b):
    return pl.pallas_call(
        _matmul_kernel,
        out_shape=jax.ShapeDtypeStruct((a.shape[0], b.shape[1]), F32),
        compiler_params=pltpu.CompilerParams(vmem_limit_bytes=VMEM_LIMIT),
    )(a, b)


def _pair_masks(rows):
    lane = lax.broadcasted_iota(jnp.int32, (rows, LANES), 1)
    lo = (lane < HEAD_DIM).astype(F32)
    return lo, 1.0 - lo


def _fold_lanes(x):
    part = x[:, 0:LANES]
    for u in range(1, x.shape[1] // LANES):
        part = part + x[:, u * LANES:(u + 1) * LANES]
    return part


def _pair_masks_t(cols):
    sub = lax.broadcasted_iota(jnp.int32, (LANES, cols), 0)
    lo = (sub < HEAD_DIM).astype(F32)
    return lo, 1.0 - lo


def _tree_rows(x, op, nacc=4):
    parts = [x[i * 8:(i + 1) * 8] for i in range(x.shape[0] // 8)]
    accs = parts[:nacc]
    for i, part in enumerate(parts[nacc:]):
        accs[i % nacc] = op(accs[i % nacc], part)
    while len(accs) > 1:
        accs = [op(a, b) for a, b in zip(accs[0::2], accs[1::2])] + (accs[-1:] if len(accs) % 2 else [])
    return accs[0]


def _fold_rows(x):
    return _tree_rows(x, jnp.add)


def _max_rows(x):
    return _tree_rows(x, jnp.maximum)


def _flash_t_update(st, pmax, m, acc, vt1, guard):
    m_new = jnp.maximum(m, jnp.max(pmax, axis=0, keepdims=True))
    m_use = jnp.where(m_new == NEG_INF, 0.0, m_new) if guard else m_new
    alpha = jnp.exp2(m - m_use)
    p = jnp.exp2(st - m_use)
    acc = alpha * acc + jnp.dot(vt1, p.astype(BF16), preferred_element_type=F32)
    return m_new, acc


def _flash_t_init(cols):
    return jnp.full((1, cols), NEG_INF, F32), jnp.zeros((LANES, cols), F32)


def _flash_t_pair_out(acca, accb):
    return jnp.concatenate([acca[:HEAD_DIM] / acca[HEAD_DIM:HEAD_DIM + 1],
                            accb[HEAD_DIM:] / accb[0:1]], axis=0)


def _values_with_ones(vt, lo_b, hi_b):
    return vt * lo_b + hi_b, vt * hi_b + lo_b


FOX_AUG = 6


def _fox_kernel(qt_ref, k_ref, vt_ref, o_ref, *, tq, tk):
    qi = pl.program_id(2)
    q0 = pl.multiple_of(qi * tq, tq)
    jd = q0 // tk
    qt = qt_ref[...].astype(F32)
    sub = lax.broadcasted_iota(jnp.int32, (2 * LANES, tq), 0)
    aug = sub - LANES
    lo = jnp.logical_or(sub < HEAD_DIM, jnp.logical_and(aug >= 0, aug < FOX_AUG))
    hi = jnp.logical_or(jnp.logical_and(sub >= HEAD_DIM, sub < LANES),
                        jnp.logical_and(aug >= FOX_AUG, aug < 2 * FOX_AUG))
    qta = jnp.where(lo, qt, 0.0).astype(BF16)
    qtb = jnp.where(hi, qt, 0.0).astype(BF16)
    key_in = lax.broadcasted_iota(jnp.int32, (tk, tq), 0)
    q_pos = q0 + lax.broadcasted_iota(jnp.int32, (tk, tq), 1)

    def scores(j):
        k0 = pl.multiple_of(j * tk, tk)
        kc = k_ref[pl.ds(k0, tk), :]
        sa = jnp.dot(kc, qta, preferred_element_type=F32)
        sb = jnp.dot(kc, qtb, preferred_element_type=F32)
        return (sa, _max_rows(sa)), (sb, _max_rows(sb))

    lo_b = (lax.broadcasted_iota(jnp.int32, (LANES, tk), 0) < HEAD_DIM).astype(F32).astype(BF16)
    hi_b = (1.0 - lo_b.astype(F32)).astype(BF16)

    def consume(s, j, carry):
        ca, cb = carry
        vta, vtb = _values_with_ones(vt_ref[:, pl.ds(pl.multiple_of(j * tk, tk), tk)], lo_b, hi_b)
        return _flash_t_update(*s[0], *ca, vta, False), _flash_t_update(*s[1], *cb, vtb, False)

    def trip(j, state):
        s, carry = state
        s_next = scores(j + 1)
        return s_next, consume(s, j, carry)

    s, carry = lax.fori_loop(0, jd, trip, (scores(0), (_flash_t_init(tq), _flash_t_init(tq))))
    causal = jd * tk + key_in <= q_pos
    s = tuple(jnp.where(causal, sh, NEG_INF) for sh, _ in s)
    (_, acca), (_, accb) = consume(tuple((sh, _max_rows(sh)) for sh in s), jd, carry)
    o_ref[...] = _flash_t_pair_out(acca, accb).astype(BF16)


def _fox_operands(qt, k, c):
    n = k.shape[0]
    npair = H_FOX // 2

    def chop(v):
        bits = lax.bitcast_convert_type(v, jnp.uint32) & jnp.uint32(0xFFFF0000)
        return lax.bitcast_convert_type(bits, F32)

    def split(v):
        hi = chop(v)
        mid = chop(v - hi)
        lo = chop(v - hi - mid)
        return jnp.stack([hi, mid, lo], axis=-1).astype(BF16)

    pos = split(c)
    neg = split(-c)
    ones = jnp.ones((n, H_FOX, 3), BF16)
    pad = jnp.zeros((n, npair, LANES - 2 * FOX_AUG), BF16)
    k_aug = jnp.concatenate([neg, ones], axis=-1).reshape(n, npair, 2 * FOX_AUG)
    q_aug = jnp.concatenate([ones, pos], axis=-1).reshape(n, npair, 2 * FOX_AUG)
    k_full = jnp.concatenate([k.reshape(n, npair, LANES), k_aug, pad], axis=-1)
    q_aug = jnp.transpose(jnp.concatenate([q_aug, pad], axis=-1), (1, 2, 0))
    qt_full = jnp.concatenate([qt.reshape(npair, LANES, n), q_aug], axis=1)
    return qt_full.reshape(npair * 2 * LANES, n), k_full.reshape(n, npair * 2 * LANES)


def _fox_attention(qt, k, vt, b, t_len, tq, tk):
    npair = H_FOX // 2
    nq = t_len // tq
    return pl.pallas_call(
        functools.partial(_fox_kernel, tq=tq, tk=tk),
        name="fox_attention",
        grid=(b, npair, nq),
        in_specs=[pl.BlockSpec((2 * LANES, tq), lambda bi, p, i: (p, bi * nq + i)),
                  pl.BlockSpec((t_len, 2 * LANES), lambda bi, p, i: (bi, p)),
                  pl.BlockSpec((LANES, t_len), lambda bi, p, i: (p, bi))],
        out_specs=pl.BlockSpec((LANES, tq), lambda bi, p, i: (p, bi * nq + i)),
        out_shape=jax.ShapeDtypeStruct(vt.shape, BF16),
        compiler_params=_cparams(("parallel", "parallel", "arbitrary")),
    )(qt, k, vt)


def _sb_kernel(q_ref, k_ref, v_ref, o_ref, *, t):
    qi = pl.program_id(2)
    q2 = q_ref[...].astype(F32)
    lo, hi = _pair_masks(t)
    row = lax.broadcasted_iota(jnp.int32, (t, t), 0)
    col = lax.broadcasted_iota(jnp.int32, (t, t), 1)
    strict = col < row
    after = (row > col).astype(BF16)

    qa, qb = (q2 * lo).astype(BF16), (q2 * hi).astype(BF16)

    def one(qh, kc, vc, run, acc, diag):
        z = _nt_dot(qh, kc)
        tl = _softplus_tail(z)
        log_beta = jnp.minimum(z, 0.0) - tl
        log_1mb = -jnp.maximum(z, 0.0) - tl
        if diag:
            log_1mb = jnp.where(strict, log_1mb, 0.0)
        h1, h2, h3 = _split3(log_1mb)
        tail = (jnp.dot(h1, after, preferred_element_type=F32)
                + jnp.dot(h2, after, preferred_element_type=F32)
                + jnp.dot(h3, after, preferred_element_type=F32))
        a = jnp.exp(log_beta + (tail + run))
        if diag:
            a = jnp.where(strict, a, 0.0)
        acc = acc + jnp.dot(a.astype(BF16), vc, preferred_element_type=F32)
        run = run + (tail[:, 0:1] + log_1mb[:, 0:1])
        return run, acc

    def chunk(j, carry, diag):
        (ra, acca), (rb, accb) = carry
        k0 = pl.multiple_of(j * t, t)
        kc = k_ref[pl.ds(k0, t), :]
        vc = v_ref[pl.ds(k0, t), :]
        return one(qa, kc, vc, ra, acca, diag), one(qb, kc, vc, rb, accb, diag)

    zero = (jnp.zeros((t, 1), F32), jnp.zeros((t, LANES), F32))
    carry = chunk(qi, (zero, zero), True)

    def live(state):
        j, ((ra, _), (rb, _)) = state
        return jnp.logical_and(j >= 0, jnp.max(jnp.maximum(ra, rb)) > SB_DEAD)

    def older(state):
        j, c = state
        return j - 1, chunk(j, c, False)

    _, ((_, acca), (_, accb)) = lax.while_loop(live, older, (qi - 1, carry))
    o_ref[...] = (acca * lo + accb * hi).astype(BF16)


def _sb_attention(q, k, v, b, t_len, t):
    npair = H_SB // 2
    nq = t_len // t
    return pl.pallas_call(
        functools.partial(_sb_kernel, t=t),
        name="stick_breaking_attention",
        grid=(b, npair, nq),
        in_specs=[pl.BlockSpec((t, LANES), lambda bi, p, i: (bi * nq + i, p)),
                  pl.BlockSpec((t_len, LANES), lambda bi, p, i: (bi, p)),
                  pl.BlockSpec((t_len, LANES), lambda bi, p, i: (bi, p))],
        out_specs=pl.BlockSpec((t, LANES), lambda bi, p, i: (bi * nq + i, p)),
        out_shape=jax.ShapeDtypeStruct(q.shape, BF16),
        compiler_params=_cparams(("parallel", "parallel", "arbitrary")),
    )(q, k, v)


def _sort_key(x):
    x = jnp.where(x == 0.0, 0.0, x)
    bits = pltpu.bitcast(x, jnp.int32)
    return bits ^ ((bits >> 31) & 0x7FFFFFFF)


def _select_topk_bias(key_ref, nch, tk, nq, topk, keys_on_lanes):
    key_axis = 1 if keys_on_lanes else 0
    vec = (nq, 1) if keys_on_lanes else (1, nq)

    def load(k0):
        return key_ref[:, pl.ds(k0, tk)] if keys_on_lanes else key_ref[pl.ds(k0, tk), :]

    def store(k0, val):
        if keys_on_lanes:
            key_ref[:, pl.ds(k0, tk)] = val
        else:
            key_ref[pl.ds(k0, tk), :] = val

    fold = _fold_lanes if keys_on_lanes else _fold_rows

    def count(pred):
        def body(c, acc):
            k0 = pl.multiple_of(c * tk, tk)
            return acc + fold(jnp.where(pred(load(k0), k0), 1.0, 0.0))
        part = (nq, LANES) if keys_on_lanes else (8, nq)
        acc = lax.fori_loop(0, nch, body, jnp.zeros(part, F32))
        return jnp.sum(acc, axis=key_axis, keepdims=True)

    kf = float(topk)

    def value_bit(i, ans):
        cand = ans | jnp.left_shift(jnp.int32(1), 31 - i)
        cand_s = cand ^ INT_MIN
        cnt = count(lambda blk, k0: blk >= cand_s)
        return jnp.where(cnt >= kf, cand, ans)

    ans = lax.fori_loop(0, 32, value_bit, jnp.zeros(vec, jnp.int32))
    thr = ans ^ INT_MIN
    need = kf - count(lambda blk, k0: blk > thr)

    r_i = lax.broadcasted_iota(jnp.int32, (tk, tk), 0)
    c_i = lax.broadcasted_iota(jnp.int32, (tk, tk), 1)
    tri = jnp.where(r_i <= c_i if keys_on_lanes else c_i <= r_i, 1.0, 0.0).astype(BF16)

    def write(c, base):
        k0 = pl.multiple_of(c * tk, tk)
        blk = load(k0)
        eq = blk == thr
        eqb = jnp.where(eq, 1.0, 0.0).astype(BF16)
        if keys_on_lanes:
            rank = jnp.dot(eqb, tri, preferred_element_type=F32)
            total = rank[:, tk - 1:tk]
        else:
            rank = jnp.dot(tri, eqb, preferred_element_type=F32)
            total = rank[tk - 1:tk, :]
        order = jnp.where(blk > thr, 0.0, jnp.where(eq, base + rank, jnp.inf))
        store(k0, pltpu.bitcast(jnp.where(order <= need, 0.0, NEG_INF), jnp.int32))
        return base + total

    lax.fori_loop(0, nch, write, jnp.zeros(vec, F32))


def _dsa_kernel(qt_ref, iqt_ref, misct_ref, ik_ref, k_ref, vt_ref, o_ref, key_ref, *, tq, tk, topk):
    qi = pl.program_id(1)
    q0 = qi * tq
    jd = q0 // tk
    nch = jd + 1
    lo, hi = _pair_masks_t(tq)
    key_in = lax.broadcasted_iota(jnp.int32, (tk, tq), 0)
    q_pos = q0 + lax.broadcasted_iota(jnp.int32, (tk, tq), 1)

    misct = misct_ref[...]
    iqt = iqt_ref[...].astype(F32)
    iq_heads = []
    for hh in range(IDX_HEADS):
        pair = iqt[(hh // 2) * LANES:(hh // 2 + 1) * LANES]
        iq_heads.append((pair * (lo if hh % 2 == 0 else hi)).astype(BF16))
    iws = [misct[MISC_IW + hh:MISC_IW + hh + 1] for hh in range(IDX_HEADS)]

    def score_chunk(c, diag):
        k0 = pl.multiple_of(c * tk, tk)
        ikc = ik_ref[pl.ds(k0, tk), :]
        isc = jnp.zeros((tk, tq), F32)
        for hh in range(IDX_HEADS):
            s = jnp.dot(ikc, iq_heads[hh], preferred_element_type=F32)
            isc = isc + iws[hh] * jnp.maximum(s, 0.0)
        if diag:
            isc = jnp.where(k0 + key_in <= q_pos, isc, NEG_INF)
        key_ref[pl.ds(k0, tk), :] = _sort_key(isc)

    def score_body(c, _):
        score_chunk(c, False)
        return 0

    lax.fori_loop(0, jd, score_body, 0)
    score_chunk(jd, True)

    _select_topk_bias(key_ref, nch, tk, tq, topk, keys_on_lanes=False)

    qs = []
    for p in range(H_DSA // 2):
        qt = qt_ref[p * LANES:(p + 1) * LANES, :].astype(F32)
        qs += [(qt * lo).astype(BF16), (qt * hi).astype(BF16)]

    def pair_cols(hh):
        return slice((hh // 2) * LANES, (hh // 2 + 1) * LANES)

    def scores(c):
        k0 = pl.multiple_of(c * tk, tk)
        bias = pltpu.bitcast(key_ref[pl.ds(k0, tk), :], F32)
        out = []
        for hh in range(H_DSA):
            st = jnp.dot(k_ref[pl.ds(k0, tk), pair_cols(hh)], qs[hh],
                         preferred_element_type=F32) + bias
            out.append((st, _max_rows(st)))
        return tuple(out)

    lo_b = (lax.broadcasted_iota(jnp.int32, (LANES, tk), 0) < HEAD_DIM).astype(F32).astype(BF16)
    hi_b = (1.0 - lo_b.astype(F32)).astype(BF16)

    def consume(s, c, carry):
        k0 = pl.multiple_of(c * tk, tk)
        out = []
        for p in range(H_DSA // 2):
            vts = _values_with_ones(vt_ref[p * LANES:(p + 1) * LANES, pl.ds(k0, tk)], lo_b, hi_b)
            for e in range(2):
                hh = 2 * p + e
                out.append(_flash_t_update(*s[hh], *carry[hh], vts[e], True))
        return tuple(out)

    def trip(c, state):
        s, carry = state
        s_next = scores(c + 1)
        return s_next, consume(s, c, carry)

    s, carry = lax.fori_loop(0, jd, trip,
                             (scores(0), tuple(_flash_t_init(tq) for _ in range(H_DSA))))
    causal = jd * tk + key_in <= q_pos
    s = tuple(jnp.where(causal, sh, NEG_INF) for sh, _ in s)
    carry = consume(tuple((sh, _max_rows(sh)) for sh in s), jd, carry)
    outs = [_flash_t_pair_out(carry[2 * p][1], carry[2 * p + 1][1]) for p in range(H_DSA // 2)]
    o_ref[...] = jnp.concatenate(outs, axis=0).astype(BF16)


def _dsa_attention(qt, iqt, misct, ikb, k, vt, b, t_len, tq, tk, topk):
    nq = t_len // tq
    colblk = lambda bi, i: (0, bi * nq + i)
    per_b = lambda bi, i: (bi, 0)
    return pl.pallas_call(
        functools.partial(_dsa_kernel, tq=tq, tk=tk, topk=topk),
        name="dsa_attention",
        grid=(b, nq),
        in_specs=[pl.BlockSpec((W_DSA, tq), colblk), pl.BlockSpec((W_IDX, tq), colblk),
                  pl.BlockSpec((LANES, tq), colblk), pl.BlockSpec((t_len, LANES), per_b),
                  pl.BlockSpec((t_len, W_DSA), per_b),
                  pl.BlockSpec((W_DSA, t_len), lambda bi, i: (0, bi))],
        out_specs=pl.BlockSpec((W_DSA, tq), colblk),
        out_shape=jax.ShapeDtypeStruct(qt.shape, BF16),
        scratch_shapes=[pltpu.VMEM((t_len, tq), jnp.int32)],
        compiler_params=_cparams(("parallel", "arbitrary")),
    )(qt, iqt, misct, ikb, k, vt)


def _mem_kernel(q_ref, mkv_ref, o_ref, *, tq):
    lo, hi = _pair_masks(tq)
    outs = []
    for p in range(H_MEM // 2):
        q2 = q_ref[:, p * LANES:(p + 1) * LANES].astype(F32)
        mk = mkv_ref[0, :, p * LANES:(p + 1) * LANES].astype(BF16)
        mv = mkv_ref[0, :, W_MEM + p * LANES:W_MEM + (p + 1) * LANES].astype(BF16)

        def head(qh):
            s = _nt_dot(qh, mk)
            pr = jnp.exp(s - jnp.max(s, axis=-1, keepdims=True))
            l = jnp.sum(pr, axis=-1, keepdims=True)
            return jnp.dot(pr.astype(BF16), mv, preferred_element_type=F32) / l

        outs.append(head((q2 * lo).astype(BF16)) * lo + head((q2 * hi).astype(BF16)) * hi)
    o_ref[...] = jnp.concatenate(outs, axis=1).astype(BF16)


def _mem_attention(q, mkv, b, t_len, tq):
    nq = t_len // tq
    n_mem = mkv.shape[1]
    return pl.pallas_call(
        functools.partial(_mem_kernel, tq=tq),
        grid=(b, nq),
        in_specs=[pl.BlockSpec((tq, W_MEM), lambda bi, i: (bi * nq + i, 0)),
                  pl.BlockSpec((1, n_mem, 2 * W_MEM), lambda bi, i: (bi, 0, 0))],
        out_specs=pl.BlockSpec((tq, W_MEM), lambda bi, i: (bi * nq + i, 0)),
        out_shape=jax.ShapeDtypeStruct(q.shape, BF16),
        compiler_params=_cparams(("parallel", "arbitrary")),
    )(q, mkv)


def _rms(x, g):
    return x * lax.rsqrt(jnp.mean(x * x, axis=-1, keepdims=True) + EPS) * g


def _merge_kernel(x_ref, of_ref, od_ref, os_ref, om_ref, gpre_ref, gpost_ref, wg_ref,
                  wf_ref, wd_ref, ws_ref, wm_ref, wo_ref, y_ref, *, transposed):
    x = x_ref[...]
    d = x.shape[1]
    hb = _rms(x, gpre_ref[...]).astype(BF16)
    merged = None
    for i, (o_ref, w_ref) in enumerate(((of_ref, wf_ref), (od_ref, wd_ref),
                                        (os_ref, ws_ref), (om_ref, wm_ref))):
        gate = jax.nn.sigmoid(_nt_dot(hb, wg_ref[i * d:(i + 1) * d, :]))
        o = o_ref[...]
        if transposed and i < 2:
            o = o.astype(F32).T
        br = jnp.dot(o.astype(BF16), w_ref[...], preferred_element_type=F32)
        merged = gate * br if merged is None else merged + gate * br
    y = jnp.dot(merged.astype(BF16), wo_ref[...], preferred_element_type=F32)
    y_ref[...] = x + _rms(y, gpost_ref[...])


def _merge(x, o_fox, o_dsa, o_sb, o_mem, gpre, gpost, wg, wf, wd, ws, wm, wo, tm, transposed):
    n, d = x.shape
    row = lambda i: (i, 0)
    const = lambda i: (0, 0)
    full = lambda a: pl.BlockSpec(a.shape, const)

    def o_spec(o, can_t):
        if transposed and can_t:
            return pl.BlockSpec((o.shape[0], tm), lambda i: (0, i))
        return pl.BlockSpec((tm, o.shape[1]), row)

    return pl.pallas_call(
        functools.partial(_merge_kernel, transposed=transposed),
        name="branch_merge",
        grid=(n // tm,),
        in_specs=[pl.BlockSpec((tm, d), row), o_spec(o_fox, True), o_spec(o_dsa, True),
                  o_spec(o_sb, False), o_spec(o_mem, False)]
        + [full(a) for a in (gpre, gpost, wg, wf, wd, ws, wm, wo)],
        out_specs=pl.BlockSpec((tm, d), row),
        out_shape=jax.ShapeDtypeStruct((n, d), F32),
        compiler_params=_cparams(("parallel",)),
    )(x, o_fox, o_dsa, o_sb, o_mem, gpre, gpost, wg, wf, wd, ws, wm, wo)


def _ffn_kernel(x_ref, gpre_ref, gpost_ref, wi_ref, wo_ref, y_ref, *, d_ff, tc):
    x = x_ref[...]
    hb = _rms(x, gpre_ref[...]).astype(BF16)
    y = jnp.zeros(x.shape, F32)
    for c0 in range(0, d_ff, tc):
        gate = jnp.dot(hb, wi_ref[:, c0:c0 + tc], preferred_element_type=F32)
        up = jnp.dot(hb, wi_ref[:, d_ff + c0:d_ff + c0 + tc], preferred_element_type=F32)
        act = (gate * jax.nn.sigmoid(gate)) * up
        y = y + jnp.dot(act.astype(BF16), wo_ref[c0:c0 + tc, :], preferred_element_type=F32)
    y_ref[...] = x + _rms(y, gpost_ref[...])


def _ffn(x, gpre, gpost, wi, wo, tm):
    n, d = x.shape
    d_ff = wo.shape[0]
    tc = 256 if d_ff % 256 == 0 else d_ff
    row = lambda i: (i, 0)
    const = lambda i: (0, 0)
    return pl.pallas_call(
        functools.partial(_ffn_kernel, d_ff=d_ff, tc=tc),
        name="swiglu_ffn",
        grid=(n // tm,),
        in_specs=[pl.BlockSpec((tm, d), row), pl.BlockSpec(gpre.shape, const),
                  pl.BlockSpec(gpost.shape, const), pl.BlockSpec(wi.shape, const),
                  pl.BlockSpec(wo.shape, const)],
        out_specs=pl.BlockSpec((tm, d), row),
        out_shape=jax.ShapeDtypeStruct((n, d), F32),
        compiler_params=_cparams(("parallel",)),
    )(x, gpre, gpost, wi, wo)


def _dec_score_kernel(pt_ref, iqh_ref, iwb_ref, iknew_ref, *rest, npages):
    pages = rest[:npages]
    o_ref = rest[npages]
    iqh = iqh_ref[0]
    iwb = iwb_ref[0]
    for j in range(npages):
        s = jnp.dot(iqh, pages[j][0, 0].astype(BF16), preferred_element_type=F32)
        o_ref[0, :, j * LANES:(j + 1) * LANES] = jnp.sum(iwb * jnp.maximum(s, 0.0), axis=0,
                                                         keepdims=True)
    ik_new = iknew_ref[0][:, :IDX_DIM].astype(BF16).astype(F32)
    s_new = jnp.sum(iqh.astype(F32) * ik_new, axis=-1, keepdims=True)
    isc_new = jnp.sum(iwb[:, 0:1] * jnp.maximum(s_new, 0.0), axis=0, keepdims=True)
    lane = lax.broadcasted_iota(jnp.int32, (1, LANES), 1)
    o_ref[0, :, npages * LANES:] = jnp.where(lane == 0, isc_new, NEG_INF)


def _dec_scores(pt_flat, iqh, iwb, ik_new, idxk_pool, layer, nb, npages):
    width = (npages + 1) * LANES
    seq = lambda bi, pt: (bi, 0, 0)

    def page_map(j):
        return lambda bi, pt: (layer, pt[bi * npages + j], 0, 0)

    grid_spec = pltpu.PrefetchScalarGridSpec(
        num_scalar_prefetch=1, grid=(nb,),
        in_specs=[pl.BlockSpec((1, 8, IDX_DIM), seq), pl.BlockSpec((1, 8, LANES), seq),
                  pl.BlockSpec((1, 1, LANES), seq)]
        + [pl.BlockSpec((1, 1, IDX_DIM, PAGE_SIZE), page_map(j)) for j in range(npages)],
        out_specs=pl.BlockSpec((1, 1, width), seq))
    return pl.pallas_call(
        functools.partial(_dec_score_kernel, npages=npages),
        name="decode_index_scores",
        grid_spec=grid_spec,
        out_shape=jax.ShapeDtypeStruct((nb, 1, width), F32),
        compiler_params=_cparams(("arbitrary",)),
    )(pt_flat, iqh, iwb, ik_new, *([idxk_pool] * npages))


def _dec_select_kernel(isc_ref, o_ref, key_ref, *, topk):
    rows, width = isc_ref.shape
    key_ref[...] = _sort_key(isc_ref[...])
    _select_topk_bias(key_ref, width // LANES, LANES, rows, topk, keys_on_lanes=True)
    o_ref[...] = pltpu.bitcast(key_ref[...], F32)


def _dec_select(isc, topk):
    return pl.pallas_call(
        functools.partial(_dec_select_kernel, topk=topk),
        out_shape=jax.ShapeDtypeStruct(isc.shape, F32),
        scratch_shapes=[pltpu.VMEM(isc.shape, jnp.int32)],
        compiler_params=pltpu.CompilerParams(vmem_limit_bytes=VMEM_LIMIT),
    )(isc)


def _head_rows(width):
    sub = lax.broadcasted_iota(jnp.int32, (8, width), 0)
    lane = lax.broadcasted_iota(jnp.int32, (8, width), 1)
    return ((lane >> 6) == sub).astype(F32)


def _dec_attn_kernel(pt_ref, fq_ref, dq_ref, sq_ref, mq_ref, fnew_ref, dnew_ref, lnew_ref,
                     bias_ref, mem_ref, *rest, npages):
    fox_pages = rest[0:npages]
    logf_pages = rest[npages:2 * npages]
    dsa_pages = rest[2 * npages:3 * npages]
    sb_pages = rest[3 * npages:4 * npages]
    of_ref, od_ref, os_ref, om_ref = rest[4 * npages:]

    row = lax.broadcasted_iota(jnp.int32, (PAGE_SIZE, PAGE_SIZE), 0)
    col = lax.broadcasted_iota(jnp.int32, (PAGE_SIZE, PAGE_SIZE), 1)
    after = (row > col).astype(BF16)

    past = npages * PAGE_SIZE

    def keys_t(pages):
        return jnp.concatenate([pg[0, 0, 0].astype(BF16) for pg in pages], axis=1)

    def values_t(pages):
        return jnp.concatenate([pg[0, 0, 1].astype(BF16) for pg in pages], axis=1)

    def page_rows(x):
        return jnp.concatenate([x[:, j * PAGE_SIZE:(j + 1) * PAGE_SIZE] for j in range(npages)],
                               axis=0)

    def suffix_sums(x_rows, run):
        h1, h2, h3 = _split3(x_rows)
        inner = (jnp.dot(h1, after, preferred_element_type=F32)
                 + jnp.dot(h2, after, preferred_element_type=F32)
                 + jnp.dot(h3, after, preferred_element_type=F32))
        tot = jnp.sum(x_rows, axis=-1, keepdims=True)
        outs = [None] * npages
        for j in range(npages - 1, -1, -1):
            outs[j] = inner[j * 8:(j + 1) * 8] + run
            run = run + tot[j * 8:(j + 1) * 8]
        return jnp.concatenate(outs, axis=1)

    def head_diag(o, mask):
        return jnp.sum(o * mask, axis=0, keepdims=True)

    def softmax_pv(s_past, s_new, vt_all, v_new):
        m = jnp.maximum(jnp.max(s_past, axis=-1, keepdims=True), s_new)
        p_new = jnp.exp(s_new - m)
        pr = jnp.exp(s_past - m)
        l = p_new + jnp.sum(pr, axis=-1, keepdims=True)
        acc = p_new.astype(BF16).astype(F32) * v_new + _nt_dot(pr.astype(BF16), vt_all)
        return acc / l

    mask_f = _head_rows(W_FOX)
    qf = (fq_ref[0].astype(F32) * mask_f).astype(BF16)
    k_new = fnew_ref[0][:, :W_FOX].astype(BF16).astype(F32)
    v_new = fnew_ref[0][:, W_FOX:].astype(BF16).astype(F32)
    s_new = jnp.sum(qf.astype(F32) * k_new, axis=-1, keepdims=True)
    lf_rows = jnp.concatenate([pg[0, 0] for pg in logf_pages], axis=0)
    decay = suffix_sums(lf_rows, lnew_ref[0][:, 0:1])
    s_past = jnp.dot(qf, keys_t(fox_pages), preferred_element_type=F32) + decay
    of_ref[0] = head_diag(softmax_pv(s_past, s_new, values_t(fox_pages), v_new), mask_f)

    mask_d = _head_rows(W_DSA)
    qd = (dq_ref[0].astype(F32) * mask_d).astype(BF16)
    k_new = dnew_ref[0][:, :W_DSA].astype(BF16).astype(F32)
    v_new = dnew_ref[0][:, W_DSA:].astype(BF16).astype(F32)
    s_new = (jnp.sum(qd.astype(F32) * k_new, axis=-1, keepdims=True)
             + bias_ref[0][:, past:past + 1])
    s_past = jnp.dot(qd, keys_t(dsa_pages), preferred_element_type=F32) + bias_ref[0][:, :past]
    od_ref[0] = head_diag(softmax_pv(s_past, s_new, values_t(dsa_pages), v_new), mask_d)

    mask_s = _head_rows(W_SB)
    qs = (sq_ref[0].astype(F32) * mask_s).astype(BF16)
    z = jnp.dot(qs, keys_t(sb_pages), preferred_element_type=F32)
    tl = _softplus_tail(z)
    log_beta = jnp.minimum(z, 0.0) - tl
    log_1mb = -jnp.maximum(z, 0.0) - tl
    tail = suffix_sums(page_rows(log_1mb), jnp.zeros((8, 1), F32))
    a = jnp.exp(log_beta + tail)
    os_ref[0] = head_diag(_nt_dot(a.astype(BF16), values_t(sb_pages)), mask_s)

    mask_m = _head_rows(W_MEM)
    qm = (mq_ref[0].astype(F32) * mask_m).astype(BF16)
    s = jnp.dot(qm, mem_ref[0, 0, 0].astype(BF16), preferred_element_type=F32)
    pr = jnp.exp(s - jnp.max(s, axis=-1, keepdims=True))
    l = jnp.sum(pr, axis=-1, keepdims=True)
    o = _nt_dot(pr.astype(BF16), mem_ref[0, 0, 1].astype(BF16))
    om_ref[0] = head_diag(o / l, mask_m)


def _dec_attention(pt_flat, fq, dq, sq, mq, fnew, dnew, lnew, bias, mem, fox_pool, logf_pool,
                   dsa_pool, sb_pool, layer, nb, npages):
    seq = lambda bi, pt: (bi, 0, 0)

    def page_map(j, nd):
        return lambda bi, pt: (layer, pt[bi * npages + j]) + (0,) * nd

    def pages(pool):
        blk = (1, 1) + pool.shape[2:]
        return [pl.BlockSpec(blk, page_map(j, len(blk) - 2)) for j in range(npages)]

    def seq_spec(a):
        return pl.BlockSpec((1,) + a.shape[1:], seq)

    mem_spec = pl.BlockSpec((1, 1) + mem.shape[2:], lambda bi, pt: (layer, bi, 0, 0, 0))
    grid_spec = pltpu.PrefetchScalarGridSpec(
        num_scalar_prefetch=1, grid=(nb,),
        in_specs=[seq_spec(a) for a in (fq, dq, sq, mq, fnew, dnew, lnew, bias)] + [mem_spec]
        + pages(fox_pool) + pages(logf_pool) + pages(dsa_pool) + pages(sb_pool),
        out_specs=[pl.BlockSpec((1, 1, wd), seq) for wd in (W_FOX, W_DSA, W_SB, W_MEM)])
    return pl.pallas_call(
        functools.partial(_dec_attn_kernel, npages=npages),
        name="decode_attention",
        grid_spec=grid_spec,
        out_shape=[jax.ShapeDtypeStruct((nb, 1, wd), F32) for wd in (W_FOX, W_DSA, W_SB, W_MEM)],
        compiler_params=_cparams(("arbitrary",)),
    )(pt_flat, fq, dq, sq, mq, fnew, dnew, lnew, bias, mem,
      *([fox_pool] * npages), *([logf_pool] * npages), *([dsa_pool] * npages),
      *([sb_pool] * npages))


def _rope_tables(pos):
    rd = HEAD_DIM // 4
    half = rd // 2
    inv_freq = ROPE_THETA ** (-jnp.arange(half, dtype=F32) * 2.0 / rd)
    ang = pos.astype(F32)[:, None] * inv_freq[None, :]
    cos, sin = jnp.cos(ang), jnp.sin(ang)
    n = pos.shape[0]
    one = jnp.ones((n, HEAD_DIM - rd), F32)
    zero = jnp.zeros((n, HEAD_DIM - rd), F32)
    zh = jnp.zeros((n, half), F32)
    cos64 = jnp.concatenate([cos, cos, one], axis=1)
    sa64 = jnp.concatenate([-sin, zh, zero], axis=1)
    sb64 = jnp.concatenate([zh, sin, zero], axis=1)
    dup = lambda a: jnp.concatenate([a, a], axis=1)
    return dup(cos64), dup(sa64), dup(sb64), cos.T, sin.T


def _prep_w_in(w_in_l, b_forget_l, d_model):
    offs = np.cumsum([0, W_FOX, W_FOX, W_FOX, H_FOX, W_DSA, W_DSA, W_DSA, W_IDX, IDX_DIM,
                      IDX_HEADS, W_SB, W_SB, W_SB, W_MEM])
    (o_fq, o_fk, o_fv, o_ff, o_dq, o_dk, o_dv, o_iq, o_ik, o_iw, o_sq, o_sk, o_sv, o_mq,
     o_g) = [int(v) for v in offs]
    wt = jnp.transpose(w_in_l)
    sl = lambda o, n: wt[o:o + n]
    ik = sl(o_ik, IDX_DIM)
    misc = jnp.concatenate([sl(o_ff, H_FOX), sl(o_iw, IDX_HEADS),
                            jnp.zeros((LANES - H_FOX - IDX_HEADS, d_model), F32)], axis=0)
    wp = jnp.concatenate([sl(o_fq, 3 * W_FOX), sl(o_dq, 3 * W_DSA), sl(o_iq, W_IDX),
                          sl(o_sq, 3 * W_SB), sl(o_mq, W_MEM), ik, ik, misc], axis=0)
    wg = wt[o_g:]
    bfp = jnp.concatenate([b_forget_l, jnp.zeros((LANES - H_FOX,), F32)])[None, :]
    return wp.astype(BF16), wg.astype(BF16), bfp


def kernel(x_prompt, x_sample, cache_fox_kv, cache_fox_logf, cache_dsa_kv, cache_dsa_idxk,
           cache_sb_kv, cache_mem_kv, page_table, mem_prompt, w_in, b_forget, w_mem_kv,
           w_br_fox, w_br_dsa, w_br_sb, w_br_mem, w_out, w_ffn_in, w_ffn_out,
           g_mix_pre, g_mix_post, g_ffn_pre, g_ffn_post):
    bp, t_len, d = x_prompt.shape
    nb = x_sample.shape[0]
    depth = w_in.shape[0]
    npages = page_table.shape[1]
    past_len = npages * PAGE_SIZE
    n_pool = cache_fox_kv.shape[1]
    n_mem = mem_prompt.shape[1]
    n_p = bp * t_len

    tm = min(256, t_len)
    t_att = min(256, t_len)
    tq_dsa = min(256, t_len)
    tk_dsa = min(512, t_len)
    topk_p = min(DSA_TOPK_MAX, t_len // 4)
    topk_s = min(DSA_TOPK_MAX, (past_len + 1) // 4)
    assert tk_dsa >= topk_p and t_len % tk_dsa == 0 and t_len % tm == 0

    tabs_p = _rope_tables(jnp.arange(t_len, dtype=jnp.int32))
    tabs_s = _rope_tables(jnp.full((nb,), past_len, jnp.int32))
    pt_flat = page_table.reshape(-1).astype(jnp.int32)

    def kv_view(cache, width):
        view = jnp.transpose(cache, (0, 1, 3, 4, 5, 2))
        return view.reshape(cache.shape[:2] + (2, width, cache.shape[2]))

    fox_t = kv_view(cache_fox_kv, W_FOX)
    dsa_t = kv_view(cache_dsa_kv, W_DSA)
    sb_t = kv_view(cache_sb_kv, W_SB)
    mem_t = kv_view(cache_mem_kv, W_MEM)
    idxk_t = jnp.transpose(cache_dsa_idxk, (0, 1, 3, 2))
    logf_t = jnp.pad(jnp.transpose(cache_fox_logf, (0, 1, 3, 2)),
                     ((0, 0), (0, 0), (0, 8 - H_FOX), (0, 0)))

    xp = x_prompt.reshape(n_p, d)
    xs = x_sample.reshape(nb, d)
    mem_flat = mem_prompt.reshape(bp * n_mem, d)

    rows_p, rows_s, mem_p = [], [], []
    for l in range(depth):
        wp, wg, bfp = _prep_w_in(w_in[l], b_forget[l], d)
        gpre, gpost = g_mix_pre[l][None, :], g_mix_post[l][None, :]
        fpre, fpost = g_ffn_pre[l][None, :], g_ffn_post[l][None, :]
        wf, wd_, ws, wm = (w.astype(BF16) for w in (w_br_fox[l], w_br_dsa[l], w_br_sb[l], w_br_mem[l]))
        wo = w_out[l].astype(BF16)
        wi, wfo = w_ffn_in[l].astype(BF16), w_ffn_out[l].astype(BF16)

        mem_kv = _matmul(mem_flat, w_mem_kv[l].astype(BF16))
        (fqt, fkv, fk, fvt, dqt, dkv, dk, dvt, iqt, sq, skv, sk, sv, mq, ik32, ikb,
         misc, misct) = _proj(xp, gpre, wp, tabs_p, bfp, tm, True)
        logf = misc[:, :H_FOX].reshape(bp, t_len, H_FOX)
        c = jnp.cumsum(logf, axis=1) * LOG2_E
        fqt_full, fk_full = _fox_operands(fqt, fk, c.reshape(n_p, H_FOX))
        o_fox = _fox_attention(fqt_full, fk_full, fvt, bp, t_len, t_att, tk_dsa)
        o_dsa = _dsa_attention(dqt, iqt, misct, ikb, dk, dvt, bp, t_len, tq_dsa, tk_dsa, topk_p)
        o_sb = _sb_attention(sq, sk, sv, bp, t_len, t_att)
        o_mem = _mem_attention(mq, mem_kv.reshape(bp, n_mem, 2 * W_MEM), bp, t_len, tm)
        xp = _merge(xp, o_fox, o_dsa, o_sb, o_mem, gpre, gpost, wg, wf, wd_, ws, wm, wo, tm, True)
        xp = _ffn(xp, fpre, fpost, wi, wfo, tm)
        rows_p.append((fkv.reshape(bp, t_len, 2, H_FOX, HEAD_DIM), logf,
                       dkv.reshape(bp, t_len, 2, H_DSA, HEAD_DIM),
                       ik32[:, :IDX_DIM].reshape(bp, t_len, IDX_DIM),
                       skv.reshape(bp, t_len, 2, H_SB, HEAD_DIM)))
        mem_p.append(mem_kv.reshape(bp, n_mem, 2, H_MEM, HEAD_DIM))

        (fq, fkv, fk, fv, dq, dkv, dk, dv, iq, sq, skv, sk, sv, mq, ik32, ikb,
         misc) = _proj(xs, gpre, wp, tabs_s, bfp, nb, False)
        r3 = lambda a: a.reshape(nb, 1, a.shape[1])
        iqh = jnp.pad(iq.reshape(nb, IDX_HEADS, IDX_DIM), ((0, 0), (0, 8 - IDX_HEADS), (0, 0)))
        iwb = jnp.pad(misc[:, MISC_IW:MISC_IW + IDX_HEADS], ((0, 0), (0, 8 - IDX_HEADS)))
        iwb = jnp.broadcast_to(iwb[:, :, None], (nb, 8, LANES))
        isc = _dec_scores(pt_flat, iqh, iwb, r3(ik32), idxk_t, l, nb, npages)
        bias = _dec_select(isc.reshape(nb, -1), topk_s).reshape(nb, 1, -1)
        lnew = jnp.pad(misc[:, :H_FOX], ((0, 0), (0, 8 - H_FOX)))
        lnew = jnp.broadcast_to(lnew[:, :, None], (nb, 8, LANES))
        o_fox, o_dsa, o_sb, o_mem = _dec_attention(
            pt_flat, r3(fq), r3(dq), r3(sq), r3(mq), r3(fkv), r3(dkv), lnew, bias,
            mem_t, fox_t, logf_t, dsa_t, sb_t, l, nb, npages)
        sq2 = lambda a: a.reshape(nb, a.shape[2])
        xs = _merge(xs, sq2(o_fox), sq2(o_dsa), sq2(o_sb), sq2(o_mem), gpre, gpost, wg, wf, wd_,
                    ws, wm, wo, nb, False)
        xs = _ffn(xs, fpre, fpost, wi, wfo, nb)
        rows_s.append((fkv.reshape(nb, 1, 2, H_FOX, HEAD_DIM), misc[:, :H_FOX].reshape(nb, 1, H_FOX),
                       dkv.reshape(nb, 1, 2, H_DSA, HEAD_DIM),
                       ik32[:, :IDX_DIM].reshape(nb, 1, IDX_DIM),
                       skv.reshape(nb, 1, 2, H_SB, HEAD_DIM)))

    stk = lambda rows, i: jnp.stack([r[i] for r in rows], axis=0)
    return (xp.reshape(bp, t_len, d), xs.reshape(nb, 1, d),
            stk(rows_p, 0), stk(rows_p, 1), stk(rows_p, 2), stk(rows_p, 3), stk(rows_p, 4),
            jnp.stack(mem_p, axis=0),
            stk(rows_s, 0), stk(rows_s, 1), stk(rows_s, 2), stk(rows_s, 3), stk(rows_s, 4))
```

```python
import functools

import jax
import jax.numpy as jnp
import numpy as np
from jax import lax
from jax.experimental import pallas as pl
from jax.experimental.pallas import tpu as pltpu

HEAD_DIM = 64
H_FOX = 6
H_DSA = 6
H_SB = 4
H_MEM = 4
IDX_HEADS = 4
IDX_DIM = 64
DSA_TOPK_MAX = 256
ROPE_THETA = 500000.0
N_BRANCH = 4
EPS = 1e-6
PAGE_SIZE = 128

LANES = 128
VMEM_LIMIT = 56 * 1024 * 1024

F32 = jnp.float32
BF16 = jnp.bfloat16
NEG_INF = float("-inf")
INT_MIN = -2 ** 31
SB_DEAD = -120.0
LOG2_E = 1.4426950408889634

W_FOX = H_FOX * HEAD_DIM
W_DSA = H_DSA * HEAD_DIM
W_SB = H_SB * HEAD_DIM
W_MEM = H_MEM * HEAD_DIM
W_IDX = IDX_HEADS * IDX_DIM

C_FQ, C_FKV = 0, W_FOX
C_DQ = C_FKV + 2 * W_FOX
C_DK = C_DQ + W_DSA
C_DV = C_DK + W_DSA
C_IQ = C_DV + W_DSA
C_SQ = C_IQ + W_IDX
C_SKV = C_SQ + W_SB
C_MQ = C_SKV + 2 * W_SB
C_IK = C_MQ + W_MEM
C_MISC = C_IK + LANES
N_PROJ = C_MISC + LANES
MISC_IW = H_FOX


def _cparams(sem):
    return pltpu.CompilerParams(dimension_semantics=sem, vmem_limit_bytes=VMEM_LIMIT)


def _nt_dot(a, b):
    return lax.dot_general(a, b, (((1,), (1,)), ((), ())), preferred_element_type=F32)


def _softplus_tail(z):
    return jnp.log1p(jnp.exp(-jnp.abs(z)))


def _split3(x):
    hi = x.astype(BF16)
    r1 = x - hi.astype(F32)
    mid = r1.astype(BF16)
    lo = (r1 - mid.astype(F32)).astype(BF16)
    return hi, mid, lo


def _proj_kernel(x_ref, g_ref, w_ref, cos_ref, sa_ref, sb_ref, bf_ref, cost_ref, sint_ref,
                 *refs, transposed, n_alias):
    (fq_ref, fkv_ref, fk_ref, fv_ref, dq_ref, dkv_ref, dk_ref, dv_ref, iq_ref, sq_ref, skv_ref,
     sk_ref, sv_ref, mq_ref, ik32_ref, ikb_ref, misc_ref, *extra) = refs[n_alias:]
    x = x_ref[...]
    h = x * lax.rsqrt(jnp.mean(x * x, axis=-1, keepdims=True) + EPS)
    hb = (h * g_ref[...]).astype(BF16)
    cosf, sa, sb = cos_ref[...], sa_ref[...], sb_ref[...]
    scale = HEAD_DIM ** -0.5

    def mm(c0, n):
        return _nt_dot(hb, w_ref[c0:c0 + n, :])

    def mm_t(c0, n):
        return _nt_dot(w_ref[c0:c0 + n, :], hb)

    def rope(z):
        outs = []
        for j in range(z.shape[1] // LANES):
            zj = z[:, j * LANES:(j + 1) * LANES]
            outs.append(zj * cosf + pltpu.roll(zj, LANES - 8, 1) * sa + pltpu.roll(zj, 8, 1) * sb)
        return outs[0] if len(outs) == 1 else jnp.concatenate(outs, axis=1)

    def rope_t(zt):
        cos_t, sin_t = cost_ref[...], sint_ref[...]
        half = HEAD_DIM // 8
        parts = []
        for hh in range(zt.shape[0] // HEAD_DIM):
            base = hh * HEAD_DIM
            x1, x2 = zt[base:base + half], zt[base + half:base + 2 * half]
            parts += [x1 * cos_t - x2 * sin_t, x2 * cos_t + x1 * sin_t,
                      zt[base + 2 * half:base + HEAD_DIM]]
        return jnp.concatenate(parts, axis=0)

    dk = rope(mm(C_DK, W_DSA))
    dk_ref[...] = dk.astype(BF16)
    skv = mm(C_SKV, 2 * W_SB)
    ik = rope(mm(C_IK, LANES))
    if transposed:
        scale2 = scale * LOG2_E
        fq_ref[...] = (mm_t(C_FQ, W_FOX) * scale2).astype(BF16)
        fkv_t = mm_t(C_FKV, 2 * W_FOX)
        fkv_ref[0, 0] = fkv_t
        fv_ref[...] = fkv_t[W_FOX:].astype(BF16)
        fk_ref[...] = mm(C_FKV, W_FOX).astype(BF16)
        dq_ref[...] = (rope_t(mm_t(C_DQ, W_DSA)) * scale2).astype(BF16)
        dv_t = mm_t(C_DV, W_DSA)
        dkv_ref[0, 0, :W_DSA] = rope_t(mm_t(C_DK, W_DSA))
        dkv_ref[0, 0, W_DSA:] = dv_t
        dv_ref[...] = dv_t.astype(BF16)
        iq_ref[...] = rope_t(mm_t(C_IQ, W_IDX)).astype(BF16)
        skv_ref[0, 0] = mm_t(C_SKV, 2 * W_SB)
        ik32_ref[0, 0] = rope_t(mm_t(C_IK, LANES))[:IDX_DIM]
        extra[0][...] = mm_t(C_MISC, LANES)
    else:
        fkv = mm(C_FKV, 2 * W_FOX)
        fkv_ref[...] = fkv
        fk_ref[...] = fkv[:, :W_FOX].astype(BF16)
        fq_ref[...] = (mm(C_FQ, W_FOX) * scale).astype(BF16)
        fv_ref[...] = fkv[:, W_FOX:].astype(BF16)
        dq_ref[...] = (rope(mm(C_DQ, W_DSA)) * scale).astype(BF16)
        dv = mm(C_DV, W_DSA)
        dkv_ref[:, :W_DSA] = dk
        dkv_ref[:, W_DSA:] = dv
        dv_ref[...] = dv.astype(BF16)
        iq_ref[...] = rope(mm(C_IQ, W_IDX)).astype(BF16)
        skv_ref[...] = skv
        ik32_ref[...] = ik

    sq_ref[...] = (mm(C_SQ, W_SB) * scale).astype(BF16)
    sk_ref[...] = skv[:, :W_SB].astype(BF16)
    sv_ref[...] = skv[:, W_SB:].astype(BF16)
    mq_ref[...] = (mm(C_MQ, W_MEM) * scale).astype(BF16)
    ikb_ref[...] = ik.astype(BF16)

    zm = mm(C_MISC, LANES)
    ff = zm + bf_ref[...]
    logf = -(jnp.maximum(-ff, 0.0) + _softplus_tail(ff))
    lane = lax.broadcasted_iota(jnp.int32, zm.shape, 1)
    misc_ref[...] = jnp.where(lane < H_FOX, logf, zm)


def _proj(x, g, w, tabs, bfp, tm, transposed, stacked=None, layer=0, depth=1):
    n, d = x.shape
    cosf, sa, sb, cos_t, sin_t = tabs
    nt = cosf.shape[0] // tm
    row = lambda i: (i, 0)
    col = lambda i: (0, i)
    tab = lambda i: (i % nt, 0)
    tab_t = lambda i: (0, i % nt)
    const = lambda i: (0, 0)
    stack_blk = lambda i: (layer, i // nt, 0, i % nt)
    widths = [(W_FOX, BF16, 't'), (2 * W_FOX, F32, 's'), (W_FOX, BF16, ''), (W_FOX, BF16, 't'),
              (W_DSA, BF16, 't'), (2 * W_DSA, F32, 's'), (W_DSA, BF16, ''), (W_DSA, BF16, 't'),
              (W_IDX, BF16, 't'), (W_SB, BF16, ''), (2 * W_SB, F32, 's'), (W_SB, BF16, ''),
              (W_SB, BF16, ''), (W_MEM, BF16, ''), (IDX_DIM if transposed else LANES, F32, 's'),
              (LANES, BF16, ''), (LANES, F32, '')]
    if transposed:
        widths.append((LANES, F32, 't'))
    out_specs, out_shape, stacked_idx = [], [], []
    for idx, (wd, dt, kind) in enumerate(widths):
        if transposed and kind == 't':
            out_specs.append(pl.BlockSpec((wd, tm), col))
            out_shape.append(jax.ShapeDtypeStruct((wd, n), dt))
        elif transposed and kind == 's':
            stacked_idx.append(idx)
            out_specs.append(pl.BlockSpec((1, 1, wd, tm), stack_blk))
            out_shape.append(jax.ShapeDtypeStruct((depth, n // (nt * tm), wd, nt * tm), dt))
        else:
            out_specs.append(pl.BlockSpec((tm, wd), row))
            out_shape.append(jax.ShapeDtypeStruct((n, wd), dt))
    in_specs = [pl.BlockSpec((tm, d), row), pl.BlockSpec((1, d), const),
                pl.BlockSpec((N_PROJ, d), const),
                pl.BlockSpec((tm, LANES), tab), pl.BlockSpec((tm, LANES), tab),
                pl.BlockSpec((tm, LANES), tab), pl.BlockSpec((1, LANES), const),
                pl.BlockSpec((8, tm), tab_t), pl.BlockSpec((8, tm), tab_t)]
    operands = [x, g, w, cosf, sa, sb, bfp, cos_t, sin_t]
    aliases = {}
    if stacked is not None:
        for k, buf in enumerate(stacked):
            aliases[len(operands)] = stacked_idx[k]
            in_specs.append(pl.BlockSpec(memory_space=pl.ANY))
            operands.append(buf)
    n_alias = len(aliases)
    return pl.pallas_call(
        functools.partial(_proj_kernel, transposed=transposed, n_alias=n_alias),
        name="input_projection",
        grid=(n // tm,),
        in_specs=in_specs,
        out_specs=out_specs,
        out_shape=out_shape,
        input_output_aliases=aliases,
        compiler_params=_cparams(("parallel",)),
    )(*operands)


def _matmul_kernel(a_ref, b_ref, o_ref):
    o_ref[...] = jnp.dot(a_ref[...].astype(BF16), b_ref[...], preferred_element_type=F32)


def _matmul(a, b):
    return pl.pallas_call(
        _matmul_kernel,
        out_shape=jax.ShapeDtypeStruct((a.shape[0], b.shape[1]), F32),
        compiler_params=pltpu.CompilerParams(vmem_limit_bytes=VMEM_LIMIT),
    )(a, b)


def _pair_masks(rows):
    lane = lax.broadcasted_iota(jnp.int32, (rows, LANES), 1)
    lo = (lane < HEAD_DIM).astype(F32)
    return lo, 1.0 - lo


def _fold_lanes(x):
    part = x[:, 0:LANES]
    for u in range(1, x.shape[1] // LANES):
        part = part + x[:, u * LANES:(u + 1) * LANES]
    return part


def _pair_masks_t(cols):
    sub = lax.broadcasted_iota(jnp.int32, (LANES, cols), 0)
    lo = (sub < HEAD_DIM).astype(F32)
    return lo, 1.0 - lo


def _tree_rows(x, op, nacc=4):
    parts = [x[i * 8:(i + 1) * 8] for i in range(x.shape[0] // 8)]
    accs = parts[:nacc]
    for i, part in enumerate(parts[nacc:]):
        accs[i % nacc] = op(accs[i % nacc], part)
    while len(accs) > 1:
        accs = [op(a, b) for a, b in zip(accs[0::2], accs[1::2])] + (accs[-1:] if len(accs) % 2 else [])
    return accs[0]


def _fold_rows(x):
    return _tree_rows(x, jnp.add)


def _max_rows(x):
    return _tree_rows(x, jnp.maximum)


def _flash_t_update(st, pmax, m, acc, vt1, guard):
    m_new = jnp.maximum(m, jnp.max(pmax, axis=0, keepdims=True))
    m_use = jnp.where(m_new == NEG_INF, 0.0, m_new) if guard else m_new
    alpha = jnp.exp2(m - m_use)
    p = jnp.exp2(st - m_use)
    acc = alpha * acc + jnp.dot(vt1, p.astype(BF16), preferred_element_type=F32)
    return m_new, acc


def _flash_t_init(cols):
    return jnp.full((1, cols), NEG_INF, F32), jnp.zeros((LANES, cols), F32)


def _flash_t_pair_out(acca, accb):
    return jnp.concatenate([acca[:HEAD_DIM] / acca[HEAD_DIM:HEAD_DIM + 1],
                            accb[HEAD_DIM:] / accb[0:1]], axis=0)


def _values_with_ones(vt, lo_b, hi_b):
    return vt * lo_b + hi_b, vt * hi_b + lo_b


FOX_AUG = 6


def _fox_kernel(qt_ref, k_ref, vt_ref, o_ref, *, tq, tk):
    qi = pl.program_id(2)
    q0 = pl.multiple_of(qi * tq, tq)
    jd = q0 // tk
    qt = qt_ref[...].astype(F32)
    sub = lax.broadcasted_iota(jnp.int32, (2 * LANES, tq), 0)
    aug = sub - LANES
    lo = jnp.logical_or(sub < HEAD_DIM, jnp.logical_and(aug >= 0, aug < FOX_AUG))
    hi = jnp.logical_or(jnp.logical_and(sub >= HEAD_DIM, sub < LANES),
                        jnp.logical_and(aug >= FOX_AUG, aug < 2 * FOX_AUG))
    qta = jnp.where(lo, qt, 0.0).astype(BF16)
    qtb = jnp.where(hi, qt, 0.0).astype(BF16)
    key_in = lax.broadcasted_iota(jnp.int32, (tk, tq), 0)
    q_pos = q0 + lax.broadcasted_iota(jnp.int32, (tk, tq), 1)

    def scores(j):
        k0 = pl.multiple_of(j * tk, tk)
        kc = k_ref[pl.ds(k0, tk), :]
        sa = jnp.dot(kc, qta, preferred_element_type=F32)
        sb = jnp.dot(kc, qtb, preferred_element_type=F32)
        return (sa, _max_rows(sa)), (sb, _max_rows(sb))

    lo_b = (lax.broadcasted_iota(jnp.int32, (LANES, tk), 0) < HEAD_DIM).astype(F32).astype(BF16)
    hi_b = (1.0 - lo_b.astype(F32)).astype(BF16)

    def consume(s, j, carry):
        ca, cb = carry
        vta, vtb = _values_with_ones(vt_ref[:, pl.ds(pl.multiple_of(j * tk, tk), tk)], lo_b, hi_b)
        return _flash_t_update(*s[0], *ca, vta, False), _flash_t_update(*s[1], *cb, vtb, False)

    def trip(j, state):
        s, carry = state
        s_next = scores(j + 1)
        return s_next, consume(s, j, carry)

    s, carry = lax.fori_loop(0, jd, trip, (scores(0), (_flash_t_init(tq), _flash_t_init(tq))))
    causal = jd * tk + key_in <= q_pos
    s = tuple(jnp.where(causal, sh, NEG_INF) for sh, _ in s)
    (_, acca), (_, accb) = consume(tuple((sh, _max_rows(sh)) for sh in s), jd, carry)
    o_ref[...] = _flash_t_pair_out(acca, accb).astype(BF16)


def _fox_operands(qt, k, c):
    n = k.shape[0]
    npair = H_FOX // 2

    def chop(v):
        bits = lax.bitcast_convert_type(v, jnp.uint32) & jnp.uint32(0xFFFF0000)
        return lax.bitcast_convert_type(bits, F32)

    def split(v):
        hi = chop(v)
        mid = chop(v - hi)
        lo = chop(v - hi - mid)
        return jnp.stack([hi, mid, lo], axis=-1).astype(BF16)

    pos = split(c)
    neg = split(-c)
    ones = jnp.ones((n, H_FOX, 3), BF16)
    pad = jnp.zeros((n, npair, LANES - 2 * FOX_AUG), BF16)
    k_aug = jnp.concatenate([neg, ones], axis=-1).reshape(n, npair, 2 * FOX_AUG)
    q_aug = jnp.concatenate([ones, pos], axis=-1).reshape(n, npair, 2 * FOX_AUG)
    k_full = jnp.concatenate([k.reshape(n, npair, LANES), k_aug, pad], axis=-1)
    q_aug = jnp.transpose(jnp.concatenate([q_aug, pad], axis=-1), (1, 2, 0))
    qt_full = jnp.concatenate([qt.reshape(npair, LANES, n), q_aug], axis=1)
    return qt_full.reshape(npair * 2 * LANES, n), k_full.reshape(n, npair * 2 * LANES)


def _fox_attention(qt, k, vt, b, t_len, tq, tk):
    npair = H_FOX // 2
    nq = t_len // tq
    return pl.pallas_call(
        functools.partial(_fox_kernel, tq=tq, tk=tk),
        name="fox_attention",
        grid=(b, npair, nq),
        in_specs=[pl.BlockSpec((2 * LANES, tq), lambda bi, p, i: (p, bi * nq + i)),
                  pl.BlockSpec((t_len, 2 * LANES), lambda bi, p, i: (bi, p)),
                  pl.BlockSpec((LANES, t_len), lambda bi, p, i: (p, bi))],
        out_specs=pl.BlockSpec((LANES, tq), lambda bi, p, i: (p, bi * nq + i)),
        out_shape=jax.ShapeDtypeStruct(vt.shape, BF16),
        compiler_params=_cparams(("parallel", "parallel", "arbitrary")),
    )(qt, k, vt)


def _sb_kernel(q_ref, k_ref, v_ref, o_ref, *, t):
    qi = pl.program_id(1)
    lo, hi = _pair_masks(t)
    row = lax.broadcasted_iota(jnp.int32, (t, t), 0)
    col = lax.broadcasted_iota(jnp.int32, (t, t), 1)
    strict = col < row
    after = (row > col).astype(BF16)

    qs = []
    for p in range(H_SB // 2):
        q2 = q_ref[:, p * LANES:(p + 1) * LANES].astype(F32)
        qs += [(q2 * lo).astype(BF16), (q2 * hi).astype(BF16)]

    def one(qh, kc, vc, run, acc, diag):
        z = _nt_dot(qh, kc)
        tl = _softplus_tail(z)
        log_beta = jnp.minimum(z, 0.0) - tl
        log_1mb = -jnp.maximum(z, 0.0) - tl
        if diag:
            log_1mb = jnp.where(strict, log_1mb, 0.0)
        h1, h2, h3 = _split3(log_1mb)
        tail = (jnp.dot(h1, after, preferred_element_type=F32)
                + jnp.dot(h2, after, preferred_element_type=F32)
                + jnp.dot(h3, after, preferred_element_type=F32))
        a = jnp.exp(log_beta + (tail + run))
        if diag:
            a = jnp.where(strict, a, 0.0)
        acc = acc + jnp.dot(a.astype(BF16), vc, preferred_element_type=F32)
        run = run + (tail[:, 0:1] + log_1mb[:, 0:1])
        return run, acc

    def chunk(j, carry, diag):
        k0 = pl.multiple_of(j * t, t)
        out = []
        for hh in range(H_SB):
            cols = slice((hh // 2) * LANES, (hh // 2 + 1) * LANES)
            out.append(one(qs[hh], k_ref[pl.ds(k0, t), cols], v_ref[pl.ds(k0, t), cols],
                           *carry[hh], diag))
        return tuple(out)

    zero = (jnp.zeros((t, 1), F32), jnp.zeros((t, LANES), F32))
    carry = chunk(qi, (zero,) * H_SB, True)

    def live(state):
        j, c = state
        top = c[0][0]
        for run, _ in c[1:]:
            top = jnp.maximum(top, run)
        return jnp.logical_and(j >= 0, jnp.max(top) > SB_DEAD)

    def older(state):
        j, c = state
        return j - 1, chunk(j, c, False)

    _, carry = lax.while_loop(live, older, (qi - 1, carry))
    outs = [carry[2 * p][1] * lo + carry[2 * p + 1][1] * hi for p in range(H_SB // 2)]
    o_ref[...] = jnp.concatenate(outs, axis=1).astype(BF16)


def _sb_attention(q, k, v, b, t_len, t):
    nq = t_len // t
    return pl.pallas_call(
        functools.partial(_sb_kernel, t=t),
        name="stick_breaking_attention",
        grid=(b, nq),
        in_specs=[pl.BlockSpec((t, W_SB), lambda bi, i: (bi * nq + i, 0)),
                  pl.BlockSpec((t_len, W_SB), lambda bi, i: (bi, 0)),
                  pl.BlockSpec((t_len, W_SB), lambda bi, i: (bi, 0))],
        out_specs=pl.BlockSpec((t, W_SB), lambda bi, i: (bi * nq + i, 0)),
        out_shape=jax.ShapeDtypeStruct(q.shape, BF16),
        compiler_params=_cparams(("parallel", "arbitrary")),
    )(q, k, v)


def _sort_key(x):
    x = jnp.where(x == 0.0, 0.0, x)
    bits = pltpu.bitcast(x, jnp.int32)
    return bits ^ ((bits >> 31) & 0x7FFFFFFF)


def _select_topk_bias(key_ref, nch, tk, nq, topk, keys_on_lanes):
    key_axis = 1 if keys_on_lanes else 0
    vec = (nq, 1) if keys_on_lanes else (1, nq)

    def load(k0):
        return key_ref[:, pl.ds(k0, tk)] if keys_on_lanes else key_ref[pl.ds(k0, tk), :]

    def store(k0, val):
        if keys_on_lanes:
            key_ref[:, pl.ds(k0, tk)] = val
        else:
            key_ref[pl.ds(k0, tk), :] = val

    fold = _fold_lanes if keys_on_lanes else _fold_rows

    def count(pred):
        def body(c, acc):
            k0 = pl.multiple_of(c * tk, tk)
            return acc + fold(jnp.where(pred(load(k0), k0), 1.0, 0.0))
        part = (nq, LANES) if keys_on_lanes else (8, nq)
        acc = lax.fori_loop(0, nch, body, jnp.zeros(part, F32))
        return jnp.sum(acc, axis=key_axis, keepdims=True)

    kf = float(topk)

    def value_bit(i, ans):
        cand = ans | jnp.left_shift(jnp.int32(1), 31 - i)
        cand_s = cand ^ INT_MIN
        cnt = count(lambda blk, k0: blk >= cand_s)
        return jnp.where(cnt >= kf, cand, ans)

    ans = lax.fori_loop(0, 32, value_bit, jnp.zeros(vec, jnp.int32))
    thr = ans ^ INT_MIN
    need = kf - count(lambda blk, k0: blk > thr)

    r_i = lax.broadcasted_iota(jnp.int32, (tk, tk), 0)
    c_i = lax.broadcasted_iota(jnp.int32, (tk, tk), 1)
    tri = jnp.where(r_i <= c_i if keys_on_lanes else c_i <= r_i, 1.0, 0.0).astype(BF16)

    def write(c, base):
        k0 = pl.multiple_of(c * tk, tk)
        blk = load(k0)
        eq = blk == thr
        eqb = jnp.where(eq, 1.0, 0.0).astype(BF16)
        if keys_on_lanes:
            rank = jnp.dot(eqb, tri, preferred_element_type=F32)
            total = rank[:, tk - 1:tk]
        else:
            rank = jnp.dot(tri, eqb, preferred_element_type=F32)
            total = rank[tk - 1:tk, :]
        order = jnp.where(blk > thr, 0.0, jnp.where(eq, base + rank, jnp.inf))
        store(k0, pltpu.bitcast(jnp.where(order <= need, 0.0, NEG_INF), jnp.int32))
        return base + total

    lax.fori_loop(0, nch, write, jnp.zeros(vec, F32))


def _dsa_kernel(qt_ref, iqt_ref, misct_ref, ik_ref, k_ref, vt_ref, o_ref, key_ref, *, tq, tk, topk):
    qi = pl.program_id(1)
    q0 = qi * tq
    jd = q0 // tk
    nch = jd + 1
    lo, hi = _pair_masks_t(tq)
    key_in = lax.broadcasted_iota(jnp.int32, (tk, tq), 0)
    q_pos = q0 + lax.broadcasted_iota(jnp.int32, (tk, tq), 1)

    misct = misct_ref[...]
    iqt = iqt_ref[...].astype(F32)
    iq_heads = []
    for hh in range(IDX_HEADS):
        pair = iqt[(hh // 2) * LANES:(hh // 2 + 1) * LANES]
        iq_heads.append((pair * (lo if hh % 2 == 0 else hi)).astype(BF16))
    iws = [misct[MISC_IW + hh:MISC_IW + hh + 1] for hh in range(IDX_HEADS)]

    def score_chunk(c, diag):
        k0 = pl.multiple_of(c * tk, tk)
        ikc = ik_ref[pl.ds(k0, tk), :]
        isc = jnp.zeros((tk, tq), F32)
        for hh in range(IDX_HEADS):
            s = jnp.dot(ikc, iq_heads[hh], preferred_element_type=F32)
            isc = isc + iws[hh] * jnp.maximum(s, 0.0)
        if diag:
            isc = jnp.where(k0 + key_in <= q_pos, isc, NEG_INF)
        key_ref[pl.ds(k0, tk), :] = _sort_key(isc)

    def score_body(c, _):
        score_chunk(c, False)
        return 0

    lax.fori_loop(0, jd, score_body, 0)
    score_chunk(jd, True)

    _select_topk_bias(key_ref, nch, tk, tq, topk, keys_on_lanes=False)

    qs = []
    for p in range(H_DSA // 2):
        qt = qt_ref[p * LANES:(p + 1) * LANES, :].astype(F32)
        qs += [(qt * lo).astype(BF16), (qt * hi).astype(BF16)]

    def pair_cols(hh):
        return slice((hh // 2) * LANES, (hh // 2 + 1) * LANES)

    def scores(c):
        k0 = pl.multiple_of(c * tk, tk)
        bias = pltpu.bitcast(key_ref[pl.ds(k0, tk), :], F32)
        out = []
        for hh in range(H_DSA):
            st = jnp.dot(k_ref[pl.ds(k0, tk), pair_cols(hh)], qs[hh],
                         preferred_element_type=F32) + bias
            out.append((st, _max_rows(st)))
        return tuple(out)

    lo_b = (lax.broadcasted_iota(jnp.int32, (LANES, tk), 0) < HEAD_DIM).astype(F32).astype(BF16)
    hi_b = (1.0 - lo_b.astype(F32)).astype(BF16)

    def consume(s, c, carry):
        k0 = pl.multiple_of(c * tk, tk)
        out = []
        for p in range(H_DSA // 2):
            vts = _values_with_ones(vt_ref[p * LANES:(p + 1) * LANES, pl.ds(k0, tk)], lo_b, hi_b)
            for e in range(2):
                hh = 2 * p + e
                out.append(_flash_t_update(*s[hh], *carry[hh], vts[e], True))
        return tuple(out)

    def trip(c, state):
        s, carry = state
        s_next = scores(c + 1)
        return s_next, consume(s, c, carry)

    s, carry = lax.fori_loop(0, jd, trip,
                             (scores(0), tuple(_flash_t_init(tq) for _ in range(H_DSA))))
    causal = jd * tk + key_in <= q_pos
    s = tuple(jnp.where(causal, sh, NEG_INF) for sh, _ in s)
    carry = consume(tuple((sh, _max_rows(sh)) for sh in s), jd, carry)
    outs = [_flash_t_pair_out(carry[2 * p][1], carry[2 * p + 1][1]) for p in range(H_DSA // 2)]
    o_ref[...] = jnp.concatenate(outs, axis=0).astype(BF16)


def _dsa_attention(qt, iqt, misct, ikb, k, vt, b, t_len, tq, tk, topk):
    nq = t_len // tq
    colblk = lambda bi, i: (0, bi * nq + i)
    per_b = lambda bi, i: (bi, 0)
    return pl.pallas_call(
        functools.partial(_dsa_kernel, tq=tq, tk=tk, topk=topk),
        name="dsa_attention",
        grid=(b, nq),
        in_specs=[pl.BlockSpec((W_DSA, tq), colblk), pl.BlockSpec((W_IDX, tq), colblk),
                  pl.BlockSpec((LANES, tq), colblk), pl.BlockSpec((t_len, LANES), per_b),
                  pl.BlockSpec((t_len, W_DSA), per_b),
                  pl.BlockSpec((W_DSA, t_len), lambda bi, i: (0, bi))],
        out_specs=pl.BlockSpec((W_DSA, tq), colblk),
        out_shape=jax.ShapeDtypeStruct(qt.shape, BF16),
        scratch_shapes=[pltpu.VMEM((t_len, tq), jnp.int32)],
        compiler_params=_cparams(("parallel", "arbitrary")),
    )(qt, iqt, misct, ikb, k, vt)


def _mem_kernel(q_ref, mkv_ref, o_ref, *, tq):
    lo, hi = _pair_masks(tq)
    outs = []
    for p in range(H_MEM // 2):
        q2 = q_ref[:, p * LANES:(p + 1) * LANES].astype(F32)
        mk = mkv_ref[0, :, p * LANES:(p + 1) * LANES].astype(BF16)
        mv = mkv_ref[0, :, W_MEM + p * LANES:W_MEM + (p + 1) * LANES].astype(BF16)

        def head(qh):
            s = _nt_dot(qh, mk)
            pr = jnp.exp(s - jnp.max(s, axis=-1, keepdims=True))
            l = jnp.sum(pr, axis=-1, keepdims=True)
            return jnp.dot(pr.astype(BF16), mv, preferred_element_type=F32) / l

        outs.append(head((q2 * lo).astype(BF16)) * lo + head((q2 * hi).astype(BF16)) * hi)
    o_ref[...] = jnp.concatenate(outs, axis=1).astype(BF16)


def _mem_attention(q, mkv, b, t_len, tq):
    nq = t_len // tq
    n_mem = mkv.shape[1]
    return pl.pallas_call(
        functools.partial(_mem_kernel, tq=tq),
        grid=(b, nq),
        in_specs=[pl.BlockSpec((tq, W_MEM), lambda bi, i: (bi * nq + i, 0)),
                  pl.BlockSpec((1, n_mem, 2 * W_MEM), lambda bi, i: (bi, 0, 0))],
        out_specs=pl.BlockSpec((tq, W_MEM), lambda bi, i: (bi * nq + i, 0)),
        out_shape=jax.ShapeDtypeStruct(q.shape, BF16),
        compiler_params=_cparams(("parallel", "arbitrary")),
    )(q, mkv)


def _rms(x, g):
    return x * lax.rsqrt(jnp.mean(x * x, axis=-1, keepdims=True) + EPS) * g


def _merge_kernel(x_ref, of_ref, od_ref, os_ref, om_ref, gpre_ref, gpost_ref, wg_ref,
                  wf_ref, wd_ref, ws_ref, wm_ref, wo_ref, y_ref, *, transposed):
    x = x_ref[...]
    d = x.shape[1]
    hb = _rms(x, gpre_ref[...]).astype(BF16)
    merged = None
    for i, (o_ref, w_ref) in enumerate(((of_ref, wf_ref), (od_ref, wd_ref),
                                        (os_ref, ws_ref), (om_ref, wm_ref))):
        gate = jax.nn.sigmoid(_nt_dot(hb, wg_ref[i * d:(i + 1) * d, :]))
        o = o_ref[...]
        if transposed and i < 2:
            o = o.astype(F32).T
        br = jnp.dot(o.astype(BF16), w_ref[...], preferred_element_type=F32)
        merged = gate * br if merged is None else merged + gate * br
    y = jnp.dot(merged.astype(BF16), wo_ref[...], preferred_element_type=F32)
    y_ref[...] = x + _rms(y, gpost_ref[...])


def _merge(x, o_fox, o_dsa, o_sb, o_mem, gpre, gpost, wg, wf, wd, ws, wm, wo, tm, transposed):
    n, d = x.shape
    row = lambda i: (i, 0)
    const = lambda i: (0, 0)
    full = lambda a: pl.BlockSpec(a.shape, const)

    def o_spec(o, can_t):
        if transposed and can_t:
            return pl.BlockSpec((o.shape[0], tm), lambda i: (0, i))
        return pl.BlockSpec((tm, o.shape[1]), row)

    return pl.pallas_call(
        functools.partial(_merge_kernel, transposed=transposed),
        name="branch_merge",
        grid=(n // tm,),
        in_specs=[pl.BlockSpec((tm, d), row), o_spec(o_fox, True), o_spec(o_dsa, True),
                  o_spec(o_sb, False), o_spec(o_mem, False)]
        + [full(a) for a in (gpre, gpost, wg, wf, wd, ws, wm, wo)],
        out_specs=pl.BlockSpec((tm, d), row),
        out_shape=jax.ShapeDtypeStruct((n, d), F32),
        compiler_params=_cparams(("parallel",)),
    )(x, o_fox, o_dsa, o_sb, o_mem, gpre, gpost, wg, wf, wd, ws, wm, wo)


def _ffn_kernel(x_ref, gpre_ref, gpost_ref, wi_ref, wo_ref, y_ref, *, d_ff, tc):
    x = x_ref[...]
    hb = _rms(x, gpre_ref[...]).astype(BF16)
    y = jnp.zeros(x.shape, F32)
    for c0 in range(0, d_ff, tc):
        gate = jnp.dot(hb, wi_ref[:, c0:c0 + tc], preferred_element_type=F32)
        up = jnp.dot(hb, wi_ref[:, d_ff + c0:d_ff + c0 + tc], preferred_element_type=F32)
        act = (gate * jax.nn.sigmoid(gate)) * up
        y = y + jnp.dot(act.astype(BF16), wo_ref[c0:c0 + tc, :], preferred_element_type=F32)
    y_ref[...] = x + _rms(y, gpost_ref[...])


def _ffn(x, gpre, gpost, wi, wo, tm):
    n, d = x.shape
    d_ff = wo.shape[0]
    tc = 256 if d_ff % 256 == 0 else d_ff
    row = lambda i: (i, 0)
    const = lambda i: (0, 0)
    return pl.pallas_call(
        functools.partial(_ffn_kernel, d_ff=d_ff, tc=tc),
        name="swiglu_ffn",
        grid=(n // tm,),
        in_specs=[pl.BlockSpec((tm, d), row), pl.BlockSpec(gpre.shape, const),
                  pl.BlockSpec(gpost.shape, const), pl.BlockSpec(wi.shape, const),
                  pl.BlockSpec(wo.shape, const)],
        out_specs=pl.BlockSpec((tm, d), row),
        out_shape=jax.ShapeDtypeStruct((n, d), F32),
        compiler_params=_cparams(("parallel",)),
    )(x, gpre, gpost, wi, wo)


def _dec_score_kernel(pt_ref, iqh_ref, iwb_ref, iknew_ref, *rest, npages):
    pages = rest[:npages]
    o_ref = rest[npages]
    iqh = iqh_ref[0]
    iwb = iwb_ref[0]
    for j in range(npages):
        s = jnp.dot(iqh, pages[j][0, 0].astype(BF16), preferred_element_type=F32)
        o_ref[0, :, j * LANES:(j + 1) * LANES] = jnp.sum(iwb * jnp.maximum(s, 0.0), axis=0,
                                                         keepdims=True)
    ik_new = iknew_ref[0][:, :IDX_DIM].astype(BF16).astype(F32)
    s_new = jnp.sum(iqh.astype(F32) * ik_new, axis=-1, keepdims=True)
    isc_new = jnp.sum(iwb[:, 0:1] * jnp.maximum(s_new, 0.0), axis=0, keepdims=True)
    lane = lax.broadcasted_iota(jnp.int32, (1, LANES), 1)
    o_ref[0, :, npages * LANES:] = jnp.where(lane == 0, isc_new, NEG_INF)


def _dec_scores(pt_flat, iqh, iwb, ik_new, idxk_pool, layer, nb, npages):
    width = (npages + 1) * LANES
    seq = lambda bi, pt: (bi, 0, 0)

    def page_map(j):
        return lambda bi, pt: (layer, pt[bi * npages + j], 0, 0)

    grid_spec = pltpu.PrefetchScalarGridSpec(
        num_scalar_prefetch=1, grid=(nb,),
        in_specs=[pl.BlockSpec((1, 8, IDX_DIM), seq), pl.BlockSpec((1, 8, LANES), seq),
                  pl.BlockSpec((1, 1, LANES), seq)]
        + [pl.BlockSpec((1, 1, IDX_DIM, PAGE_SIZE), page_map(j)) for j in range(npages)],
        out_specs=pl.BlockSpec((1, 1, width), seq))
    return pl.pallas_call(
        functools.partial(_dec_score_kernel, npages=npages),
        name="decode_index_scores",
        grid_spec=grid_spec,
        out_shape=jax.ShapeDtypeStruct((nb, 1, width), F32),
        compiler_params=_cparams(("arbitrary",)),
    )(pt_flat, iqh, iwb, ik_new, *([idxk_pool] * npages))


def _dec_select_kernel(isc_ref, o_ref, key_ref, *, topk):
    rows, width = isc_ref.shape
    key_ref[...] = _sort_key(isc_ref[...])
    _select_topk_bias(key_ref, width // LANES, LANES, rows, topk, keys_on_lanes=True)
    o_ref[...] = pltpu.bitcast(key_ref[...], F32)


def _dec_select(isc, topk):
    return pl.pallas_call(
        functools.partial(_dec_select_kernel, topk=topk),
        out_shape=jax.ShapeDtypeStruct(isc.shape, F32),
        scratch_shapes=[pltpu.VMEM(isc.shape, jnp.int32)],
        compiler_params=pltpu.CompilerParams(vmem_limit_bytes=VMEM_LIMIT),
    )(isc)


def _head_rows(width):
    sub = lax.broadcasted_iota(jnp.int32, (8, width), 0)
    lane = lax.broadcasted_iota(jnp.int32, (8, width), 1)
    return ((lane >> 6) == sub).astype(F32)


def _dec_attn_kernel(pt_ref, fq_ref, dq_ref, sq_ref, mq_ref, fnew_ref, dnew_ref, lnew_ref,
                     bias_ref, mem_ref, *rest, npages):
    fox_pages = rest[0:npages]
    logf_pages = rest[npages:2 * npages]
    dsa_pages = rest[2 * npages:3 * npages]
    sb_pages = rest[3 * npages:4 * npages]
    of_ref, od_ref, os_ref, om_ref = rest[4 * npages:]

    row = lax.broadcasted_iota(jnp.int32, (PAGE_SIZE, PAGE_SIZE), 0)
    col = lax.broadcasted_iota(jnp.int32, (PAGE_SIZE, PAGE_SIZE), 1)
    after = (row > col).astype(BF16)

    past = npages * PAGE_SIZE

    def keys_t(pages):
        return jnp.concatenate([pg[0, 0, 0].astype(BF16) for pg in pages], axis=1)

    def values_t(pages):
        return jnp.concatenate([pg[0, 0, 1].astype(BF16) for pg in pages], axis=1)

    def page_rows(x):
        return jnp.concatenate([x[:, j * PAGE_SIZE:(j + 1) * PAGE_SIZE] for j in range(npages)],
                               axis=0)

    def suffix_sums(x_rows, run):
        h1, h2, h3 = _split3(x_rows)
        inner = (jnp.dot(h1, after, preferred_element_type=F32)
                 + jnp.dot(h2, after, preferred_element_type=F32)
                 + jnp.dot(h3, after, preferred_element_type=F32))
        tot = jnp.sum(x_rows, axis=-1, keepdims=True)
        outs = [None] * npages
        for j in range(npages - 1, -1, -1):
            outs[j] = inner[j * 8:(j + 1) * 8] + run
            run = run + tot[j * 8:(j + 1) * 8]
        return jnp.concatenate(outs, axis=1)

    def head_diag(o, mask):
        return jnp.sum(o * mask, axis=0, keepdims=True)

    def softmax_pv(s_past, s_new, vt_all, v_new):
        m = jnp.maximum(jnp.max(s_past, axis=-1, keepdims=True), s_new)
        p_new = jnp.exp(s_new - m)
        pr = jnp.exp(s_past - m)
        l = p_new + jnp.sum(pr, axis=-1, keepdims=True)
        acc = p_new.astype(BF16).astype(F32) * v_new + _nt_dot(pr.astype(BF16), vt_all)
        return acc / l

    mask_f = _head_rows(W_FOX)
    qf = (fq_ref[0].astype(F32) * mask_f).astype(BF16)
    k_new = fnew_ref[0][:, :W_FOX].astype(BF16).astype(F32)
    v_new = fnew_ref[0][:, W_FOX:].astype(BF16).astype(F32)
    s_new = jnp.sum(qf.astype(F32) * k_new, axis=-1, keepdims=True)
    lf_rows = jnp.concatenate([pg[0, 0] for pg in logf_pages], axis=0)
    decay = suffix_sums(lf_rows, lnew_ref[0][:, 0:1])
    s_past = jnp.dot(qf, keys_t(fox_pages), preferred_element_type=F32) + decay
    of_ref[0] = head_diag(softmax_pv(s_past, s_new, values_t(fox_pages), v_new), mask_f)

    mask_d = _head_rows(W_DSA)
    qd = (dq_ref[0].astype(F32) * mask_d).astype(BF16)
    k_new = dnew_ref[0][:, :W_DSA].astype(BF16).astype(F32)
    v_new = dnew_ref[0][:, W_DSA:].astype(BF16).astype(F32)
    s_new = (jnp.sum(qd.astype(F32) * k_new, axis=-1, keepdims=True)
             + bias_ref[0][:, past:past + 1])
    s_past = jnp.dot(qd, keys_t(dsa_pages), preferred_element_type=F32) + bias_ref[0][:, :past]
    od_ref[0] = head_diag(softmax_pv(s_past, s_new, values_t(dsa_pages), v_new), mask_d)

    mask_s = _head_rows(W_SB)
    qs = (sq_ref[0].astype(F32) * mask_s).astype(BF16)
    z = jnp.dot(qs, keys_t(sb_pages), preferred_element_type=F32)
    tl = _softplus_tail(z)
    log_beta = jnp.minimum(z, 0.0) - tl
    log_1mb = -jnp.maximum(z, 0.0) - tl
    tail = suffix_sums(page_rows(log_1mb), jnp.zeros((8, 1), F32))
    a = jnp.exp(log_beta + tail)
    os_ref[0] = head_diag(_nt_dot(a.astype(BF16), values_t(sb_pages)), mask_s)

    mask_m = _head_rows(W_MEM)
    qm = (mq_ref[0].astype(F32) * mask_m).astype(BF16)
    s = jnp.dot(qm, mem_ref[0, 0, 0].astype(BF16), preferred_element_type=F32)
    pr = jnp.exp(s - jnp.max(s, axis=-1, keepdims=True))
    l = jnp.sum(pr, axis=-1, keepdims=True)
    o = _nt_dot(pr.astype(BF16), mem_ref[0, 0, 1].astype(BF16))
    om_ref[0] = head_diag(o / l, mask_m)


def _dec_attention(pt_flat, fq, dq, sq, mq, fnew, dnew, lnew, bias, mem, fox_pool, logf_pool,
                   dsa_pool, sb_pool, layer, nb, npages):
    seq = lambda bi, pt: (bi, 0, 0)

    def page_map(j, nd):
        return lambda bi, pt: (layer, pt[bi * npages + j]) + (0,) * nd

    def pages(pool):
        blk = (1, 1) + pool.shape[2:]
        return [pl.BlockSpec(blk, page_map(j, len(blk) - 2)) for j in range(npages)]

    def seq_spec(a):
        return pl.BlockSpec((1,) + a.shape[1:], seq)

    mem_spec = pl.BlockSpec((1, 1) + mem.shape[2:], lambda bi, pt: (layer, bi, 0, 0, 0))
    grid_spec = pltpu.PrefetchScalarGridSpec(
        num_scalar_prefetch=1, grid=(nb,),
        in_specs=[seq_spec(a) for a in (fq, dq, sq, mq, fnew, dnew, lnew, bias)] + [mem_spec]
        + pages(fox_pool) + pages(logf_pool) + pages(dsa_pool) + pages(sb_pool),
        out_specs=[pl.BlockSpec((1, 1, wd), seq) for wd in (W_FOX, W_DSA, W_SB, W_MEM)])
    return pl.pallas_call(
        functools.partial(_dec_attn_kernel, npages=npages),
        name="decode_attention",
        grid_spec=grid_spec,
        out_shape=[jax.ShapeDtypeStruct((nb, 1, wd), F32) for wd in (W_FOX, W_DSA, W_SB, W_MEM)],
        compiler_params=_cparams(("arbitrary",)),
    )(pt_flat, fq, dq, sq, mq, fnew, dnew, lnew, bias, mem,
      *([fox_pool] * npages), *([logf_pool] * npages), *([dsa_pool] * npages),
      *([sb_pool] * npages))


def _rope_tables(pos):
    rd = HEAD_DIM // 4
    half = rd // 2
    inv_freq = ROPE_THETA ** (-jnp.arange(half, dtype=F32) * 2.0 / rd)
    ang = pos.astype(F32)[:, None] * inv_freq[None, :]
    cos, sin = jnp.cos(ang), jnp.sin(ang)
    n = pos.shape[0]
    one = jnp.ones((n, HEAD_DIM - rd), F32)
    zero = jnp.zeros((n, HEAD_DIM - rd), F32)
    zh = jnp.zeros((n, half), F32)
    cos64 = jnp.concatenate([cos, cos, one], axis=1)
    sa64 = jnp.concatenate([-sin, zh, zero], axis=1)
    sb64 = jnp.concatenate([zh, sin, zero], axis=1)
    dup = lambda a: jnp.concatenate([a, a], axis=1)
    return dup(cos64), dup(sa64), dup(sb64), cos.T, sin.T


def _prep_w_in(w_in_l, b_forget_l, d_model):
    offs = np.cumsum([0, W_FOX, W_FOX, W_FOX, H_FOX, W_DSA, W_DSA, W_DSA, W_IDX, IDX_DIM,
                      IDX_HEADS, W_SB, W_SB, W_SB, W_MEM])
    (o_fq, o_fk, o_fv, o_ff, o_dq, o_dk, o_dv, o_iq, o_ik, o_iw, o_sq, o_sk, o_sv, o_mq,
     o_g) = [int(v) for v in offs]
    wt = jnp.transpose(w_in_l)
    sl = lambda o, n: wt[o:o + n]
    ik = sl(o_ik, IDX_DIM)
    misc = jnp.concatenate([sl(o_ff, H_FOX), sl(o_iw, IDX_HEADS),
                            jnp.zeros((LANES - H_FOX - IDX_HEADS, d_model), F32)], axis=0)
    wp = jnp.concatenate([sl(o_fq, 3 * W_FOX), sl(o_dq, 3 * W_DSA), sl(o_iq, W_IDX),
                          sl(o_sq, 3 * W_SB), sl(o_mq, W_MEM), ik, ik, misc], axis=0)
    wg = wt[o_g:]
    bfp = jnp.concatenate([b_forget_l, jnp.zeros((LANES - H_FOX,), F32)])[None, :]
    return wp.astype(BF16), wg.astype(BF16), bfp


def kernel(x_prompt, x_sample, cache_fox_kv, cache_fox_logf, cache_dsa_kv, cache_dsa_idxk,
           cache_sb_kv, cache_mem_kv, page_table, mem_prompt, w_in, b_forget, w_mem_kv,
           w_br_fox, w_br_dsa, w_br_sb, w_br_mem, w_out, w_ffn_in, w_ffn_out,
           g_mix_pre, g_mix_post, g_ffn_pre, g_ffn_post):
    bp, t_len, d = x_prompt.shape
    nb = x_sample.shape[0]
    depth = w_in.shape[0]
    npages = page_table.shape[1]
    past_len = npages * PAGE_SIZE
    n_pool = cache_fox_kv.shape[1]
    n_mem = mem_prompt.shape[1]
    n_p = bp * t_len

    tm = min(256, t_len)
    t_att = min(256, t_len)
    tq_dsa = min(256, t_len)
    tk_dsa = min(512, t_len)
    topk_p = min(DSA_TOPK_MAX, t_len // 4)
    topk_s = min(DSA_TOPK_MAX, (past_len + 1) // 4)
    assert tk_dsa >= topk_p and t_len % tk_dsa == 0 and t_len % tm == 0

    tabs_p = _rope_tables(jnp.arange(t_len, dtype=jnp.int32))
    tabs_s = _rope_tables(jnp.full((nb,), past_len, jnp.int32))
    pt_flat = page_table.reshape(-1).astype(jnp.int32)

    def kv_view(cache, width):
        view = jnp.transpose(cache, (0, 1, 3, 4, 5, 2))
        return view.reshape(cache.shape[:2] + (2, width, cache.shape[2]))

    fox_t = kv_view(cache_fox_kv, W_FOX)
    dsa_t = kv_view(cache_dsa_kv, W_DSA)
    sb_t = kv_view(cache_sb_kv, W_SB)
    mem_t = kv_view(cache_mem_kv, W_MEM)
    idxk_t = jnp.transpose(cache_dsa_idxk, (0, 1, 3, 2))
    logf_t = jnp.pad(jnp.transpose(cache_fox_logf, (0, 1, 3, 2)),
                     ((0, 0), (0, 0), (0, 8 - H_FOX), (0, 0)))

    xp = x_prompt.reshape(n_p, d)
    xs = x_sample.reshape(nb, d)
    mem_flat = mem_prompt.reshape(bp * n_mem, d)

    rows_p, rows_s, mem_p = [], [], []
    stacked_p = None
    for l in range(depth):
        wp, wg, bfp = _prep_w_in(w_in[l], b_forget[l], d)
        gpre, gpost = g_mix_pre[l][None, :], g_mix_post[l][None, :]
        fpre, fpost = g_ffn_pre[l][None, :], g_ffn_post[l][None, :]
        wf, wd_, ws, wm = (w.astype(BF16) for w in (w_br_fox[l], w_br_dsa[l], w_br_sb[l], w_br_mem[l]))
        wo = w_out[l].astype(BF16)
        wi, wfo = w_ffn_in[l].astype(BF16), w_ffn_out[l].astype(BF16)

        mem_kv = _matmul(mem_flat, w_mem_kv[l].astype(BF16))
        (fqt, fkv_st, fk, fvt, dqt, dkv_st, dk, dvt, iqt, sq, skv_st, sk, sv, mq, ik_st, ikb,
         misc, misct) = _proj(xp, gpre, wp, tabs_p, bfp, tm, True, stacked_p, l, depth)
        stacked_p = (fkv_st, dkv_st, skv_st, ik_st)
        logf = misc[:, :H_FOX].reshape(bp, t_len, H_FOX)
        c = jnp.cumsum(logf, axis=1) * LOG2_E
        fqt_full, fk_full = _fox_operands(fqt, fk, c.reshape(n_p, H_FOX))
        o_fox = _fox_attention(fqt_full, fk_full, fvt, bp, t_len, t_att, tk_dsa)
        o_dsa = _dsa_attention(dqt, iqt, misct, ikb, dk, dvt, bp, t_len, tq_dsa, tk_dsa, topk_p)
        o_sb = _sb_attention(sq, sk, sv, bp, t_len, t_att)
        o_mem = _mem_attention(mq, mem_kv.reshape(bp, n_mem, 2 * W_MEM), bp, t_len, tm)
        xp = _merge(xp, o_fox, o_dsa, o_sb, o_mem, gpre, gpost, wg, wf, wd_, ws, wm, wo, tm, True)
        xp = _ffn(xp, fpre, fpost, wi, wfo, tm)
        rows_p.append(logf)
        mem_p.append(mem_kv.reshape(bp, n_mem, 2, H_MEM, HEAD_DIM))

        (fq, fkv, fk, fv, dq, dkv, dk, dv, iq, sq, skv, sk, sv, mq, ik32, ikb,
         misc) = _proj(xs, gpre, wp, tabs_s, bfp, nb, False)
        r3 = lambda a: a.reshape(nb, 1, a.shape[1])
        iqh = jnp.pad(iq.reshape(nb, IDX_HEADS, IDX_DIM), ((0, 0), (0, 8 - IDX_HEADS), (0, 0)))
        iwb = jnp.pad(misc[:, MISC_IW:MISC_IW + IDX_HEADS], ((0, 0), (0, 8 - IDX_HEADS)))
        iwb = jnp.broadcast_to(iwb[:, :, None], (nb, 8, LANES))
        isc = _dec_scores(pt_flat, iqh, iwb, r3(ik32), idxk_t, l, nb, npages)
        bias = _dec_select(isc.reshape(nb, -1), topk_s).reshape(nb, 1, -1)
        lnew = jnp.pad(misc[:, :H_FOX], ((0, 0), (0, 8 - H_FOX)))
        lnew = jnp.broadcast_to(lnew[:, :, None], (nb, 8, LANES))
        o_fox, o_dsa, o_sb, o_mem = _dec_attention(
            pt_flat, r3(fq), r3(dq), r3(sq), r3(mq), r3(fkv), r3(dkv), lnew, bias,
            mem_t, fox_t, logf_t, dsa_t, sb_t, l, nb, npages)
        sq2 = lambda a: a.reshape(nb, a.shape[2])
        xs = _merge(xs, sq2(o_fox), sq2(o_dsa), sq2(o_sb), sq2(o_mem), gpre, gpost, wg, wf, wd_,
                    ws, wm, wo, nb, False)
        xs = _ffn(xs, fpre, fpost, wi, wfo, nb)
        rows_s.append((fkv.reshape(nb, 1, 2, H_FOX, HEAD_DIM), misc[:, :H_FOX].reshape(nb, 1, H_FOX),
                       dkv.reshape(nb, 1, 2, H_DSA, HEAD_DIM),
                       ik32[:, :IDX_DIM].reshape(nb, 1, IDX_DIM),
                       skv.reshape(nb, 1, 2, H_SB, HEAD_DIM)))

    stk = lambda rows, i: jnp.stack([r[i] for r in rows], axis=0)
    fkv_st, dkv_st, skv_st, ik_st = stacked_p

    def kv_rows(buf, heads):
        return jnp.transpose(buf.reshape(depth, bp, 2, heads, HEAD_DIM, t_len), (0, 1, 5, 2, 3, 4))

    return (xp.reshape(bp, t_len, d), xs.reshape(nb, 1, d),
            kv_rows(fkv_st, H_FOX), jnp.stack(rows_p, axis=0), kv_rows(dkv_st, H_DSA),
            jnp.transpose(ik_st, (0, 1, 3, 2)), kv_rows(skv_st, H_SB),
            jnp.stack(mem_p, axis=0),
            stk(rows_s, 0), stk(rows_s, 1), stk(rows_s, 2), stk(rows_s, 3), stk(rows_s, 4))
```

```python
import functools

import jax
import jax.numpy as jnp
import numpy as np
from jax import lax
from jax.experimental import pallas as pl
from jax.experimental.pallas import tpu as pltpu

HEAD_DIM = 64
H_FOX = 6
H_DSA = 6
H_SB = 4
H_MEM = 4
IDX_HEADS = 4
IDX_DIM = 64
DSA_TOPK_MAX = 256
ROPE_THETA = 500000.0
N_BRANCH = 4
EPS = 1e-6
PAGE_SIZE = 128

LANES = 128
VMEM_LIMIT = 56 * 1024 * 1024

F32 = jnp.float32
BF16 = jnp.bfloat16
NEG_INF = float("-inf")
INT_MIN = -2 ** 31
SB_DEAD = -120.0
LOG2_E = 1.4426950408889634
MIN_NORMAL = 2.0 ** -126

W_FOX = H_FOX * HEAD_DIM
W_DSA = H_DSA * HEAD_DIM
W_SB = H_SB * HEAD_DIM
W_MEM = H_MEM * HEAD_DIM
W_IDX = IDX_HEADS * IDX_DIM

C_FQ, C_FKV = 0, W_FOX
C_DQ = C_FKV + 2 * W_FOX
C_DK = C_DQ + W_DSA
C_DV = C_DK + W_DSA
C_IQ = C_DV + W_DSA
C_SQ = C_IQ + W_IDX
C_SKV = C_SQ + W_SB
C_MQ = C_SKV + 2 * W_SB
C_IK = C_MQ + W_MEM
C_MISC = C_IK + LANES
N_PROJ = C_MISC + LANES
MISC_IW = H_FOX


def _cparams(sem):
    return pltpu.CompilerParams(dimension_semantics=sem, vmem_limit_bytes=VMEM_LIMIT)


def _nt_dot(a, b):
    return lax.dot_general(a, b, (((1,), (1,)), ((), ())), preferred_element_type=F32)


def _softplus_tail(z):
    return jnp.log1p(jnp.exp(-jnp.abs(z)))


def _split3(x):
    hi = x.astype(BF16)
    r1 = x - hi.astype(F32)
    mid = r1.astype(BF16)
    lo = (r1 - mid.astype(F32)).astype(BF16)
    return hi, mid, lo


def _proj_kernel(x_ref, g_ref, w_ref, cos_ref, sa_ref, sb_ref, bf_ref, cost_ref, sint_ref,
                 *refs, transposed, n_alias):
    (fq_ref, fkv_ref, fk_ref, fv_ref, dq_ref, dkv_ref, dk_ref, dv_ref, iq_ref, sq_ref, skv_ref,
     sk_ref, sv_ref, mq_ref, ik32_ref, ikb_ref, misc_ref, *extra) = refs[n_alias:]
    x = x_ref[...]
    h = x * lax.rsqrt(jnp.mean(x * x, axis=-1, keepdims=True) + EPS)
    hb = (h * g_ref[...]).astype(BF16)
    cosf, sa, sb = cos_ref[...], sa_ref[...], sb_ref[...]
    scale = HEAD_DIM ** -0.5

    def mm(c0, n):
        return _nt_dot(hb, w_ref[c0:c0 + n, :])

    def mm_t(c0, n):
        return _nt_dot(w_ref[c0:c0 + n, :], hb)

    def rope(z):
        outs = []
        for j in range(z.shape[1] // LANES):
            zj = z[:, j * LANES:(j + 1) * LANES]
            outs.append(zj * cosf + pltpu.roll(zj, LANES - 8, 1) * sa + pltpu.roll(zj, 8, 1) * sb)
        return outs[0] if len(outs) == 1 else jnp.concatenate(outs, axis=1)

    def rope_t(zt):
        cos_t, sin_t = cost_ref[...], sint_ref[...]
        half = HEAD_DIM // 8
        parts = []
        for hh in range(zt.shape[0] // HEAD_DIM):
            base = hh * HEAD_DIM
            x1, x2 = zt[base:base + half], zt[base + half:base + 2 * half]
            parts += [x1 * cos_t - x2 * sin_t, x2 * cos_t + x1 * sin_t,
                      zt[base + 2 * half:base + HEAD_DIM]]
        return jnp.concatenate(parts, axis=0)

    dk = rope(mm(C_DK, W_DSA))
    dk_ref[...] = dk.astype(BF16)
    skv = mm(C_SKV, 2 * W_SB)
    ik = rope(mm(C_IK, LANES))
    if transposed:
        scale2 = scale * LOG2_E
        fq_ref[...] = (mm_t(C_FQ, W_FOX) * scale2).astype(BF16)
        fkv_t = mm_t(C_FKV, 2 * W_FOX)
        fkv_ref[0, 0] = fkv_t
        fv_ref[...] = fkv_t[W_FOX:].astype(BF16)
        fk_ref[...] = mm(C_FKV, W_FOX).astype(BF16)
        dq_ref[...] = (rope_t(mm_t(C_DQ, W_DSA)) * scale2).astype(BF16)
        dv_t = mm_t(C_DV, W_DSA)
        dkv_ref[0, 0, :W_DSA] = rope_t(mm_t(C_DK, W_DSA))
        dkv_ref[0, 0, W_DSA:] = dv_t
        dv_ref[...] = dv_t.astype(BF16)
        iq_ref[...] = rope_t(mm_t(C_IQ, W_IDX)).astype(BF16)
        skv_ref[0, 0] = mm_t(C_SKV, 2 * W_SB)
        ik32_ref[0, 0] = rope_t(mm_t(C_IK, LANES))[:IDX_DIM]
        extra[0][...] = mm_t(C_MISC, LANES)
    else:
        fkv = mm(C_FKV, 2 * W_FOX)
        fkv_ref[...] = fkv
        fk_ref[...] = fkv[:, :W_FOX].astype(BF16)
        fq_ref[...] = (mm(C_FQ, W_FOX) * scale).astype(BF16)
        fv_ref[...] = fkv[:, W_FOX:].astype(BF16)
        dq_ref[...] = (rope(mm(C_DQ, W_DSA)) * scale).astype(BF16)
        dv = mm(C_DV, W_DSA)
        dkv_ref[:, :W_DSA] = dk
        dkv_ref[:, W_DSA:] = dv
        dv_ref[...] = dv.astype(BF16)
        iq_ref[...] = rope(mm(C_IQ, W_IDX)).astype(BF16)
        skv_ref[...] = skv
        ik32_ref[...] = ik

    sq_ref[...] = (mm(C_SQ, W_SB) * scale).astype(BF16)
    sk_ref[...] = skv[:, :W_SB].astype(BF16)
    sv_ref[...] = skv[:, W_SB:].astype(BF16)
    mq_ref[...] = (mm(C_MQ, W_MEM) * scale).astype(BF16)
    ikb_ref[...] = ik.astype(BF16)

    zm = mm(C_MISC, LANES)
    ff = zm + bf_ref[...]
    logf = -(jnp.maximum(-ff, 0.0) + _softplus_tail(ff))
    lane = lax.broadcasted_iota(jnp.int32, zm.shape, 1)
    misc_ref[...] = jnp.where(lane < H_FOX, logf, zm)


def _proj(x, g, w, tabs, bfp, tm, transposed, stacked=None, layer=0, depth=1):
    n, d = x.shape
    cosf, sa, sb, cos_t, sin_t = tabs
    nt = cosf.shape[0] // tm
    row = lambda i: (i, 0)
    col = lambda i: (0, i)
    tab = lambda i: (i % nt, 0)
    tab_t = lambda i: (0, i % nt)
    const = lambda i: (0, 0)
    stack_blk = lambda i: (layer, i // nt, 0, i % nt)
    widths = [(W_FOX, BF16, 't'), (2 * W_FOX, F32, 's'), (W_FOX, BF16, ''), (W_FOX, BF16, 't'),
              (W_DSA, BF16, 't'), (2 * W_DSA, F32, 's'), (W_DSA, BF16, ''), (W_DSA, BF16, 't'),
              (W_IDX, BF16, 't'), (W_SB, BF16, ''), (2 * W_SB, F32, 's'), (W_SB, BF16, ''),
              (W_SB, BF16, ''), (W_MEM, BF16, ''), (IDX_DIM if transposed else LANES, F32, 's'),
              (LANES, BF16, ''), (LANES, F32, '')]
    if transposed:
        widths.append((LANES, F32, 't'))
    out_specs, out_shape, stacked_idx = [], [], []
    for idx, (wd, dt, kind) in enumerate(widths):
        if transposed and kind == 't':
            out_specs.append(pl.BlockSpec((wd, tm), col))
            out_shape.append(jax.ShapeDtypeStruct((wd, n), dt))
        elif transposed and kind == 's':
            stacked_idx.append(idx)
            out_specs.append(pl.BlockSpec((1, 1, wd, tm), stack_blk))
            out_shape.append(jax.ShapeDtypeStruct((depth, n // (nt * tm), wd, nt * tm), dt))
        else:
            out_specs.append(pl.BlockSpec((tm, wd), row))
            out_shape.append(jax.ShapeDtypeStruct((n, wd), dt))
    in_specs = [pl.BlockSpec((tm, d), row), pl.BlockSpec((1, d), const),
                pl.BlockSpec((N_PROJ, d), const),
                pl.BlockSpec((tm, LANES), tab), pl.BlockSpec((tm, LANES), tab),
                pl.BlockSpec((tm, LANES), tab), pl.BlockSpec((1, LANES), const),
                pl.BlockSpec((8, tm), tab_t), pl.BlockSpec((8, tm), tab_t)]
    operands = [x, g, w, cosf, sa, sb, bfp, cos_t, sin_t]
    aliases = {}
    if stacked is not None:
        for k, buf in enumerate(stacked):
            aliases[len(operands)] = stacked_idx[k]
            in_specs.append(pl.BlockSpec(memory_space=pl.ANY))
            operands.append(buf)
    n_alias = len(aliases)
    return pl.pallas_call(
        functools.partial(_proj_kernel, transposed=transposed, n_alias=n_alias),
        name="input_projection",
        grid=(n // tm,),
        in_specs=in_specs,
        out_specs=out_specs,
        out_shape=out_shape,
        input_output_aliases=aliases,
        compiler_params=_cparams(("parallel",)),
    )(*operands)


def _matmul_kernel(a_ref, b_ref, o_ref):
    o_ref[...] = jnp.dot(a_ref[...].astype(BF16), b_ref[...], preferred_element_type=F32)


def _matmul(a, b):
    return pl.pallas_call(
        _matmul_kernel,
        out_shape=jax.ShapeDtypeStruct((a.shape[0], b.shape[1]), F32),
        compiler_params=pltpu.CompilerParams(vmem_limit_bytes=VMEM_LIMIT),
    )(a, b)


def _pair_masks(rows):
    lane = lax.broadcasted_iota(jnp.int32, (rows, LANES), 1)
    lo = (lane < HEAD_DIM).astype(F32)
    return lo, 1.0 - lo


def _fold_lanes(x):
    part = x[:, 0:LANES]
    for u in range(1, x.shape[1] // LANES):
        part = part + x[:, u * LANES:(u + 1) * LANES]
    return part


def _pair_masks_t(cols):
    sub = lax.broadcasted_iota(jnp.int32, (LANES, cols), 0)
    lo = (sub < HEAD_DIM).astype(F32)
    return lo, 1.0 - lo


def _tree_rows(x, op, nacc=4):
    parts = [x[i * 8:(i + 1) * 8] for i in range(x.shape[0] // 8)]
    accs = parts[:nacc]
    for i, part in enumerate(parts[nacc:]):
        accs[i % nacc] = op(accs[i % nacc], part)
    while len(accs) > 1:
        accs = [op(a, b) for a, b in zip(accs[0::2], accs[1::2])] + (accs[-1:] if len(accs) % 2 else [])
    return accs[0]


def _fold_rows(x):
    return _tree_rows(x, jnp.add)


def _max_rows(x):
    return _tree_rows(x, jnp.maximum)


def _flash_t_update(st, pmax, m, acc, vt1, guard):
    m_new = jnp.maximum(m, jnp.max(pmax, axis=0, keepdims=True))
    m_use = jnp.where(m_new == NEG_INF, 0.0, m_new) if guard else m_new
    alpha = jnp.exp2(m - m_use)
    p = jnp.exp2(st - m_use)
    acc = alpha * acc + jnp.dot(vt1, p.astype(BF16), preferred_element_type=F32)
    return m_new, acc


def _flash_t_init(cols):
    return jnp.full((1, cols), NEG_INF, F32), jnp.zeros((LANES, cols), F32)


def _flash_t_pair_out(acca, accb):
    return jnp.concatenate([acca[:HEAD_DIM] / acca[HEAD_DIM:HEAD_DIM + 1],
                            accb[HEAD_DIM:] / accb[0:1]], axis=0)


def _values_with_ones(vt, lo_b, hi_b):
    return vt * lo_b + hi_b, vt * hi_b + lo_b


FOX_AUG = 6


def _fox_kernel(qt_ref, k_ref, vt_ref, o_ref, *, tq, tk):
    qi = pl.program_id(2)
    q0 = pl.multiple_of(qi * tq, tq)
    jd = q0 // tk
    qt = qt_ref[...].astype(F32)
    sub = lax.broadcasted_iota(jnp.int32, (2 * LANES, tq), 0)
    aug = sub - LANES
    lo = jnp.logical_or(sub < HEAD_DIM, jnp.logical_and(aug >= 0, aug < FOX_AUG))
    hi = jnp.logical_or(jnp.logical_and(sub >= HEAD_DIM, sub < LANES),
                        jnp.logical_and(aug >= FOX_AUG, aug < 2 * FOX_AUG))
    qta = jnp.where(lo, qt, 0.0).astype(BF16)
    qtb = jnp.where(hi, qt, 0.0).astype(BF16)
    key_in = lax.broadcasted_iota(jnp.int32, (tk, tq), 0)
    q_pos = q0 + lax.broadcasted_iota(jnp.int32, (tk, tq), 1)

    def scores(j):
        k0 = pl.multiple_of(j * tk, tk)
        kc = k_ref[pl.ds(k0, tk), :]
        sa = jnp.dot(kc, qta, preferred_element_type=F32)
        sb = jnp.dot(kc, qtb, preferred_element_type=F32)
        return (sa, _max_rows(sa)), (sb, _max_rows(sb))

    lo_b = (lax.broadcasted_iota(jnp.int32, (LANES, tk), 0) < HEAD_DIM).astype(F32).astype(BF16)
    hi_b = (1.0 - lo_b.astype(F32)).astype(BF16)

    def consume(s, j, carry):
        ca, cb = carry
        vta, vtb = _values_with_ones(vt_ref[:, pl.ds(pl.multiple_of(j * tk, tk), tk)], lo_b, hi_b)
        return _flash_t_update(*s[0], *ca, vta, False), _flash_t_update(*s[1], *cb, vtb, False)

    def trip(j, state):
        s, carry = state
        s_next = scores(j + 1)
        return s_next, consume(s, j, carry)

    s, carry = lax.fori_loop(0, jd, trip, (scores(0), (_flash_t_init(tq), _flash_t_init(tq))))
    causal = jd * tk + key_in <= q_pos
    s = tuple(jnp.where(causal, sh, NEG_INF) for sh, _ in s)
    (_, acca), (_, accb) = consume(tuple((sh, _max_rows(sh)) for sh in s), jd, carry)
    o_ref[...] = _flash_t_pair_out(acca, accb).astype(BF16)


def _fox_operands(qt, k, c):
    n = k.shape[0]
    npair = H_FOX // 2

    def chop(v):
        bits = lax.bitcast_convert_type(v, jnp.uint32) & jnp.uint32(0xFFFF0000)
        return lax.bitcast_convert_type(bits, F32)

    def split(v):
        hi = chop(v)
        mid = chop(v - hi)
        lo = chop(v - hi - mid)
        return jnp.stack([hi, mid, lo], axis=-1).astype(BF16)

    pos = split(c)
    neg = split(-c)
    ones = jnp.ones((n, H_FOX, 3), BF16)
    pad = jnp.zeros((n, npair, LANES - 2 * FOX_AUG), BF16)
    k_aug = jnp.concatenate([neg, ones], axis=-1).reshape(n, npair, 2 * FOX_AUG)
    q_aug = jnp.concatenate([ones, pos], axis=-1).reshape(n, npair, 2 * FOX_AUG)
    k_full = jnp.concatenate([k.reshape(n, npair, LANES), k_aug, pad], axis=-1)
    q_aug = jnp.transpose(jnp.concatenate([q_aug, pad], axis=-1), (1, 2, 0))
    qt_full = jnp.concatenate([qt.reshape(npair, LANES, n), q_aug], axis=1)
    return qt_full.reshape(npair * 2 * LANES, n), k_full.reshape(n, npair * 2 * LANES)


def _fox_attention(qt, k, vt, b, t_len, tq, tk):
    npair = H_FOX // 2
    nq = t_len // tq
    return pl.pallas_call(
        functools.partial(_fox_kernel, tq=tq, tk=tk),
        name="fox_attention",
        grid=(b, npair, nq),
        in_specs=[pl.BlockSpec((2 * LANES, tq), lambda bi, p, i: (p, bi * nq + i)),
                  pl.BlockSpec((t_len, 2 * LANES), lambda bi, p, i: (bi, p)),
                  pl.BlockSpec((LANES, t_len), lambda bi, p, i: (p, bi))],
        out_specs=pl.BlockSpec((LANES, tq), lambda bi, p, i: (p, bi * nq + i)),
        out_shape=jax.ShapeDtypeStruct(vt.shape, BF16),
        compiler_params=_cparams(("parallel", "parallel", "arbitrary")),
    )(qt, k, vt)


def _sb_kernel(q_ref, k_ref, v_ref, o_ref, *, t):
    qi = pl.program_id(1)
    lo, hi = _pair_masks(t)
    row = lax.broadcasted_iota(jnp.int32, (t, t), 0)
    col = lax.broadcasted_iota(jnp.int32, (t, t), 1)
    strict = col < row
    after = (row > col).astype(BF16)

    qs = []
    for p in range(H_SB // 2):
        q2 = q_ref[:, p * LANES:(p + 1) * LANES].astype(F32)
        qs += [(q2 * lo).astype(BF16), (q2 * hi).astype(BF16)]

    def one(qh, kc, vc, run, acc, diag):
        z = _nt_dot(qh, kc)
        tl = _softplus_tail(z)
        log_beta = jnp.minimum(z, 0.0) - tl
        log_1mb = -jnp.maximum(z, 0.0) - tl
        if diag:
            log_1mb = jnp.where(strict, log_1mb, 0.0)
        h1, h2, h3 = _split3(log_1mb)
        tail = (jnp.dot(h1, after, preferred_element_type=F32)
                + jnp.dot(h2, after, preferred_element_type=F32)
                + jnp.dot(h3, after, preferred_element_type=F32))
        a = jnp.exp(log_beta + (tail + run))
        if diag:
            a = jnp.where(strict, a, 0.0)
        acc = acc + jnp.dot(a.astype(BF16), vc, preferred_element_type=F32)
        run = run + (tail[:, 0:1] + log_1mb[:, 0:1])
        return run, acc

    def chunk(j, carry, diag):
        k0 = pl.multiple_of(j * t, t)
        out = []
        for hh in range(H_SB):
            cols = slice((hh // 2) * LANES, (hh // 2 + 1) * LANES)
            out.append(one(qs[hh], k_ref[pl.ds(k0, t), cols], v_ref[pl.ds(k0, t), cols],
                           *carry[hh], diag))
        return tuple(out)

    zero = (jnp.zeros((t, 1), F32), jnp.zeros((t, LANES), F32))
    carry = chunk(qi, (zero,) * H_SB, True)

    def live(state):
        j, c = state
        top = c[0][0]
        for run, _ in c[1:]:
            top = jnp.maximum(top, run)
        return jnp.logical_and(j >= 0, jnp.max(top) > SB_DEAD)

    def older(state):
        j, c = state
        return j - 1, chunk(j, c, False)

    _, carry = lax.while_loop(live, older, (qi - 1, carry))
    outs = [carry[2 * p][1] * lo + carry[2 * p + 1][1] * hi for p in range(H_SB // 2)]
    o_ref[...] = jnp.concatenate(outs, axis=1).astype(BF16)


def _sb_attention(q, k, v, b, t_len, t):
    nq = t_len // t
    return pl.pallas_call(
        functools.partial(_sb_kernel, t=t),
        name="stick_breaking_attention",
        grid=(b, nq),
        in_specs=[pl.BlockSpec((t, W_SB), lambda bi, i: (bi * nq + i, 0)),
                  pl.BlockSpec((t_len, W_SB), lambda bi, i: (bi, 0)),
                  pl.BlockSpec((t_len, W_SB), lambda bi, i: (bi, 0))],
        out_specs=pl.BlockSpec((t, W_SB), lambda bi, i: (bi * nq + i, 0)),
        out_shape=jax.ShapeDtypeStruct(q.shape, BF16),
        compiler_params=_cparams(("parallel", "arbitrary")),
    )(q, k, v)


def _sort_key(x):
    x = jnp.where(x == 0.0, 0.0, x)
    bits = pltpu.bitcast(x, jnp.int32)
    return bits ^ ((bits >> 31) & 0x7FFFFFFF)


def _top_half(x):
    bits = pltpu.bitcast(x, jnp.int32) & jnp.int32(-65536)
    return pltpu.bitcast(bits, F32).astype(BF16)


def _select_topk_bias(key_ref, nch, tk, nq, topk, keys_on_lanes, hi_ref=None):
    key_axis = 1 if keys_on_lanes else 0
    vec = (nq, 1) if keys_on_lanes else (1, nq)

    def load(k0):
        return key_ref[:, pl.ds(k0, tk)] if keys_on_lanes else key_ref[pl.ds(k0, tk), :]

    def store(k0, val):
        if keys_on_lanes:
            key_ref[:, pl.ds(k0, tk)] = val
        else:
            key_ref[pl.ds(k0, tk), :] = val

    fold = _fold_lanes if keys_on_lanes else _fold_rows

    def count(pred):
        def body(c, acc):
            k0 = pl.multiple_of(c * tk, tk)
            return acc + fold(jnp.where(pred(load(k0), k0), 1.0, 0.0))
        part = (nq, LANES) if keys_on_lanes else (8, nq)
        acc = lax.fori_loop(0, nch, body, jnp.zeros(part, F32))
        return jnp.sum(acc, axis=key_axis, keepdims=True)

    kf = float(topk)

    def value_bit(i, ans):
        cand = ans | jnp.left_shift(jnp.int32(1), 31 - i)
        cand_s = cand ^ INT_MIN
        cnt = count(lambda blk, k0: blk >= cand_s)
        return jnp.where(cnt >= kf, cand, ans)

    first = 0
    ans = jnp.zeros(vec, jnp.int32)
    if hi_ref is not None:
        one_b, zero_b = jnp.ones((), BF16), jnp.zeros((), BF16)
        total = jnp.asarray(nch * tk, F32)
        below_all = INT_MIN + 0x007FFFFF

        def count_hi(cand16):
            def body(c, acc):
                k0 = pl.multiple_of(c * tk, tk)
                blk = hi_ref[pl.ds(k0, tk), :]
                hits = [jnp.where(blk[i * 16:(i + 1) * 16] >= cand16, one_b, zero_b)
                        for i in range(tk // 16)]
                accs = hits[:4]
                for i, hit in enumerate(hits[4:]):
                    accs[i % 4] = accs[i % 4] + hit
                return acc + ((accs[0] + accs[1]) + (accs[2] + accs[3])).astype(F32)
            acc = lax.fori_loop(0, nch, body, jnp.zeros((16, nq), F32))
            return jnp.sum(acc, axis=0, keepdims=True)

        def hi_bit(i, a):
            cand = a | jnp.left_shift(jnp.int32(1), 31 - i)
            cand_s = cand ^ INT_MIN
            bits = jnp.where(cand_s >= 0, cand_s, cand_s ^ 0x7FFFFFFF)
            tiny = jnp.where(bits < 0, 0, jnp.where((bits & 0x007F0000) != 0, 0x00800000, 0))
            bits = jnp.where((bits & 0x7F800000) == 0, tiny, bits)
            cand16 = _top_half(jnp.broadcast_to(pltpu.bitcast(bits, F32), (16, nq)))
            cnt = jnp.where(cand_s <= below_all, total, count_hi(cand16))
            return jnp.where(cnt >= kf, cand, a)

        first = 16
        ans = lax.fori_loop(0, first, hi_bit, ans)

    ans = lax.fori_loop(first, 32, value_bit, ans)
    thr = ans ^ INT_MIN
    need = kf - count(lambda blk, k0: blk > thr)

    r_i = lax.broadcasted_iota(jnp.int32, (tk, tk), 0)
    c_i = lax.broadcasted_iota(jnp.int32, (tk, tk), 1)
    tri = jnp.where(r_i <= c_i if keys_on_lanes else c_i <= r_i, 1.0, 0.0).astype(BF16)

    def write(c, base):
        k0 = pl.multiple_of(c * tk, tk)
        blk = load(k0)
        eq = blk == thr
        eqb = jnp.where(eq, 1.0, 0.0).astype(BF16)
        if keys_on_lanes:
            rank = jnp.dot(eqb, tri, preferred_element_type=F32)
            total = rank[:, tk - 1:tk]
        else:
            rank = jnp.dot(tri, eqb, preferred_element_type=F32)
            total = rank[tk - 1:tk, :]
        order = jnp.where(blk > thr, 0.0, jnp.where(eq, base + rank, jnp.inf))
        store(k0, pltpu.bitcast(jnp.where(order <= need, 0.0, NEG_INF), jnp.int32))
        return base + total

    lax.fori_loop(0, nch, write, jnp.zeros(vec, F32))


def _dsa_kernel(qt_ref, iqt_ref, misct_ref, ik_ref, k_ref, vt_ref, o_ref, key_ref, hi_ref, *,
                tq, tk, topk):
    qi = pl.program_id(1)
    q0 = qi * tq
    jd = q0 // tk
    nch = jd + 1
    lo, hi = _pair_masks_t(tq)
    key_in = lax.broadcasted_iota(jnp.int32, (tk, tq), 0)
    q_pos = q0 + lax.broadcasted_iota(jnp.int32, (tk, tq), 1)

    misct = misct_ref[...]
    iqt = iqt_ref[...].astype(F32)
    iq_heads = []
    for hh in range(IDX_HEADS):
        pair = iqt[(hh // 2) * LANES:(hh // 2 + 1) * LANES]
        iq_heads.append((pair * (lo if hh % 2 == 0 else hi)).astype(BF16))
    iws = [misct[MISC_IW + hh:MISC_IW + hh + 1] for hh in range(IDX_HEADS)]

    def score_chunk(c, diag):
        k0 = pl.multiple_of(c * tk, tk)
        ikc = ik_ref[pl.ds(k0, tk), :]
        isc = jnp.zeros((tk, tq), F32)
        for hh in range(IDX_HEADS):
            s = jnp.dot(ikc, iq_heads[hh], preferred_element_type=F32)
            isc = isc + iws[hh] * jnp.maximum(s, 0.0)
        if diag:
            isc = jnp.where(k0 + key_in <= q_pos, isc, NEG_INF)
        isc = jnp.where(jnp.abs(isc) < MIN_NORMAL, 0.0, isc)
        key_ref[pl.ds(k0, tk), :] = _sort_key(isc)
        hi_ref[pl.ds(k0, tk), :] = _top_half(isc)

    def score_body(c, _):
        score_chunk(c, False)
        return 0

    lax.fori_loop(0, jd, score_body, 0)
    score_chunk(jd, True)

    _select_topk_bias(key_ref, nch, tk, tq, topk, keys_on_lanes=False, hi_ref=hi_ref)

    qs = []
    for p in range(H_DSA // 2):
        qt = qt_ref[p * LANES:(p + 1) * LANES, :].astype(F32)
        qs += [(qt * lo).astype(BF16), (qt * hi).astype(BF16)]

    def pair_cols(hh):
        return slice((hh // 2) * LANES, (hh // 2 + 1) * LANES)

    def scores(c):
        k0 = pl.multiple_of(c * tk, tk)
        bias = pltpu.bitcast(key_ref[pl.ds(k0, tk), :], F32)
        out = []
        for hh in range(H_DSA):
            st = jnp.dot(k_ref[pl.ds(k0, tk), pair_cols(hh)], qs[hh],
                         preferred_element_type=F32) + bias
            out.append((st, _max_rows(st)))
        return tuple(out)

    lo_b = (lax.broadcasted_iota(jnp.int32, (LANES, tk), 0) < HEAD_DIM).astype(F32).astype(BF16)
    hi_b = (1.0 - lo_b.astype(F32)).astype(BF16)

    def consume(s, c, carry):
        k0 = pl.multiple_of(c * tk, tk)
        out = []
        for p in range(H_DSA // 2):
            vts = _values_with_ones(vt_ref[p * LANES:(p + 1) * LANES, pl.ds(k0, tk)], lo_b, hi_b)
            for e in range(2):
                hh = 2 * p + e
                out.append(_flash_t_update(*s[hh], *carry[hh], vts[e], True))
        return tuple(out)

    def trip(c, state):
        s, carry = state
        s_next = scores(c + 1)
        return s_next, consume(s, c, carry)

    s, carry = lax.fori_loop(0, jd, trip,
                             (scores(0), tuple(_flash_t_init(tq) for _ in range(H_DSA))))
    causal = jd * tk + key_in <= q_pos
    s = tuple(jnp.where(causal, sh, NEG_INF) for sh, _ in s)
    carry = consume(tuple((sh, _max_rows(sh)) for sh in s), jd, carry)
    outs = [_flash_t_pair_out(carry[2 * p][1], carry[2 * p + 1][1]) for p in range(H_DSA // 2)]
    o_ref[...] = jnp.concatenate(outs, axis=0).astype(BF16)


def _dsa_attention(qt, iqt, misct, ikb, k, vt, b, t_len, tq, tk, topk):
    nq = t_len // tq
    colblk = lambda bi, i: (0, bi * nq + i)
    per_b = lambda bi, i: (bi, 0)
    return pl.pallas_call(
        functools.partial(_dsa_kernel, tq=tq, tk=tk, topk=topk),
        name="dsa_attention",
        grid=(b, nq),
        in_specs=[pl.BlockSpec((W_DSA, tq), colblk), pl.BlockSpec((W_IDX, tq), colblk),
                  pl.BlockSpec((LANES, tq), colblk), pl.BlockSpec((t_len, LANES), per_b),
                  pl.BlockSpec((t_len, W_DSA), per_b),
                  pl.BlockSpec((W_DSA, t_len), lambda bi, i: (0, bi))],
        out_specs=pl.BlockSpec((W_DSA, tq), colblk),
        out_shape=jax.ShapeDtypeStruct(qt.shape, BF16),
        scratch_shapes=[pltpu.VMEM((t_len, tq), jnp.int32), pltpu.VMEM((t_len, tq), BF16)],
        compiler_params=_cparams(("parallel", "arbitrary")),
    )(qt, iqt, misct, ikb, k, vt)


def _mem_kernel(q_ref, mkv_ref, o_ref, *, tq):
    lo, hi = _pair_masks(tq)
    outs = []
    for p in range(H_MEM // 2):
        q2 = q_ref[:, p * LANES:(p + 1) * LANES].astype(F32)
        mk = mkv_ref[0, :, p * LANES:(p + 1) * LANES].astype(BF16)
        mv = mkv_ref[0, :, W_MEM + p * LANES:W_MEM + (p + 1) * LANES].astype(BF16)

        def head(qh):
            s = _nt_dot(qh, mk)
            pr = jnp.exp(s - jnp.max(s, axis=-1, keepdims=True))
            l = jnp.sum(pr, axis=-1, keepdims=True)
            return jnp.dot(pr.astype(BF16), mv, preferred_element_type=F32) / l

        outs.append(head((q2 * lo).astype(BF16)) * lo + head((q2 * hi).astype(BF16)) * hi)
    o_ref[...] = jnp.concatenate(outs, axis=1).astype(BF16)


def _mem_attention(q, mkv, b, t_len, tq):
    nq = t_len // tq
    n_mem = mkv.shape[1]
    return pl.pallas_call(
        functools.partial(_mem_kernel, tq=tq),
        grid=(b, nq),
        in_specs=[pl.BlockSpec((tq, W_MEM), lambda bi, i: (bi * nq + i, 0)),
                  pl.BlockSpec((1, n_mem, 2 * W_MEM), lambda bi, i: (bi, 0, 0))],
        out_specs=pl.BlockSpec((tq, W_MEM), lambda bi, i: (bi * nq + i, 0)),
        out_shape=jax.ShapeDtypeStruct(q.shape, BF16),
        compiler_params=_cparams(("parallel", "arbitrary")),
    )(q, mkv)


def _rms(x, g):
    return x * lax.rsqrt(jnp.mean(x * x, axis=-1, keepdims=True) + EPS) * g


def _merge_kernel(x_ref, of_ref, od_ref, os_ref, om_ref, gpre_ref, gpost_ref, wg_ref,
                  wf_ref, wd_ref, ws_ref, wm_ref, wo_ref, y_ref, *, transposed):
    x = x_ref[...]
    d = x.shape[1]
    hb = _rms(x, gpre_ref[...]).astype(BF16)
    merged = None
    for i, (o_ref, w_ref) in enumerate(((of_ref, wf_ref), (od_ref, wd_ref),
                                        (os_ref, ws_ref), (om_ref, wm_ref))):
        gate = jax.nn.sigmoid(_nt_dot(hb, wg_ref[i * d:(i + 1) * d, :]))
        o = o_ref[...]
        if transposed and i < 2:
            o = o.astype(F32).T
        br = jnp.dot(o.astype(BF16), w_ref[...], preferred_element_type=F32)
        merged = gate * br if merged is None else merged + gate * br
    y = jnp.dot(merged.astype(BF16), wo_ref[...], preferred_element_type=F32)
    y_ref[...] = x + _rms(y, gpost_ref[...])


def _merge(x, o_fox, o_dsa, o_sb, o_mem, gpre, gpost, wg, wf, wd, ws, wm, wo, tm, transposed):
    n, d = x.shape
    row = lambda i: (i, 0)
    const = lambda i: (0, 0)
    full = lambda a: pl.BlockSpec(a.shape, const)

    def o_spec(o, can_t):
        if transposed and can_t:
            return pl.BlockSpec((o.shape[0], tm), lambda i: (0, i))
        return pl.BlockSpec((tm, o.shape[1]), row)

    return pl.pallas_call(
        functools.partial(_merge_kernel, transposed=transposed),
        name="branch_merge",
        grid=(n // tm,),
        in_specs=[pl.BlockSpec((tm, d), row), o_spec(o_fox, True), o_spec(o_dsa, True),
                  o_spec(o_sb, False), o_spec(o_mem, False)]
        + [full(a) for a in (gpre, gpost, wg, wf, wd, ws, wm, wo)],
        out_specs=pl.BlockSpec((tm, d), row),
        out_shape=jax.ShapeDtypeStruct((n, d), F32),
        compiler_params=_cparams(("parallel",)),
    )(x, o_fox, o_dsa, o_sb, o_mem, gpre, gpost, wg, wf, wd, ws, wm, wo)


def _ffn_kernel(x_ref, gpre_ref, gpost_ref, wi_ref, wo_ref, y_ref, *, d_ff, tc):
    x = x_ref[...]
    hb = _rms(x, gpre_ref[...]).astype(BF16)
    y = jnp.zeros(x.shape, F32)
    for c0 in range(0, d_ff, tc):
        gate = jnp.dot(hb, wi_ref[:, c0:c0 + tc], preferred_element_type=F32)
        up = jnp.dot(hb, wi_ref[:, d_ff + c0:d_ff + c0 + tc], preferred_element_type=F32)
        act = (gate * jax.nn.sigmoid(gate)) * up
        y = y + jnp.dot(act.astype(BF16), wo_ref[c0:c0 + tc, :], preferred_element_type=F32)
    y_ref[...] = x + _rms(y, gpost_ref[...])


def _ffn(x, gpre, gpost, wi, wo, tm):
    n, d = x.shape
    d_ff = wo.shape[0]
    tc = 256 if d_ff % 256 == 0 else d_ff
    row = lambda i: (i, 0)
    const = lambda i: (0, 0)
    return pl.pallas_call(
        functools.partial(_ffn_kernel, d_ff=d_ff, tc=tc),
        name="swiglu_ffn",
        grid=(n // tm,),
        in_specs=[pl.BlockSpec((tm, d), row), pl.BlockSpec(gpre.shape, const),
                  pl.BlockSpec(gpost.shape, const), pl.BlockSpec(wi.shape, const),
                  pl.BlockSpec(wo.shape, const)],
        out_specs=pl.BlockSpec((tm, d), row),
        out_shape=jax.ShapeDtypeStruct((n, d), F32),
        compiler_params=_cparams(("parallel",)),
    )(x, gpre, gpost, wi, wo)


def _dec_score_kernel(pt_ref, iqh_ref, iwb_ref, iknew_ref, *rest, npages):
    pages = rest[:npages]
    o_ref = rest[npages]
    iqh = iqh_ref[0]
    iwb = iwb_ref[0]
    for j in range(npages):
        s = jnp.dot(iqh, pages[j][0, 0].astype(BF16), preferred_element_type=F32)
        o_ref[0, :, j * LANES:(j + 1) * LANES] = jnp.sum(iwb * jnp.maximum(s, 0.0), axis=0,
                                                         keepdims=True)
    ik_new = iknew_ref[0][:, :IDX_DIM].astype(BF16).astype(F32)
    s_new = jnp.sum(iqh.astype(F32) * ik_new, axis=-1, keepdims=True)
    isc_new = jnp.sum(iwb[:, 0:1] * jnp.maximum(s_new, 0.0), axis=0, keepdims=True)
    lane = lax.broadcasted_iota(jnp.int32, (1, LANES), 1)
    o_ref[0, :, npages * LANES:] = jnp.where(lane == 0, isc_new, NEG_INF)


def _dec_scores(pt_flat, iqh, iwb, ik_new, idxk_pool, layer, nb, npages):
    width = (npages + 1) * LANES
    seq = lambda bi, pt: (bi, 0, 0)

    def page_map(j):
        return lambda bi, pt: (layer, pt[bi * npages + j], 0, 0)

    grid_spec = pltpu.PrefetchScalarGridSpec(
        num_scalar_prefetch=1, grid=(nb,),
        in_specs=[pl.BlockSpec((1, 8, IDX_DIM), seq), pl.BlockSpec((1, 8, LANES), seq),
                  pl.BlockSpec((1, 1, LANES), seq)]
        + [pl.BlockSpec((1, 1, IDX_DIM, PAGE_SIZE), page_map(j)) for j in range(npages)],
        out_specs=pl.BlockSpec((1, 1, width), seq))
    return pl.pallas_call(
        functools.partial(_dec_score_kernel, npages=npages),
        name="decode_index_scores",
        grid_spec=grid_spec,
        out_shape=jax.ShapeDtypeStruct((nb, 1, width), F32),
        compiler_params=_cparams(("arbitrary",)),
    )(pt_flat, iqh, iwb, ik_new, *([idxk_pool] * npages))


def _dec_select_kernel(isc_ref, o_ref, key_ref, *, topk):
    rows, width = isc_ref.shape
    key_ref[...] = _sort_key(isc_ref[...])
    _select_topk_bias(key_ref, width // LANES, LANES, rows, topk, keys_on_lanes=True)
    o_ref[...] = pltpu.bitcast(key_ref[...], F32)


def _dec_select(isc, topk):
    return pl.pallas_call(
        functools.partial(_dec_select_kernel, topk=topk),
        out_shape=jax.ShapeDtypeStruct(isc.shape, F32),
        scratch_shapes=[pltpu.VMEM(isc.shape, jnp.int32)],
        compiler_params=pltpu.CompilerParams(vmem_limit_bytes=VMEM_LIMIT),
    )(isc)


def _head_rows(width):
    sub = lax.broadcasted_iota(jnp.int32, (8, width), 0)
    lane = lax.broadcasted_iota(jnp.int32, (8, width), 1)
    return ((lane >> 6) == sub).astype(F32)


def _dec_attn_kernel(pt_ref, fq_ref, dq_ref, sq_ref, mq_ref, fnew_ref, dnew_ref, lnew_ref,
                     bias_ref, mem_ref, *rest, npages):
    fox_pages = rest[0:npages]
    logf_pages = rest[npages:2 * npages]
    dsa_pages = rest[2 * npages:3 * npages]
    sb_pages = rest[3 * npages:4 * npages]
    of_ref, od_ref, os_ref, om_ref = rest[4 * npages:]

    row = lax.broadcasted_iota(jnp.int32, (PAGE_SIZE, PAGE_SIZE), 0)
    col = lax.broadcasted_iota(jnp.int32, (PAGE_SIZE, PAGE_SIZE), 1)
    after = (row > col).astype(BF16)

    past = npages * PAGE_SIZE

    def keys_t(pages):
        return jnp.concatenate([pg[0, 0, 0].astype(BF16) for pg in pages], axis=1)

    def values_t(pages):
        return jnp.concatenate([pg[0, 0, 1].astype(BF16) for pg in pages], axis=1)

    def page_rows(x):
        return jnp.concatenate([x[:, j * PAGE_SIZE:(j + 1) * PAGE_SIZE] for j in range(npages)],
                               axis=0)

    def suffix_sums(x_rows, run):
        h1, h2, h3 = _split3(x_rows)
        inner = (jnp.dot(h1, after, preferred_element_type=F32)
                 + jnp.dot(h2, after, preferred_element_type=F32)
                 + jnp.dot(h3, after, preferred_element_type=F32))
        tot = jnp.sum(x_rows, axis=-1, keepdims=True)
        outs = [None] * npages
        for j in range(npages - 1, -1, -1):
            outs[j] = inner[j * 8:(j + 1) * 8] + run
            run = run + tot[j * 8:(j + 1) * 8]
        return jnp.concatenate(outs, axis=1)

    def head_diag(o, mask):
        return jnp.sum(o * mask, axis=0, keepdims=True)

    def softmax_pv(s_past, s_new, vt_all, v_new):
        m = jnp.maximum(jnp.max(s_past, axis=-1, keepdims=True), s_new)
        p_new = jnp.exp(s_new - m)
        pr = jnp.exp(s_past - m)
        l = p_new + jnp.sum(pr, axis=-1, keepdims=True)
        acc = p_new.astype(BF16).astype(F32) * v_new + _nt_dot(pr.astype(BF16), vt_all)
        return acc / l

    mask_f = _head_rows(W_FOX)
    qf = (fq_ref[0].astype(F32) * mask_f).astype(BF16)
    k_new = fnew_ref[0][:, :W_FOX].astype(BF16).astype(F32)
    v_new = fnew_ref[0][:, W_FOX:].astype(BF16).astype(F32)
    s_new = jnp.sum(qf.astype(F32) * k_new, axis=-1, keepdims=True)
    lf_rows = jnp.concatenate([pg[0, 0] for pg in logf_pages], axis=0)
    decay = suffix_sums(lf_rows, lnew_ref[0][:, 0:1])
    s_past = jnp.dot(qf, keys_t(fox_pages), preferred_element_type=F32) + decay
    of_ref[0] = head_diag(softmax_pv(s_past, s_new, values_t(fox_pages), v_new), mask_f)

    mask_d = _head_rows(W_DSA)
    qd = (dq_ref[0].astype(F32) * mask_d).astype(BF16)
    k_new = dnew_ref[0][:, :W_DSA].astype(BF16).astype(F32)
    v_new = dnew_ref[0][:, W_DSA:].astype(BF16).astype(F32)
    s_new = (jnp.sum(qd.astype(F32) * k_new, axis=-1, keepdims=True)
             + bias_ref[0][:, past:past + 1])
    s_past = jnp.dot(qd, keys_t(dsa_pages), preferred_element_type=F32) + bias_ref[0][:, :past]
    od_ref[0] = head_diag(softmax_pv(s_past, s_new, values_t(dsa_pages), v_new), mask_d)

    mask_s = _head_rows(W_SB)
    qs = (sq_ref[0].astype(F32) * mask_s).astype(BF16)
    z = jnp.dot(qs, keys_t(sb_pages), preferred_element_type=F32)
    tl = _softplus_tail(z)
    log_beta = jnp.minimum(z, 0.0) - tl
    log_1mb = -jnp.maximum(z, 0.0) - tl
    tail = suffix_sums(page_rows(log_1mb), jnp.zeros((8, 1), F32))
    a = jnp.exp(log_beta + tail)
    os_ref[0] = head_diag(_nt_dot(a.astype(BF16), values_t(sb_pages)), mask_s)

    mask_m = _head_rows(W_MEM)
    qm = (mq_ref[0].astype(F32) * mask_m).astype(BF16)
    s = jnp.dot(qm, mem_ref[0, 0, 0].astype(BF16), preferred_element_type=F32)
    pr = jnp.exp(s - jnp.max(s, axis=-1, keepdims=True))
    l = jnp.sum(pr, axis=-1, keepdims=True)
    o = _nt_dot(pr.astype(BF16), mem_ref[0, 0, 1].astype(BF16))
    om_ref[0] = head_diag(o / l, mask_m)


def _dec_attention(pt_flat, fq, dq, sq, mq, fnew, dnew, lnew, bias, mem, fox_pool, logf_pool,
                   dsa_pool, sb_pool, layer, nb, npages):
    seq = lambda bi, pt: (bi, 0, 0)

    def page_map(j, nd):
        return lambda bi, pt: (layer, pt[bi * npages + j]) + (0,) * nd

    def pages(pool):
        blk = (1, 1) + pool.shape[2:]
        return [pl.BlockSpec(blk, page_map(j, len(blk) - 2)) for j in range(npages)]

    def seq_spec(a):
        return pl.BlockSpec((1,) + a.shape[1:], seq)

    mem_spec = pl.BlockSpec((1, 1) + mem.shape[2:], lambda bi, pt: (layer, bi, 0, 0, 0))
    grid_spec = pltpu.PrefetchScalarGridSpec(
        num_scalar_prefetch=1, grid=(nb,),
        in_specs=[seq_spec(a) for a in (fq, dq, sq, mq, fnew, dnew, lnew, bias)] + [mem_spec]
        + pages(fox_pool) + pages(logf_pool) + pages(dsa_pool) + pages(sb_pool),
        out_specs=[pl.BlockSpec((1, 1, wd), seq) for wd in (W_FOX, W_DSA, W_SB, W_MEM)])
    return pl.pallas_call(
        functools.partial(_dec_attn_kernel, npages=npages),
        name="decode_attention",
        grid_spec=grid_spec,
        out_shape=[jax.ShapeDtypeStruct((nb, 1, wd), F32) for wd in (W_FOX, W_DSA, W_SB, W_MEM)],
        compiler_params=_cparams(("arbitrary",)),
    )(pt_flat, fq, dq, sq, mq, fnew, dnew, lnew, bias, mem,
      *([fox_pool] * npages), *([logf_pool] * npages), *([dsa_pool] * npages),
      *([sb_pool] * npages))


def _rope_tables(pos):
    rd = HEAD_DIM // 4
    half = rd // 2
    inv_freq = ROPE_THETA ** (-jnp.arange(half, dtype=F32) * 2.0 / rd)
    ang = pos.astype(F32)[:, None] * inv_freq[None, :]
    cos, sin = jnp.cos(ang), jnp.sin(ang)
    n = pos.shape[0]
    one = jnp.ones((n, HEAD_DIM - rd), F32)
    zero = jnp.zeros((n, HEAD_DIM - rd), F32)
    zh = jnp.zeros((n, half), F32)
    cos64 = jnp.concatenate([cos, cos, one], axis=1)
    sa64 = jnp.concatenate([-sin, zh, zero], axis=1)
    sb64 = jnp.concatenate([zh, sin, zero], axis=1)
    dup = lambda a: jnp.concatenate([a, a], axis=1)
    return dup(cos64), dup(sa64), dup(sb64), cos.T, sin.T


def _prep_w_in(w_in_l, b_forget_l, d_model):
    offs = np.cumsum([0, W_FOX, W_FOX, W_FOX, H_FOX, W_DSA, W_DSA, W_DSA, W_IDX, IDX_DIM,
                      IDX_HEADS, W_SB, W_SB, W_SB, W_MEM])
    (o_fq, o_fk, o_fv, o_ff, o_dq, o_dk, o_dv, o_iq, o_ik, o_iw, o_sq, o_sk, o_sv, o_mq,
     o_g) = [int(v) for v in offs]
    wt = jnp.transpose(w_in_l)
    sl = lambda o, n: wt[o:o + n]
    ik = sl(o_ik, IDX_DIM)
    misc = jnp.concatenate([sl(o_ff, H_FOX), sl(o_iw, IDX_HEADS),
                            jnp.zeros((LANES - H_FOX - IDX_HEADS, d_model), F32)], axis=0)
    wp = jnp.concatenate([sl(o_fq, 3 * W_FOX), sl(o_dq, 3 * W_DSA), sl(o_iq, W_IDX),
                          sl(o_sq, 3 * W_SB), sl(o_mq, W_MEM), ik, ik, misc], axis=0)
    wg = wt[o_g:]
    bfp = jnp.concatenate([b_forget_l, jnp.zeros((LANES - H_FOX,), F32)])[None, :]
    return wp.astype(BF16), wg.astype(BF16), bfp


def kernel(x_prompt, x_sample, cache_fox_kv, cache_fox_logf, cache_dsa_kv, cache_dsa_idxk,
           cache_sb_kv, cache_mem_kv, page_table, mem_prompt, w_in, b_forget, w_mem_kv,
           w_br_fox, w_br_dsa, w_br_sb, w_br_mem, w_out, w_ffn_in, w_ffn_out,
           g_mix_pre, g_mix_post, g_ffn_pre, g_ffn_post):
    bp, t_len, d = x_prompt.shape
    nb = x_sample.shape[0]
    depth = w_in.shape[0]
    npages = page_table.shape[1]
    past_len = npages * PAGE_SIZE
    n_pool = cache_fox_kv.shape[1]
    n_mem = mem_prompt.shape[1]
    n_p = bp * t_len

    tm = min(256, t_len)
    t_att = min(256, t_len)
    tq_dsa = min(256, t_len)
    tk_dsa = min(512, t_len)
    topk_p = min(DSA_TOPK_MAX, t_len // 4)
    topk_s = min(DSA_TOPK_MAX, (past_len + 1) // 4)
    assert tk_dsa >= topk_p and t_len % tk_dsa == 0 and t_len % tm == 0

    tabs_p = _rope_tables(jnp.arange(t_len, dtype=jnp.int32))
    tabs_s = _rope_tables(jnp.full((nb,), past_len, jnp.int32))
    pt_flat = page_table.reshape(-1).astype(jnp.int32)

    def kv_view(cache, width):
        view = jnp.transpose(cache, (0, 1, 3, 4, 5, 2))
        return view.reshape(cache.shape[:2] + (2, width, cache.shape[2]))

    fox_t = kv_view(cache_fox_kv, W_FOX)
    dsa_t = kv_view(cache_dsa_kv, W_DSA)
    sb_t = kv_view(cache_sb_kv, W_SB)
    mem_t = kv_view(cache_mem_kv, W_MEM)
    idxk_t = jnp.transpose(cache_dsa_idxk, (0, 1, 3, 2))
    logf_t = jnp.pad(jnp.transpose(cache_fox_logf, (0, 1, 3, 2)),
                     ((0, 0), (0, 0), (0, 8 - H_FOX), (0, 0)))

    xp = x_prompt.reshape(n_p, d)
    xs = x_sample.reshape(nb, d)
    mem_flat = mem_prompt.reshape(bp * n_mem, d)

    rows_p, rows_s, mem_p = [], [], []
    stacked_p = None
    for l in range(depth):
        wp, wg, bfp = _prep_w_in(w_in[l], b_forget[l], d)
        gpre, gpost = g_mix_pre[l][None, :], g_mix_post[l][None, :]
        fpre, fpost = g_ffn_pre[l][None, :], g_ffn_post[l][None, :]
        wf, wd_, ws, wm = (w.astype(BF16) for w in (w_br_fox[l], w_br_dsa[l], w_br_sb[l], w_br_mem[l]))
        wo = w_out[l].astype(BF16)
        wi, wfo = w_ffn_in[l].astype(BF16), w_ffn_out[l].astype(BF16)

        mem_kv = _matmul(mem_flat, w_mem_kv[l].astype(BF16))
        (fqt, fkv_st, fk, fvt, dqt, dkv_st, dk, dvt, iqt, sq, skv_st, sk, sv, mq, ik_st, ikb,
         misc, misct) = _proj(xp, gpre, wp, tabs_p, bfp, tm, True, stacked_p, l, depth)
        stacked_p = (fkv_st, dkv_st, skv_st, ik_st)
        logf = misc[:, :H_FOX].reshape(bp, t_len, H_FOX)
        c = jnp.cumsum(logf, axis=1) * LOG2_E
        fqt_full, fk_full = _fox_operands(fqt, fk, c.reshape(n_p, H_FOX))
        o_fox = _fox_attention(fqt_full, fk_full, fvt, bp, t_len, t_att, tk_dsa)
        o_dsa = _dsa_attention(dqt, iqt, misct, ikb, dk, dvt, bp, t_len, tq_dsa, tk_dsa, topk_p)
        o_sb = _sb_attention(sq, sk, sv, bp, t_len, t_att)
        o_mem = _mem_attention(mq, mem_kv.reshape(bp, n_mem, 2 * W_MEM), bp, t_len, tm)
        xp = _merge(xp, o_fox, o_dsa, o_sb, o_mem, gpre, gpost, wg, wf, wd_, ws, wm, wo, tm, True)
        xp = _ffn(xp, fpre, fpost, wi, wfo, tm)
        rows_p.append(logf)
        mem_p.append(mem_kv.reshape(bp, n_mem, 2, H_MEM, HEAD_DIM))

        (fq, fkv, fk, fv, dq, dkv, dk, dv, iq, sq, skv, sk, sv, mq, ik32, ikb,
         misc) = _proj(xs, gpre, wp, tabs_s, bfp, nb, False)
        r3 = lambda a: a.reshape(nb, 1, a.shape[1])
        iqh = jnp.pad(iq.reshape(nb, IDX_HEADS, IDX_DIM), ((0, 0), (0, 8 - IDX_HEADS), (0, 0)))
        iwb = jnp.pad(misc[:, MISC_IW:MISC_IW + IDX_HEADS], ((0, 0), (0, 8 - IDX_HEADS)))
        iwb = jnp.broadcast_to(iwb[:, :, None], (nb, 8, LANES))
        isc = _dec_scores(pt_flat, iqh, iwb, r3(ik32), idxk_t, l, nb, npages)
        bias = _dec_select(isc.reshape(nb, -1), topk_s).reshape(nb, 1, -1)
        lnew = jnp.pad(misc[:, :H_FOX], ((0, 0), (0, 8 - H_FOX)))
        lnew = jnp.broadcast_to(lnew[:, :, None], (nb, 8, LANES))
        o_fox, o_dsa, o_sb, o_mem = _dec_attention(
            pt_flat, r3(fq), r3(dq), r3(sq), r3(mq), r3(fkv), r3(dkv), lnew, bias,
            mem_t, fox_t, logf_t, dsa_t, sb_t, l, nb, npages)
        sq2 = lambda a: a.reshape(nb, a.shape[2])
        xs = _merge(xs, sq2(o_fox), sq2(o_dsa), sq2(o_sb), sq2(o_mem), gpre, gpost, wg, wf, wd_,
                    ws, wm, wo, nb, False)
        xs = _ffn(xs, fpre, fpost, wi, wfo, nb)
        rows_s.append((fkv.reshape(nb, 1, 2, H_FOX, HEAD_DIM), misc[:, :H_FOX].reshape(nb, 1, H_FOX),
                       dkv.reshape(nb, 1, 2, H_DSA, HEAD_DIM),
                       ik32[:, :IDX_DIM].reshape(nb, 1, IDX_DIM),
                       skv.reshape(nb, 1, 2, H_SB, HEAD_DIM)))

    stk = lambda rows, i: jnp.stack([r[i] for r in rows], axis=0)
    fkv_st, dkv_st, skv_st, ik_st = stacked_p

    def kv_rows(buf, heads):
        return jnp.transpose(buf.reshape(depth, bp, 2, heads, HEAD_DIM, t_len), (0, 1, 5, 2, 3, 4))

    return (xp.reshape(bp, t_len, d), xs.reshape(nb, 1, d),
            kv_rows(fkv_st, H_FOX), jnp.stack(rows_p, axis=0), kv_rows(dkv_st, H_DSA),
            jnp.transpose(ik_st, (0, 1, 3, 2)), kv_rows(skv_st, H_SB),
            jnp.stack(mem_p, axis=0),
            stk(rows_s, 0), stk(rows_s, 1), stk(rows_s, 2), stk(rows_s, 3), stk(rows_s, 4))
```

```python
import functools

import jax
import jax.numpy as jnp
import numpy as np
from jax import lax
from jax.experimental import pallas as pl
from jax.experimental.pallas import tpu as pltpu

HEAD_DIM = 64
H_FOX = 6
H_DSA = 6
H_SB = 4
H_MEM = 4
IDX_HEADS = 4
IDX_DIM = 64
DSA_TOPK_MAX = 256
ROPE_THETA = 500000.0
N_BRANCH = 4
EPS = 1e-6
PAGE_SIZE = 128

LANES = 128
VMEM_LIMIT = 56 * 1024 * 1024

F32 = jnp.float32
BF16 = jnp.bfloat16
NEG_INF = float("-inf")
INT_MIN = -2 ** 31
SB_DEAD = -120.0
LOG2_E = 1.4426950408889634
MIN_NORMAL = 2.0 ** -126

W_FOX = H_FOX * HEAD_DIM
W_DSA = H_DSA * HEAD_DIM
W_SB = H_SB * HEAD_DIM
W_MEM = H_MEM * HEAD_DIM
W_IDX = IDX_HEADS * IDX_DIM

C_FQ, C_FKV = 0, W_FOX
C_DQ = C_FKV + 2 * W_FOX
C_DK = C_DQ + W_DSA
C_DV = C_DK + W_DSA
C_IQ = C_DV + W_DSA
C_SQ = C_IQ + W_IDX
C_SKV = C_SQ + W_SB
C_MQ = C_SKV + 2 * W_SB
C_IK = C_MQ + W_MEM
C_MISC = C_IK + LANES
N_PROJ = C_MISC + LANES
MISC_IW = H_FOX


def _cparams(sem):
    return pltpu.CompilerParams(dimension_semantics=sem, vmem_limit_bytes=VMEM_LIMIT)


def _nt_dot(a, b):
    return lax.dot_general(a, b, (((1,), (1,)), ((), ())), preferred_element_type=F32)


def _softplus_tail(z):
    return jnp.log1p(jnp.exp(-jnp.abs(z)))


def _split3(x):
    hi = x.astype(BF16)
    r1 = x - hi.astype(F32)
    mid = r1.astype(BF16)
    lo = (r1 - mid.astype(F32)).astype(BF16)
    return hi, mid, lo


def _proj_kernel(x_ref, g_ref, w_ref, cos_ref, sa_ref, sb_ref, bf_ref, cost_ref, sint_ref,
                 *refs, transposed, n_alias):
    (fq_ref, fkv_ref, fk_ref, fv_ref, dq_ref, dkv_ref, dk_ref, dv_ref, iq_ref, sq_ref, skv_ref,
     sk_ref, sv_ref, mq_ref, ik32_ref, ikb_ref, misc_ref, *extra) = refs[n_alias:]
    x = x_ref[...]
    h = x * lax.rsqrt(jnp.mean(x * x, axis=-1, keepdims=True) + EPS)
    hb = (h * g_ref[...]).astype(BF16)
    cosf, sa, sb = cos_ref[...], sa_ref[...], sb_ref[...]
    scale = HEAD_DIM ** -0.5

    def mm(c0, n):
        return _nt_dot(hb, w_ref[c0:c0 + n, :])

    def mm_t(c0, n):
        return _nt_dot(w_ref[c0:c0 + n, :], hb)

    def rope(z):
        outs = []
        for j in range(z.shape[1] // LANES):
            zj = z[:, j * LANES:(j + 1) * LANES]
            outs.append(zj * cosf + pltpu.roll(zj, LANES - 8, 1) * sa + pltpu.roll(zj, 8, 1) * sb)
        return outs[0] if len(outs) == 1 else jnp.concatenate(outs, axis=1)

    def rope_t(zt):
        cos_t, sin_t = cost_ref[...], sint_ref[...]
        half = HEAD_DIM // 8
        parts = []
        for hh in range(zt.shape[0] // HEAD_DIM):
            base = hh * HEAD_DIM
            x1, x2 = zt[base:base + half], zt[base + half:base + 2 * half]
            parts += [x1 * cos_t - x2 * sin_t, x2 * cos_t + x1 * sin_t,
                      zt[base + 2 * half:base + HEAD_DIM]]
        return jnp.concatenate(parts, axis=0)

    dk = rope(mm(C_DK, W_DSA))
    dk_ref[...] = dk.astype(BF16)
    skv = mm(C_SKV, 2 * W_SB)
    ik = rope(mm(C_IK, LANES))
    if transposed:
        scale2 = scale * LOG2_E
        fq_ref[...] = (mm_t(C_FQ, W_FOX) * scale2).astype(BF16)
        fkv_t = mm_t(C_FKV, 2 * W_FOX)
        fkv_ref[0, 0] = fkv_t
        fv_ref[...] = fkv_t[W_FOX:].astype(BF16)
        fk_ref[...] = mm(C_FKV, W_FOX).astype(BF16)
        dq_ref[...] = (rope_t(mm_t(C_DQ, W_DSA)) * scale2).astype(BF16)
        dv_t = mm_t(C_DV, W_DSA)
        dkv_ref[0, 0, :W_DSA] = rope_t(mm_t(C_DK, W_DSA))
        dkv_ref[0, 0, W_DSA:] = dv_t
        dv_ref[...] = dv_t.astype(BF16)
        iq_ref[...] = rope_t(mm_t(C_IQ, W_IDX)).astype(BF16)
        skv_ref[0, 0] = mm_t(C_SKV, 2 * W_SB)
        ik32_ref[0, 0] = rope_t(mm_t(C_IK, LANES))[:IDX_DIM]
        extra[0][...] = mm_t(C_MISC, LANES)
    else:
        fkv = mm(C_FKV, 2 * W_FOX)
        fkv_ref[...] = fkv
        fk_ref[...] = fkv[:, :W_FOX].astype(BF16)
        fq_ref[...] = (mm(C_FQ, W_FOX) * scale).astype(BF16)
        fv_ref[...] = fkv[:, W_FOX:].astype(BF16)
        dq_ref[...] = (rope(mm(C_DQ, W_DSA)) * scale).astype(BF16)
        dv = mm(C_DV, W_DSA)
        dkv_ref[:, :W_DSA] = dk
        dkv_ref[:, W_DSA:] = dv
        dv_ref[...] = dv.astype(BF16)
        iq_ref[...] = rope(mm(C_IQ, W_IDX)).astype(BF16)
        skv_ref[...] = skv
        ik32_ref[...] = ik

    sq_ref[...] = (mm(C_SQ, W_SB) * scale).astype(BF16)
    sk_ref[...] = skv[:, :W_SB].astype(BF16)
    sv_ref[...] = skv[:, W_SB:].astype(BF16)
    mq_ref[...] = (mm(C_MQ, W_MEM) * scale).astype(BF16)
    ikb_ref[...] = ik.astype(BF16)

    zm = mm(C_MISC, LANES)
    ff = zm + bf_ref[...]
    logf = -(jnp.maximum(-ff, 0.0) + _softplus_tail(ff))
    lane = lax.broadcasted_iota(jnp.int32, zm.shape, 1)
    misc_ref[...] = jnp.where(lane < H_FOX, logf, zm)


def _proj(x, g, w, tabs, bfp, tm, transposed, stacked=None, layer=0, depth=1):
    n, d = x.shape
    cosf, sa, sb, cos_t, sin_t = tabs
    nt = cosf.shape[0] // tm
    row = lambda i: (i, 0)
    col = lambda i: (0, i)
    tab = lambda i: (i % nt, 0)
    tab_t = lambda i: (0, i % nt)
    const = lambda i: (0, 0)
    stack_blk = lambda i: (layer, i // nt, 0, i % nt)
    widths = [(W_FOX, BF16, 't'), (2 * W_FOX, F32, 's'), (W_FOX, BF16, ''), (W_FOX, BF16, 't'),
              (W_DSA, BF16, 't'), (2 * W_DSA, F32, 's'), (W_DSA, BF16, ''), (W_DSA, BF16, 't'),
              (W_IDX, BF16, 't'), (W_SB, BF16, ''), (2 * W_SB, F32, 's'), (W_SB, BF16, ''),
              (W_SB, BF16, ''), (W_MEM, BF16, ''), (IDX_DIM if transposed else LANES, F32, 's'),
              (LANES, BF16, ''), (LANES, F32, '')]
    if transposed:
        widths.append((LANES, F32, 't'))
    out_specs, out_shape, stacked_idx = [], [], []
    for idx, (wd, dt, kind) in enumerate(widths):
        if transposed and kind == 't':
            out_specs.append(pl.BlockSpec((wd, tm), col))
            out_shape.append(jax.ShapeDtypeStruct((wd, n), dt))
        elif transposed and kind == 's':
            stacked_idx.append(idx)
            out_specs.append(pl.BlockSpec((1, 1, wd, tm), stack_blk))
            out_shape.append(jax.ShapeDtypeStruct((depth, n // (nt * tm), wd, nt * tm), dt))
        else:
            out_specs.append(pl.BlockSpec((tm, wd), row))
            out_shape.append(jax.ShapeDtypeStruct((n, wd), dt))
    in_specs = [pl.BlockSpec((tm, d), row), pl.BlockSpec((1, d), const),
                pl.BlockSpec((N_PROJ, d), const),
                pl.BlockSpec((tm, LANES), tab), pl.BlockSpec((tm, LANES), tab),
                pl.BlockSpec((tm, LANES), tab), pl.BlockSpec((1, LANES), const),
                pl.BlockSpec((8, tm), tab_t), pl.BlockSpec((8, tm), tab_t)]
    operands = [x, g, w, cosf, sa, sb, bfp, cos_t, sin_t]
    aliases = {}
    if stacked is not None:
        for k, buf in enumerate(stacked):
            aliases[len(operands)] = stacked_idx[k]
            in_specs.append(pl.BlockSpec(memory_space=pl.ANY))
            operands.append(buf)
    n_alias = len(aliases)
    return pl.pallas_call(
        functools.partial(_proj_kernel, transposed=transposed, n_alias=n_alias),
        name="input_projection",
        grid=(n // tm,),
        in_specs=in_specs,
        out_specs=out_specs,
        out_shape=out_shape,
        input_output_aliases=aliases,
        compiler_params=_cparams(("parallel",)),
    )(*operands)


def _matmul_kernel(a_ref, b_ref, o_ref):
    o_ref[...] = jnp.dot(a_ref[...].astype(BF16), b_ref[...], preferred_element_type=F32)


def _matmul(a, b):
    return pl.pallas_call(
        _matmul_kernel,
        out_shape=jax.ShapeDtypeStruct((a.shape[0], b.shape[1]), F32),
        compiler_params=pltpu.CompilerParams(vmem_limit_bytes=VMEM_LIMIT),
    )(a, b)


def _pair_masks(rows):
    lane = lax.broadcasted_iota(jnp.int32, (rows, LANES), 1)
    lo = (lane < HEAD_DIM).astype(F32)
    return lo, 1.0 - lo


def _fold_lanes(x):
    part = x[:, 0:LANES]
    for u in range(1, x.shape[1] // LANES):
        part = part + x[:, u * LANES:(u + 1) * LANES]
    return part


def _pair_masks_t(cols):
    sub = lax.broadcasted_iota(jnp.int32, (LANES, cols), 0)
    lo = (sub < HEAD_DIM).astype(F32)
    return lo, 1.0 - lo


def _tree_rows(x, op, nacc=4):
    parts = [x[i * 8:(i + 1) * 8] for i in range(x.shape[0] // 8)]
    accs = parts[:nacc]
    for i, part in enumerate(parts[nacc:]):
        accs[i % nacc] = op(accs[i % nacc], part)
    while len(accs) > 1:
        accs = [op(a, b) for a, b in zip(accs[0::2], accs[1::2])] + (accs[-1:] if len(accs) % 2 else [])
    return accs[0]


def _fold_rows(x):
    return _tree_rows(x, jnp.add)


def _max_rows(x):
    return _tree_rows(x, jnp.maximum)


def _flash_t_update(st, pmax, m, acc, vt1, guard):
    m_new = jnp.maximum(m, jnp.max(pmax, axis=0, keepdims=True))
    m_use = jnp.where(m_new == NEG_INF, 0.0, m_new) if guard else m_new
    alpha = jnp.exp2(m - m_use)
    p = jnp.exp2(st - m_use)
    acc = alpha * acc + jnp.dot(vt1, p.astype(BF16), preferred_element_type=F32)
    return m_new, acc


def _flash_t_init(cols):
    return jnp.full((1, cols), NEG_INF, F32), jnp.zeros((LANES, cols), F32)


def _flash_t_pair_out(acca, accb):
    return jnp.concatenate([acca[:HEAD_DIM] / acca[HEAD_DIM:HEAD_DIM + 1],
                            accb[HEAD_DIM:] / accb[0:1]], axis=0)


def _values_with_ones(vt, lo_b, hi_b):
    return vt * lo_b + hi_b, vt * hi_b + lo_b


FOX_AUG = 6


def _fox_kernel(qt_ref, k_ref, vt_ref, o_ref, *, tq, tk):
    qi = pl.program_id(2)
    q0 = pl.multiple_of(qi * tq, tq)
    jd = q0 // tk
    qt = qt_ref[...].astype(F32)
    sub = lax.broadcasted_iota(jnp.int32, (2 * LANES, tq), 0)
    aug = sub - LANES
    lo = jnp.logical_or(sub < HEAD_DIM, jnp.logical_and(aug >= 0, aug < FOX_AUG))
    hi = jnp.logical_or(jnp.logical_and(sub >= HEAD_DIM, sub < LANES),
                        jnp.logical_and(aug >= FOX_AUG, aug < 2 * FOX_AUG))
    qta = jnp.where(lo, qt, 0.0).astype(BF16)
    qtb = jnp.where(hi, qt, 0.0).astype(BF16)
    key_in = lax.broadcasted_iota(jnp.int32, (tk, tq), 0)
    q_pos = q0 + lax.broadcasted_iota(jnp.int32, (tk, tq), 1)

    def scores(j):
        k0 = pl.multiple_of(j * tk, tk)
        kc = k_ref[pl.ds(k0, tk), :]
        sa = jnp.dot(kc, qta, preferred_element_type=F32)
        sb = jnp.dot(kc, qtb, preferred_element_type=F32)
        return (sa, _max_rows(sa)), (sb, _max_rows(sb))

    lo_b = (lax.broadcasted_iota(jnp.int32, (LANES, tk), 0) < HEAD_DIM).astype(F32).astype(BF16)
    hi_b = (1.0 - lo_b.astype(F32)).astype(BF16)

    def consume(s, j, carry):
        ca, cb = carry
        vta, vtb = _values_with_ones(vt_ref[:, pl.ds(pl.multiple_of(j * tk, tk), tk)], lo_b, hi_b)
        return _flash_t_update(*s[0], *ca, vta, False), _flash_t_update(*s[1], *cb, vtb, False)

    def trip(j, state):
        s, carry = state
        s_next = scores(j + 1)
        return s_next, consume(s, j, carry)

    s, carry = lax.fori_loop(0, jd, trip, (scores(0), (_flash_t_init(tq), _flash_t_init(tq))))
    causal = jd * tk + key_in <= q_pos
    s = tuple(jnp.where(causal, sh, NEG_INF) for sh, _ in s)
    (_, acca), (_, accb) = consume(tuple((sh, _max_rows(sh)) for sh in s), jd, carry)
    o_ref[...] = _flash_t_pair_out(acca, accb).astype(BF16)


def _fox_operands(qt, k, c):
    n = k.shape[0]
    npair = H_FOX // 2

    def chop(v):
        bits = lax.bitcast_convert_type(v, jnp.uint32) & jnp.uint32(0xFFFF0000)
        return lax.bitcast_convert_type(bits, F32)

    def split(v):
        hi = chop(v)
        mid = chop(v - hi)
        lo = chop(v - hi - mid)
        return jnp.stack([hi, mid, lo], axis=-1).astype(BF16)

    pos = split(c)
    neg = split(-c)
    ones = jnp.ones((n, H_FOX, 3), BF16)
    pad = jnp.zeros((n, npair, LANES - 2 * FOX_AUG), BF16)
    k_aug = jnp.concatenate([neg, ones], axis=-1).reshape(n, npair, 2 * FOX_AUG)
    q_aug = jnp.concatenate([ones, pos], axis=-1).reshape(n, npair, 2 * FOX_AUG)
    k_full = jnp.concatenate([k.reshape(n, npair, LANES), k_aug, pad], axis=-1)
    q_aug = jnp.transpose(jnp.concatenate([q_aug, pad], axis=-1), (1, 2, 0))
    qt_full = jnp.concatenate([qt.reshape(npair, LANES, n), q_aug], axis=1)
    return qt_full.reshape(npair * 2 * LANES, n), k_full.reshape(n, npair * 2 * LANES)


def _fox_attention(qt, k, vt, b, t_len, tq, tk):
    npair = H_FOX // 2
    nq = t_len // tq
    return pl.pallas_call(
        functools.partial(_fox_kernel, tq=tq, tk=tk),
        name="fox_attention",
        grid=(b, npair, nq),
        in_specs=[pl.BlockSpec((2 * LANES, tq), lambda bi, p, i: (p, bi * nq + i)),
                  pl.BlockSpec((t_len, 2 * LANES), lambda bi, p, i: (bi, p)),
                  pl.BlockSpec((LANES, t_len), lambda bi, p, i: (p, bi))],
        out_specs=pl.BlockSpec((LANES, tq), lambda bi, p, i: (p, bi * nq + i)),
        out_shape=jax.ShapeDtypeStruct(vt.shape, BF16),
        compiler_params=_cparams(("parallel", "parallel", "arbitrary")),
    )(qt, k, vt)


def _sb_kernel(q_ref, k_ref, v_ref, o_ref, *, t):
    qi = pl.program_id(1)
    lo, hi = _pair_masks(t)
    row = lax.broadcasted_iota(jnp.int32, (t, t), 0)
    col = lax.broadcasted_iota(jnp.int32, (t, t), 1)
    strict = col < row
    after = (row > col).astype(BF16)

    qs = []
    for p in range(H_SB // 2):
        q2 = q_ref[:, p * LANES:(p + 1) * LANES].astype(F32)
        qs += [(q2 * lo).astype(BF16), (q2 * hi).astype(BF16)]

    def one(qh, kc, vc, run, acc, diag):
        z = _nt_dot(qh, kc)
        tl = _softplus_tail(z)
        log_beta = jnp.minimum(z, 0.0) - tl
        log_1mb = -jnp.maximum(z, 0.0) - tl
        if diag:
            log_1mb = jnp.where(strict, log_1mb, 0.0)
        h1, h2, h3 = _split3(log_1mb)
        tail = (jnp.dot(h1, after, preferred_element_type=F32)
                + jnp.dot(h2, after, preferred_element_type=F32)
                + jnp.dot(h3, after, preferred_element_type=F32))
        a = jnp.exp(log_beta + (tail + run))
        if diag:
            a = jnp.where(strict, a, 0.0)
        acc = acc + jnp.dot(a.astype(BF16), vc, preferred_element_type=F32)
        run = run + (tail[:, 0:1] + log_1mb[:, 0:1])
        return run, acc

    def chunk(j, carry, diag):
        k0 = pl.multiple_of(j * t, t)
        out = []
        for hh in range(H_SB):
            cols = slice((hh // 2) * LANES, (hh // 2 + 1) * LANES)
            out.append(one(qs[hh], k_ref[pl.ds(k0, t), cols], v_ref[pl.ds(k0, t), cols],
                           *carry[hh], diag))
        return tuple(out)

    zero = (jnp.zeros((t, 1), F32), jnp.zeros((t, LANES), F32))
    carry = chunk(qi, (zero,) * H_SB, True)

    def live(state):
        j, c = state
        top = c[0][0]
        for run, _ in c[1:]:
            top = jnp.maximum(top, run)
        return jnp.logical_and(j >= 0, jnp.max(top) > SB_DEAD)

    def older(state):
        j, c = state
        return j - 1, chunk(j, c, False)

    _, carry = lax.while_loop(live, older, (qi - 1, carry))
    outs = [carry[2 * p][1] * lo + carry[2 * p + 1][1] * hi for p in range(H_SB // 2)]
    o_ref[...] = jnp.concatenate(outs, axis=1).astype(BF16)


def _sb_attention(q, k, v, b, t_len, t):
    nq = t_len // t
    return pl.pallas_call(
        functools.partial(_sb_kernel, t=t),
        name="stick_breaking_attention",
        grid=(b, nq),
        in_specs=[pl.BlockSpec((t, W_SB), lambda bi, i: (bi * nq + i, 0)),
                  pl.BlockSpec((t_len, W_SB), lambda bi, i: (bi, 0)),
                  pl.BlockSpec((t_len, W_SB), lambda bi, i: (bi, 0))],
        out_specs=pl.BlockSpec((t, W_SB), lambda bi, i: (bi * nq + i, 0)),
        out_shape=jax.ShapeDtypeStruct(q.shape, BF16),
        compiler_params=_cparams(("parallel", "arbitrary")),
    )(q, k, v)


def _sort_key(x, folded=False):
    if not folded:
        x = jnp.where(x == 0.0, 0.0, x)
    bits = pltpu.bitcast(x, jnp.int32)
    return bits ^ ((bits >> 31) & 0x7FFFFFFF)


def _top_half(x):
    bits = pltpu.bitcast(x, jnp.int32) & jnp.int32(-65536)
    return pltpu.bitcast(bits, F32).astype(BF16)


def _select_topk_bias(key_ref, nch, tk, nq, topk, keys_on_lanes, hi_ref=None):
    key_axis = 1 if keys_on_lanes else 0
    vec = (nq, 1) if keys_on_lanes else (1, nq)

    def load(k0):
        return key_ref[:, pl.ds(k0, tk)] if keys_on_lanes else key_ref[pl.ds(k0, tk), :]

    def store(k0, val):
        if keys_on_lanes:
            key_ref[:, pl.ds(k0, tk)] = val
        else:
            key_ref[pl.ds(k0, tk), :] = val

    fold = _fold_lanes if keys_on_lanes else _fold_rows

    def count(pred):
        def body(c, acc):
            k0 = pl.multiple_of(c * tk, tk)
            if keys_on_lanes or tk <= LANES:
                return acc + fold(jnp.where(pred(load(k0), k0), 1.0, 0.0))
            for u in range(tk // LANES):
                ku = k0 + u * LANES
                slab = key_ref[pl.ds(pl.multiple_of(ku, LANES), LANES), :]
                acc = acc + _tree_rows(jnp.where(pred(slab, ku), 1.0, 0.0), jnp.add, nacc=2)
            return acc
        part = (nq, LANES) if keys_on_lanes else (8, nq)
        acc = lax.fori_loop(0, nch, body, jnp.zeros(part, F32))
        return jnp.sum(acc, axis=key_axis, keepdims=True)

    kf = float(topk)

    def value_bit(i, ans):
        cand = ans | jnp.left_shift(jnp.int32(1), 31 - i)
        cand_s = cand ^ INT_MIN
        cnt = count(lambda blk, k0: blk >= cand_s)
        return jnp.where(cnt >= kf, cand, ans)

    first = 0
    ans = jnp.zeros(vec, jnp.int32)
    if hi_ref is not None:
        one_b, zero_b = jnp.ones((), BF16), jnp.zeros((), BF16)
        total = jnp.asarray(nch * tk, F32)
        below_all = INT_MIN + 0x007FFFFF

        def count_hi(cand16):
            def body(c, acc):
                k0 = pl.multiple_of(c * tk, tk)
                accs = [None, None]
                for u in range(tk // LANES):
                    slab = hi_ref[pl.ds(pl.multiple_of(k0 + u * LANES, LANES), LANES), :]
                    for i in range(LANES // 16):
                        hit = jnp.where(slab[i * 16:(i + 1) * 16] >= cand16, one_b, zero_b)
                        accs[i % 2] = hit if accs[i % 2] is None else accs[i % 2] + hit
                return acc + (accs[0] + accs[1]).astype(F32)
            acc = lax.fori_loop(0, nch, body, jnp.zeros((16, nq), F32))
            return jnp.sum(acc, axis=0, keepdims=True)

        def hi_bit(i, a):
            cand = a | jnp.left_shift(jnp.int32(1), 31 - i)
            cand_s = cand ^ INT_MIN
            bits = jnp.where(cand_s >= 0, cand_s, cand_s ^ 0x7FFFFFFF)
            tiny = jnp.where(bits < 0, 0, jnp.where((bits & 0x007F0000) != 0, 0x00800000, 0))
            bits = jnp.where((bits & 0x7F800000) == 0, tiny, bits)
            cand16 = _top_half(jnp.broadcast_to(pltpu.bitcast(bits, F32), (16, nq)))
            cnt = jnp.where(cand_s <= below_all, total, count_hi(cand16))
            return jnp.where(cnt >= kf, cand, a)

        first = 16
        ans = lax.fori_loop(0, first, hi_bit, ans)

    ans = lax.fori_loop(first, 32, value_bit, ans)
    thr = ans ^ INT_MIN
    need = kf - count(lambda blk, k0: blk > thr)

    r_i = lax.broadcasted_iota(jnp.int32, (tk, tk), 0)
    c_i = lax.broadcasted_iota(jnp.int32, (tk, tk), 1)
    tri = jnp.where(r_i <= c_i if keys_on_lanes else c_i <= r_i, 1.0, 0.0).astype(BF16)

    def write(c, base):
        k0 = pl.multiple_of(c * tk, tk)
        blk = load(k0)
        eq = blk == thr
        eqb = jnp.where(eq, 1.0, 0.0).astype(BF16)
        if keys_on_lanes:
            rank = jnp.dot(eqb, tri, preferred_element_type=F32)
            total = rank[:, tk - 1:tk]
        else:
            rank = jnp.dot(tri, eqb, preferred_element_type=F32)
            total = rank[tk - 1:tk, :]
        order = jnp.where(blk > thr, 0.0, jnp.where(eq, base + rank, jnp.inf))
        store(k0, pltpu.bitcast(jnp.where(order <= need, 0.0, NEG_INF), jnp.int32))
        return base + total

    lax.fori_loop(0, nch, write, jnp.zeros(vec, F32))


def _dsa_kernel(qt_ref, iqt_ref, misct_ref, ik_ref, k_ref, vt_ref, o_ref, key_ref, hi_ref, *,
                tq, tk, topk):
    qi = pl.program_id(1)
    q0 = qi * tq
    jd = q0 // tk
    nch = jd + 1
    lo, hi = _pair_masks_t(tq)
    key_in = lax.broadcasted_iota(jnp.int32, (tk, tq), 0)
    q_pos = q0 + lax.broadcasted_iota(jnp.int32, (tk, tq), 1)

    misct = misct_ref[...]
    iqt = iqt_ref[...].astype(F32)
    iq_heads = []
    for hh in range(IDX_HEADS):
        pair = iqt[(hh // 2) * LANES:(hh // 2 + 1) * LANES]
        iq_heads.append((pair * (lo if hh % 2 == 0 else hi)).astype(BF16))
    iws = [misct[MISC_IW + hh:MISC_IW + hh + 1] for hh in range(IDX_HEADS)]

    def score_chunk(c, diag):
        k0 = pl.multiple_of(c * tk, tk)
        ikc = ik_ref[pl.ds(k0, tk), :]
        isc = jnp.zeros((tk, tq), F32)
        for hh in range(IDX_HEADS):
            s = jnp.dot(ikc, iq_heads[hh], preferred_element_type=F32)
            isc = isc + iws[hh] * jnp.maximum(s, 0.0)
        if diag:
            isc = jnp.where(k0 + key_in <= q_pos, isc, NEG_INF)
        isc = jnp.where(jnp.abs(isc) < MIN_NORMAL, 0.0, isc)
        key_ref[pl.ds(k0, tk), :] = _sort_key(isc, folded=True)
        hi_ref[pl.ds(k0, tk), :] = _top_half(isc)

    def score_body(c, _):
        score_chunk(c, False)
        return 0

    lax.fori_loop(0, jd, score_body, 0)
    score_chunk(jd, True)

    _select_topk_bias(key_ref, nch, tk, tq, topk, keys_on_lanes=False, hi_ref=hi_ref)

    qs = []
    for p in range(H_DSA // 2):
        qt = qt_ref[p * LANES:(p + 1) * LANES, :].astype(F32)
        qs += [(qt * lo).astype(BF16), (qt * hi).astype(BF16)]

    def pair_cols(hh):
        return slice((hh // 2) * LANES, (hh // 2 + 1) * LANES)

    def scores(c):
        k0 = pl.multiple_of(c * tk, tk)
        bias = pltpu.bitcast(key_ref[pl.ds(k0, tk), :], F32)
        out = []
        for hh in range(H_DSA):
            st = jnp.dot(k_ref[pl.ds(k0, tk), pair_cols(hh)], qs[hh],
                         preferred_element_type=F32) + bias
            out.append((st, _max_rows(st)))
        return tuple(out)

    lo_b = (lax.broadcasted_iota(jnp.int32, (LANES, tk), 0) < HEAD_DIM).astype(F32).astype(BF16)
    hi_b = (1.0 - lo_b.astype(F32)).astype(BF16)

    def consume(s, c, carry):
        k0 = pl.multiple_of(c * tk, tk)
        out = []
        for p in range(H_DSA // 2):
            vts = _values_with_ones(vt_ref[p * LANES:(p + 1) * LANES, pl.ds(k0, tk)], lo_b, hi_b)
            for e in range(2):
                hh = 2 * p + e
                out.append(_flash_t_update(*s[hh], *carry[hh], vts[e], True))
        return tuple(out)

    def trip(c, state):
        s, carry = state
        s_next = scores(c + 1)
        return s_next, consume(s, c, carry)

    s, carry = lax.fori_loop(0, jd, trip,
                             (scores(0), tuple(_flash_t_init(tq) for _ in range(H_DSA))))
    causal = jd * tk + key_in <= q_pos
    s = tuple(jnp.where(causal, sh, NEG_INF) for sh, _ in s)
    carry = consume(tuple((sh, _max_rows(sh)) for sh in s), jd, carry)
    outs = [_flash_t_pair_out(carry[2 * p][1], carry[2 * p + 1][1]) for p in range(H_DSA // 2)]
    o_ref[...] = jnp.concatenate(outs, axis=0).astype(BF16)


def _dsa_attention(qt, iqt, misct, ikb, k, vt, b, t_len, tq, tk, topk):
    nq = t_len // tq
    colblk = lambda bi, i: (0, bi * nq + i)
    per_b = lambda bi, i: (bi, 0)
    return pl.pallas_call(
        functools.partial(_dsa_kernel, tq=tq, tk=tk, topk=topk),
        name="dsa_attention",
        grid=(b, nq),
        in_specs=[pl.BlockSpec((W_DSA, tq), colblk), pl.BlockSpec((W_IDX, tq), colblk),
                  pl.BlockSpec((LANES, tq), colblk), pl.BlockSpec((t_len, LANES), per_b),
                  pl.BlockSpec((t_len, W_DSA), per_b),
                  pl.BlockSpec((W_DSA, t_len), lambda bi, i: (0, bi))],
        out_specs=pl.BlockSpec((W_DSA, tq), colblk),
        out_shape=jax.ShapeDtypeStruct(qt.shape, BF16),
        scratch_shapes=[pltpu.VMEM((t_len, tq), jnp.int32), pltpu.VMEM((t_len, tq), BF16)],
        compiler_params=_cparams(("parallel", "arbitrary")),
    )(qt, iqt, misct, ikb, k, vt)


def _mem_kernel(q_ref, mkv_ref, o_ref, *, tq):
    lo, hi = _pair_masks(tq)
    outs = []
    for p in range(H_MEM // 2):
        q2 = q_ref[:, p * LANES:(p + 1) * LANES].astype(F32)
        mk = mkv_ref[0, :, p * LANES:(p + 1) * LANES].astype(BF16)
        mv = mkv_ref[0, :, W_MEM + p * LANES:W_MEM + (p + 1) * LANES].astype(BF16)

        def head(qh):
            s = _nt_dot(qh, mk)
            pr = jnp.exp(s - jnp.max(s, axis=-1, keepdims=True))
            l = jnp.sum(pr, axis=-1, keepdims=True)
            return jnp.dot(pr.astype(BF16), mv, preferred_element_type=F32) / l

        outs.append(head((q2 * lo).astype(BF16)) * lo + head((q2 * hi).astype(BF16)) * hi)
    o_ref[...] = jnp.concatenate(outs, axis=1).astype(BF16)


def _mem_attention(q, mkv, b, t_len, tq):
    nq = t_len // tq
    n_mem = mkv.shape[1]
    return pl.pallas_call(
        functools.partial(_mem_kernel, tq=tq),
        grid=(b, nq),
        in_specs=[pl.BlockSpec((tq, W_MEM), lambda bi, i: (bi * nq + i, 0)),
                  pl.BlockSpec((1, n_mem, 2 * W_MEM), lambda bi, i: (bi, 0, 0))],
        out_specs=pl.BlockSpec((tq, W_MEM), lambda bi, i: (bi * nq + i, 0)),
        out_shape=jax.ShapeDtypeStruct(q.shape, BF16),
        compiler_params=_cparams(("parallel", "arbitrary")),
    )(q, mkv)


def _rms(x, g):
    return x * lax.rsqrt(jnp.mean(x * x, axis=-1, keepdims=True) + EPS) * g


def _merge_kernel(x_ref, of_ref, od_ref, os_ref, om_ref, gpre_ref, gpost_ref, wg_ref,
                  wf_ref, wd_ref, ws_ref, wm_ref, wo_ref, y_ref, *, transposed):
    x = x_ref[...]
    d = x.shape[1]
    hb = _rms(x, gpre_ref[...]).astype(BF16)
    merged = None
    for i, (o_ref, w_ref) in enumerate(((of_ref, wf_ref), (od_ref, wd_ref),
                                        (os_ref, ws_ref), (om_ref, wm_ref))):
        gate = jax.nn.sigmoid(_nt_dot(hb, wg_ref[i * d:(i + 1) * d, :]))
        o = o_ref[...]
        if transposed and i < 2:
            o = o.astype(F32).T
        br = jnp.dot(o.astype(BF16), w_ref[...], preferred_element_type=F32)
        merged = gate * br if merged is None else merged + gate * br
    y = jnp.dot(merged.astype(BF16), wo_ref[...], preferred_element_type=F32)
    y_ref[...] = x + _rms(y, gpost_ref[...])


def _merge(x, o_fox, o_dsa, o_sb, o_mem, gpre, gpost, wg, wf, wd, ws, wm, wo, tm, transposed):
    n, d = x.shape
    row = lambda i: (i, 0)
    const = lambda i: (0, 0)
    full = lambda a: pl.BlockSpec(a.shape, const)

    def o_spec(o, can_t):
        if transposed and can_t:
            return pl.BlockSpec((o.shape[0], tm), lambda i: (0, i))
        return pl.BlockSpec((tm, o.shape[1]), row)

    return pl.pallas_call(
        functools.partial(_merge_kernel, transposed=transposed),
        name="branch_merge",
        grid=(n // tm,),
        in_specs=[pl.BlockSpec((tm, d), row), o_spec(o_fox, True), o_spec(o_dsa, True),
                  o_spec(o_sb, False), o_spec(o_mem, False)]
        + [full(a) for a in (gpre, gpost, wg, wf, wd, ws, wm, wo)],
        out_specs=pl.BlockSpec((tm, d), row),
        out_shape=jax.ShapeDtypeStruct((n, d), F32),
        compiler_params=_cparams(("parallel",)),
    )(x, o_fox, o_dsa, o_sb, o_mem, gpre, gpost, wg, wf, wd, ws, wm, wo)


def _ffn_kernel(x_ref, gpre_ref, gpost_ref, wi_ref, wo_ref, y_ref, *, d_ff, tc):
    x = x_ref[...]
    hb = _rms(x, gpre_ref[...]).astype(BF16)
    y = jnp.zeros(x.shape, F32)
    for c0 in range(0, d_ff, tc):
        gate = jnp.dot(hb, wi_ref[:, c0:c0 + tc], preferred_element_type=F32)
        up = jnp.dot(hb, wi_ref[:, d_ff + c0:d_ff + c0 + tc], preferred_element_type=F32)
        act = (gate * jax.nn.sigmoid(gate)) * up
        y = y + jnp.dot(act.astype(BF16), wo_ref[c0:c0 + tc, :], preferred_element_type=F32)
    y_ref[...] = x + _rms(y, gpost_ref[...])


def _ffn(x, gpre, gpost, wi, wo, tm):
    n, d = x.shape
    d_ff = wo.shape[0]
    tc = 256 if d_ff % 256 == 0 else d_ff
    row = lambda i: (i, 0)
    const = lambda i: (0, 0)
    return pl.pallas_call(
        functools.partial(_ffn_kernel, d_ff=d_ff, tc=tc),
        name="swiglu_ffn",
        grid=(n // tm,),
        in_specs=[pl.BlockSpec((tm, d), row), pl.BlockSpec(gpre.shape, const),
                  pl.BlockSpec(gpost.shape, const), pl.BlockSpec(wi.shape, const),
                  pl.BlockSpec(wo.shape, const)],
        out_specs=pl.BlockSpec((tm, d), row),
        out_shape=jax.ShapeDtypeStruct((n, d), F32),
        compiler_params=_cparams(("parallel",)),
    )(x, gpre, gpost, wi, wo)


def _dec_score_kernel(pt_ref, iqh_ref, iwb_ref, iknew_ref, *rest, npages):
    pages = rest[:npages]
    o_ref = rest[npages]
    iqh = iqh_ref[0]
    iwb = iwb_ref[0]
    for j in range(npages):
        s = jnp.dot(iqh, pages[j][0, 0].astype(BF16), preferred_element_type=F32)
        o_ref[0, :, j * LANES:(j + 1) * LANES] = jnp.sum(iwb * jnp.maximum(s, 0.0), axis=0,
                                                         keepdims=True)
    ik_new = iknew_ref[0][:, :IDX_DIM].astype(BF16).astype(F32)
    s_new = jnp.sum(iqh.astype(F32) * ik_new, axis=-1, keepdims=True)
    isc_new = jnp.sum(iwb[:, 0:1] * jnp.maximum(s_new, 0.0), axis=0, keepdims=True)
    lane = lax.broadcasted_iota(jnp.int32, (1, LANES), 1)
    o_ref[0, :, npages * LANES:] = jnp.where(lane == 0, isc_new, NEG_INF)


def _dec_scores(pt_flat, iqh, iwb, ik_new, idxk_pool, layer, nb, npages):
    width = (npages + 1) * LANES
    seq = lambda bi, pt: (bi, 0, 0)

    def page_map(j):
        return lambda bi, pt: (layer, pt[bi * npages + j], 0, 0)

    grid_spec = pltpu.PrefetchScalarGridSpec(
        num_scalar_prefetch=1, grid=(nb,),
        in_specs=[pl.BlockSpec((1, 8, IDX_DIM), seq), pl.BlockSpec((1, 8, LANES), seq),
                  pl.BlockSpec((1, 1, LANES), seq)]
        + [pl.BlockSpec((1, 1, IDX_DIM, PAGE_SIZE), page_map(j)) for j in range(npages)],
        out_specs=pl.BlockSpec((1, 1, width), seq))
    return pl.pallas_call(
        functools.partial(_dec_score_kernel, npages=npages),
        name="decode_index_scores",
        grid_spec=grid_spec,
        out_shape=jax.ShapeDtypeStruct((nb, 1, width), F32),
        compiler_params=_cparams(("arbitrary",)),
    )(pt_flat, iqh, iwb, ik_new, *([idxk_pool] * npages))


def _dec_select_kernel(isc_ref, o_ref, key_ref, *, topk):
    rows, width = isc_ref.shape
    key_ref[...] = _sort_key(isc_ref[...])
    _select_topk_bias(key_ref, width // LANES, LANES, rows, topk, keys_on_lanes=True)
    o_ref[...] = pltpu.bitcast(key_ref[...], F32)


def _dec_select(isc, topk):
    return pl.pallas_call(
        functools.partial(_dec_select_kernel, topk=topk),
        out_shape=jax.ShapeDtypeStruct(isc.shape, F32),
        scratch_shapes=[pltpu.VMEM(isc.shape, jnp.int32)],
        compiler_params=pltpu.CompilerParams(vmem_limit_bytes=VMEM_LIMIT),
    )(isc)


def _head_rows(width):
    sub = lax.broadcasted_iota(jnp.int32, (8, width), 0)
    lane = lax.broadcasted_iota(jnp.int32, (8, width), 1)
    return ((lane >> 6) == sub).astype(F32)


def _dec_attn_kernel(pt_ref, fq_ref, dq_ref, sq_ref, mq_ref, fnew_ref, dnew_ref, lnew_ref,
                     bias_ref, mem_ref, *rest, npages):
    fox_pages = rest[0:npages]
    logf_pages = rest[npages:2 * npages]
    dsa_pages = rest[2 * npages:3 * npages]
    sb_pages = rest[3 * npages:4 * npages]
    of_ref, od_ref, os_ref, om_ref = rest[4 * npages:]

    row = lax.broadcasted_iota(jnp.int32, (PAGE_SIZE, PAGE_SIZE), 0)
    col = lax.broadcasted_iota(jnp.int32, (PAGE_SIZE, PAGE_SIZE), 1)
    after = (row > col).astype(BF16)

    past = npages * PAGE_SIZE

    def keys_t(pages):
        return jnp.concatenate([pg[0, 0, 0].astype(BF16) for pg in pages], axis=1)

    def values_t(pages):
        return jnp.concatenate([pg[0, 0, 1].astype(BF16) for pg in pages], axis=1)

    def page_rows(x):
        return jnp.concatenate([x[:, j * PAGE_SIZE:(j + 1) * PAGE_SIZE] for j in range(npages)],
                               axis=0)

    def suffix_sums(x_rows, run):
        h1, h2, h3 = _split3(x_rows)
        inner = (jnp.dot(h1, after, preferred_element_type=F32)
                 + jnp.dot(h2, after, preferred_element_type=F32)
                 + jnp.dot(h3, after, preferred_element_type=F32))
        tot = jnp.sum(x_rows, axis=-1, keepdims=True)
        outs = [None] * npages
        for j in range(npages - 1, -1, -1):
            outs[j] = inner[j * 8:(j + 1) * 8] + run
            run = run + tot[j * 8:(j + 1) * 8]
        return jnp.concatenate(outs, axis=1)

    def head_diag(o, mask):
        return jnp.sum(o * mask, axis=0, keepdims=True)

    def softmax_pv(s_past, s_new, vt_all, v_new):
        m = jnp.maximum(jnp.max(s_past, axis=-1, keepdims=True), s_new)
        p_new = jnp.exp(s_new - m)
        pr = jnp.exp(s_past - m)
        l = p_new + jnp.sum(pr, axis=-1, keepdims=True)
        acc = p_new.astype(BF16).astype(F32) * v_new + _nt_dot(pr.astype(BF16), vt_all)
        return acc / l

    mask_f = _head_rows(W_FOX)
    qf = (fq_ref[0].astype(F32) * mask_f).astype(BF16)
    k_new = fnew_ref[0][:, :W_FOX].astype(BF16).astype(F32)
    v_new = fnew_ref[0][:, W_FOX:].astype(BF16).astype(F32)
    s_new = jnp.sum(qf.astype(F32) * k_new, axis=-1, keepdims=True)
    lf_rows = jnp.concatenate([pg[0, 0] for pg in logf_pages], axis=0)
    decay = suffix_sums(lf_rows, lnew_ref[0][:, 0:1])
    s_past = jnp.dot(qf, keys_t(fox_pages), preferred_element_type=F32) + decay
    of_ref[0] = head_diag(softmax_pv(s_past, s_new, values_t(fox_pages), v_new), mask_f)

    mask_d = _head_rows(W_DSA)
    qd = (dq_ref[0].astype(F32) * mask_d).astype(BF16)
    k_new = dnew_ref[0][:, :W_DSA].astype(BF16).astype(F32)
    v_new = dnew_ref[0][:, W_DSA:].astype(BF16).astype(F32)
    s_new = (jnp.sum(qd.astype(F32) * k_new, axis=-1, keepdims=True)
             + bias_ref[0][:, past:past + 1])
    s_past = jnp.dot(qd, keys_t(dsa_pages), preferred_element_type=F32) + bias_ref[0][:, :past]
    od_ref[0] = head_diag(softmax_pv(s_past, s_new, values_t(dsa_pages), v_new), mask_d)

    mask_s = _head_rows(W_SB)
    qs = (sq_ref[0].astype(F32) * mask_s).astype(BF16)
    z = jnp.dot(qs, keys_t(sb_pages), preferred_element_type=F32)
    tl = _softplus_tail(z)
    log_beta = jnp.minimum(z, 0.0) - tl
    log_1mb = -jnp.maximum(z, 0.0) - tl
    tail = suffix_sums(page_rows(log_1mb), jnp.zeros((8, 1), F32))
    a = jnp.exp(log_beta + tail)
    os_ref[0] = head_diag(_nt_dot(a.astype(BF16), values_t(sb_pages)), mask_s)

    mask_m = _head_rows(W_MEM)
    qm = (mq_ref[0].astype(F32) * mask_m).astype(BF16)
    s = jnp.dot(qm, mem_ref[0, 0, 0].astype(BF16), preferred_element_type=F32)
    pr = jnp.exp(s - jnp.max(s, axis=-1, keepdims=True))
    l = jnp.sum(pr, axis=-1, keepdims=True)
    o = _nt_dot(pr.astype(BF16), mem_ref[0, 0, 1].astype(BF16))
    om_ref[0] = head_diag(o / l, mask_m)


def _dec_attention(pt_flat, fq, dq, sq, mq, fnew, dnew, lnew, bias, mem, fox_pool, logf_pool,
                   dsa_pool, sb_pool, layer, nb, npages):
    seq = lambda bi, pt: (bi, 0, 0)

    def page_map(j, nd):
        return lambda bi, pt: (layer, pt[bi * npages + j]) + (0,) * nd

    def pages(pool):
        blk = (1, 1) + pool.shape[2:]
        return [pl.BlockSpec(blk, page_map(j, len(blk) - 2)) for j in range(npages)]

    def seq_spec(a):
        return pl.BlockSpec((1,) + a.shape[1:], seq)

    mem_spec = pl.BlockSpec((1, 1) + mem.shape[2:], lambda bi, pt: (layer, bi, 0, 0, 0))
    grid_spec = pltpu.PrefetchScalarGridSpec(
        num_scalar_prefetch=1, grid=(nb,),
        in_specs=[seq_spec(a) for a in (fq, dq, sq, mq, fnew, dnew, lnew, bias)] + [mem_spec]
        + pages(fox_pool) + pages(logf_pool) + pages(dsa_pool) + pages(sb_pool),
        out_specs=[pl.BlockSpec((1, 1, wd), seq) for wd in (W_FOX, W_DSA, W_SB, W_MEM)])
    return pl.pallas_call(
        functools.partial(_dec_attn_kernel, npages=npages),
        name="decode_attention",
        grid_spec=grid_spec,
        out_shape=[jax.ShapeDtypeStruct((nb, 1, wd), F32) for wd in (W_FOX, W_DSA, W_SB, W_MEM)],
        compiler_params=_cparams(("arbitrary",)),
    )(pt_flat, fq, dq, sq, mq, fnew, dnew, lnew, bias, mem,
      *([fox_pool] * npages), *([logf_pool] * npages), *([dsa_pool] * npages),
      *([sb_pool] * npages))


def _rope_tables(pos):
    rd = HEAD_DIM // 4
    half = rd // 2
    inv_freq = ROPE_THETA ** (-jnp.arange(half, dtype=F32) * 2.0 / rd)
    ang = pos.astype(F32)[:, None] * inv_freq[None, :]
    cos, sin = jnp.cos(ang), jnp.sin(ang)
    n = pos.shape[0]
    one = jnp.ones((n, HEAD_DIM - rd), F32)
    zero = jnp.zeros((n, HEAD_DIM - rd), F32)
    zh = jnp.zeros((n, half), F32)
    cos64 = jnp.concatenate([cos, cos, one], axis=1)
    sa64 = jnp.concatenate([-sin, zh, zero], axis=1)
    sb64 = jnp.concatenate([zh, sin, zero], axis=1)
    dup = lambda a: jnp.concatenate([a, a], axis=1)
    return dup(cos64), dup(sa64), dup(sb64), cos.T, sin.T


def _prep_w_in(w_in_l, b_forget_l, d_model):
    offs = np.cumsum([0, W_FOX, W_FOX, W_FOX, H_FOX, W_DSA, W_DSA, W_DSA, W_IDX, IDX_DIM,
                      IDX_HEADS, W_SB, W_SB, W_SB, W_MEM])
    (o_fq, o_fk, o_fv, o_ff, o_dq, o_dk, o_dv, o_iq, o_ik, o_iw, o_sq, o_sk, o_sv, o_mq,
     o_g) = [int(v) for v in offs]
    wt = jnp.transpose(w_in_l)
    sl = lambda o, n: wt[o:o + n]
    ik = sl(o_ik, IDX_DIM)
    misc = jnp.concatenate([sl(o_ff, H_FOX), sl(o_iw, IDX_HEADS),
                            jnp.zeros((LANES - H_FOX - IDX_HEADS, d_model), F32)], axis=0)
    wp = jnp.concatenate([sl(o_fq, 3 * W_FOX), sl(o_dq, 3 * W_DSA), sl(o_iq, W_IDX),
                          sl(o_sq, 3 * W_SB), sl(o_mq, W_MEM), ik, ik, misc], axis=0)
    wg = wt[o_g:]
    bfp = jnp.concatenate([b_forget_l, jnp.zeros((LANES - H_FOX,), F32)])[None, :]
    return wp.astype(BF16), wg.astype(BF16), bfp


def kernel(x_prompt, x_sample, cache_fox_kv, cache_fox_logf, cache_dsa_kv, cache_dsa_idxk,
           cache_sb_kv, cache_mem_kv, page_table, mem_prompt, w_in, b_forget, w_mem_kv,
           w_br_fox, w_br_dsa, w_br_sb, w_br_mem, w_out, w_ffn_in, w_ffn_out,
           g_mix_pre, g_mix_post, g_ffn_pre, g_ffn_post):
    bp, t_len, d = x_prompt.shape
    nb = x_sample.shape[0]
    depth = w_in.shape[0]
    npages = page_table.shape[1]
    past_len = npages * PAGE_SIZE
    n_pool = cache_fox_kv.shape[1]
    n_mem = mem_prompt.shape[1]
    n_p = bp * t_len

    tm = min(256, t_len)
    t_att = min(256, t_len)
    tq_dsa = min(256, t_len)
    tk_dsa = min(512, t_len)
    topk_p = min(DSA_TOPK_MAX, t_len // 4)
    topk_s = min(DSA_TOPK_MAX, (past_len + 1) // 4)
    assert tk_dsa >= topk_p and t_len % tk_dsa == 0 and t_len % tm == 0

    tabs_p = _rope_tables(jnp.arange(t_len, dtype=jnp.int32))
    tabs_s = _rope_tables(jnp.full((nb,), past_len, jnp.int32))
    pt_flat = page_table.reshape(-1).astype(jnp.int32)

    def kv_view(cache, width):
        view = jnp.transpose(cache, (0, 1, 3, 4, 5, 2))
        return view.reshape(cache.shape[:2] + (2, width, cache.shape[2]))

    fox_t = kv_view(cache_fox_kv, W_FOX)
    dsa_t = kv_view(cache_dsa_kv, W_DSA)
    sb_t = kv_view(cache_sb_kv, W_SB)
    mem_t = kv_view(cache_mem_kv, W_MEM)
    idxk_t = jnp.transpose(cache_dsa_idxk, (0, 1, 3, 2))
    logf_t = jnp.pad(jnp.transpose(cache_fox_logf, (0, 1, 3, 2)),
                     ((0, 0), (0, 0), (0, 8 - H_FOX), (0, 0)))

    xp = x_prompt.reshape(n_p, d)
    xs = x_sample.reshape(nb, d)
    mem_flat = mem_prompt.reshape(bp * n_mem, d)

    rows_p, rows_s, mem_p = [], [], []
    stacked_p = None
    for l in range(depth):
        wp, wg, bfp = _prep_w_in(w_in[l], b_forget[l], d)
        gpre, gpost = g_mix_pre[l][None, :], g_mix_post[l][None, :]
        fpre, fpost = g_ffn_pre[l][None, :], g_ffn_post[l][None, :]
        wf, wd_, ws, wm = (w.astype(BF16) for w in (w_br_fox[l], w_br_dsa[l], w_br_sb[l], w_br_mem[l]))
        wo = w_out[l].astype(BF16)
        wi, wfo = w_ffn_in[l].astype(BF16), w_ffn_out[l].astype(BF16)

        mem_kv = _matmul(mem_flat, w_mem_kv[l].astype(BF16))
        (fqt, fkv_st, fk, fvt, dqt, dkv_st, dk, dvt, iqt, sq, skv_st, sk, sv, mq, ik_st, ikb,
         misc, misct) = _proj(xp, gpre, wp, tabs_p, bfp, tm, True, stacked_p, l, depth)
        stacked_p = (fkv_st, dkv_st, skv_st, ik_st)
        logf = misc[:, :H_FOX].reshape(bp, t_len, H_FOX)
        c = jnp.cumsum(logf, axis=1) * LOG2_E
        fqt_full, fk_full = _fox_operands(fqt, fk, c.reshape(n_p, H_FOX))
        o_fox = _fox_attention(fqt_full, fk_full, fvt, bp, t_len, t_att, tk_dsa)
        o_dsa = _dsa_attention(dqt, iqt, misct, ikb, dk, dvt, bp, t_len, tq_dsa, tk_dsa, topk_p)
        o_sb = _sb_attention(sq, sk, sv, bp, t_len, t_att)
        o_mem = _mem_attention(mq, mem_kv.reshape(bp, n_mem, 2 * W_MEM), bp, t_len, tm)
        xp = _merge(xp, o_fox, o_dsa, o_sb, o_mem, gpre, gpost, wg, wf, wd_, ws, wm, wo, tm, True)
        xp = _ffn(xp, fpre, fpost, wi, wfo, tm)
        rows_p.append(logf)
        mem_p.append(mem_kv.reshape(bp, n_mem, 2, H_MEM, HEAD_DIM))

        (fq, fkv, fk, fv, dq, dkv, dk, dv, iq, sq, skv, sk, sv, mq, ik32, ikb,
         misc) = _proj(xs, gpre, wp, tabs_s, bfp, nb, False)
        r3 = lambda a: a.reshape(nb, 1, a.shape[1])
        iqh = jnp.pad(iq.reshape(nb, IDX_HEADS, IDX_DIM), ((0, 0), (0, 8 - IDX_HEADS), (0, 0)))
        iwb = jnp.pad(misc[:, MISC_IW:MISC_IW + IDX_HEADS], ((0, 0), (0, 8 - IDX_HEADS)))
        iwb = jnp.broadcast_to(iwb[:, :, None], (nb, 8, LANES))
        isc = _dec_scores(pt_flat, iqh, iwb, r3(ik32), idxk_t, l, nb, npages)
        bias = _dec_select(isc.reshape(nb, -1), topk_s).reshape(nb, 1, -1)
        lnew = jnp.pad(misc[:, :H_FOX], ((0, 0), (0, 8 - H_FOX)))
        lnew = jnp.broadcast_to(lnew[:, :, None], (nb, 8, LANES))
        o_fox, o_dsa, o_sb, o_mem = _dec_attention(
            pt_flat, r3(fq), r3(dq), r3(sq), r3(mq), r3(fkv), r3(dkv), lnew, bias,
            mem_t, fox_t, logf_t, dsa_t, sb_t, l, nb, npages)
        sq2 = lambda a: a.reshape(nb, a.shape[2])
        xs = _merge(xs, sq2(o_fox), sq2(o_dsa), sq2(o_sb), sq2(o_mem), gpre, gpost, wg, wf, wd_,
                    ws, wm, wo, nb, False)
        xs = _ffn(xs, fpre, fpost, wi, wfo, nb)
        rows_s.append((fkv.reshape(nb, 1, 2, H_FOX, HEAD_DIM), misc[:, :H_FOX].reshape(nb, 1, H_FOX),
                       dkv.reshape(nb, 1, 2, H_DSA, HEAD_DIM),
                       ik32[:, :IDX_DIM].reshape(nb, 1, IDX_DIM),
                       skv.reshape(nb, 1, 2, H_SB, HEAD_DIM)))

    stk = lambda rows, i: jnp.stack([r[i] for r in rows], axis=0)
    fkv_st, dkv_st, skv_st, ik_st = stacked_p

    def kv_rows(buf, heads):
        return jnp.transpose(buf.reshape(depth, bp, 2, heads, HEAD_DIM, t_len), (0, 1, 5, 2, 3, 4))

    return (xp.reshape(bp, t_len, d), xs.reshape(nb, 1, d),
            kv_rows(fkv_st, H_FOX), jnp.stack(rows_p, axis=0), kv_rows(dkv_st, H_DSA),
            jnp.transpose(ik_st, (0, 1, 3, 2)), kv_rows(skv_st, H_SB),
            jnp.stack(mem_p, axis=0),
            stk(rows_s, 0), stk(rows_s, 1), stk(rows_s, 2), stk(rows_s, 3), stk(rows_s, 4))
```

```python
import functools

import jax
import jax.numpy as jnp
import numpy as np
from jax import lax
from jax.experimental import pallas as pl
from jax.experimental.pallas import tpu as pltpu

HEAD_DIM = 64
H_FOX = 6
H_DSA = 6
H_SB = 4
H_MEM = 4
IDX_HEADS = 4
IDX_DIM = 64
DSA_TOPK_MAX = 256
ROPE_THETA = 500000.0
N_BRANCH = 4
EPS = 1e-6
PAGE_SIZE = 128

LANES = 128
VMEM_LIMIT = 56 * 1024 * 1024

F32 = jnp.float32
BF16 = jnp.bfloat16
NEG_INF = float("-inf")
INT_MIN = -2 ** 31
SB_DEAD = -120.0
LOG2_E = 1.4426950408889634
MIN_NORMAL = 2.0 ** -126

W_FOX = H_FOX * HEAD_DIM
W_DSA = H_DSA * HEAD_DIM
W_SB = H_SB * HEAD_DIM
W_MEM = H_MEM * HEAD_DIM
W_IDX = IDX_HEADS * IDX_DIM

C_FQ, C_FKV = 0, W_FOX
C_DQ = C_FKV + 2 * W_FOX
C_DK = C_DQ + W_DSA
C_DV = C_DK + W_DSA
C_IQ = C_DV + W_DSA
C_SQ = C_IQ + W_IDX
C_SKV = C_SQ + W_SB
C_MQ = C_SKV + 2 * W_SB
C_IK = C_MQ + W_MEM
C_MISC = C_IK + LANES
N_PROJ = C_MISC + LANES
MISC_IW = H_FOX


def _cparams(sem):
    return pltpu.CompilerParams(dimension_semantics=sem, vmem_limit_bytes=VMEM_LIMIT)


def _nt_dot(a, b):
    return lax.dot_general(a, b, (((1,), (1,)), ((), ())), preferred_element_type=F32)


def _softplus_tail(z):
    return jnp.log1p(jnp.exp(-jnp.abs(z)))


def _split3(x):
    hi = x.astype(BF16)
    r1 = x - hi.astype(F32)
    mid = r1.astype(BF16)
    lo = (r1 - mid.astype(F32)).astype(BF16)
    return hi, mid, lo


def _proj_kernel(x_ref, g_ref, w_ref, cos_ref, sa_ref, sb_ref, bf_ref, cost_ref, sint_ref,
                 *refs, transposed, n_alias):
    (fq_ref, fkv_ref, fk_ref, fv_ref, dq_ref, dkv_ref, dk_ref, dv_ref, iq_ref, sq_ref, skv_ref,
     sk_ref, sv_ref, mq_ref, ik32_ref, ikb_ref, misc_ref, *extra) = refs[n_alias:]
    x = x_ref[...]
    h = x * lax.rsqrt(jnp.mean(x * x, axis=-1, keepdims=True) + EPS)
    hb = (h * g_ref[...]).astype(BF16)
    cosf, sa, sb = cos_ref[...], sa_ref[...], sb_ref[...]
    scale = HEAD_DIM ** -0.5

    def mm(c0, n):
        return _nt_dot(hb, w_ref[c0:c0 + n, :])

    def mm_t(c0, n):
        return _nt_dot(w_ref[c0:c0 + n, :], hb)

    def rope(z):
        outs = []
        for j in range(z.shape[1] // LANES):
            zj = z[:, j * LANES:(j + 1) * LANES]
            outs.append(zj * cosf + pltpu.roll(zj, LANES - 8, 1) * sa + pltpu.roll(zj, 8, 1) * sb)
        return outs[0] if len(outs) == 1 else jnp.concatenate(outs, axis=1)

    def rope_t(zt):
        cos_t, sin_t = cost_ref[...], sint_ref[...]
        half = HEAD_DIM // 8
        parts = []
        for hh in range(zt.shape[0] // HEAD_DIM):
            base = hh * HEAD_DIM
            x1, x2 = zt[base:base + half], zt[base + half:base + 2 * half]
            parts += [x1 * cos_t - x2 * sin_t, x2 * cos_t + x1 * sin_t,
                      zt[base + 2 * half:base + HEAD_DIM]]
        return jnp.concatenate(parts, axis=0)

    dk = rope(mm(C_DK, W_DSA))
    dk_ref[...] = dk.astype(BF16)
    skv = mm(C_SKV, 2 * W_SB)
    ik = rope(mm(C_IK, LANES))
    if transposed:
        scale2 = scale * LOG2_E
        fq_ref[...] = (mm_t(C_FQ, W_FOX) * scale2).astype(BF16)
        fkv_t = mm_t(C_FKV, 2 * W_FOX)
        fkv_ref[0, 0] = fkv_t
        fv_ref[...] = fkv_t[W_FOX:].astype(BF16)
        fk_ref[...] = mm(C_FKV, W_FOX).astype(BF16)
        dq_ref[...] = (rope_t(mm_t(C_DQ, W_DSA)) * scale2).astype(BF16)
        dv_t = mm_t(C_DV, W_DSA)
        dkv_ref[0, 0, :W_DSA] = rope_t(mm_t(C_DK, W_DSA))
        dkv_ref[0, 0, W_DSA:] = dv_t
        dv_ref[...] = dv_t.astype(BF16)
        iq_ref[...] = rope_t(mm_t(C_IQ, W_IDX)).astype(BF16)
        skv_ref[0, 0] = mm_t(C_SKV, 2 * W_SB)
        ik32_ref[0, 0] = rope_t(mm_t(C_IK, LANES))[:IDX_DIM]
        extra[0][...] = mm_t(C_MISC, LANES)
    else:
        fkv = mm(C_FKV, 2 * W_FOX)
        fkv_ref[...] = fkv
        fk_ref[...] = fkv[:, :W_FOX].astype(BF16)
        fq_ref[...] = (mm(C_FQ, W_FOX) * scale).astype(BF16)
        fv_ref[...] = fkv[:, W_FOX:].astype(BF16)
        dq_ref[...] = (rope(mm(C_DQ, W_DSA)) * scale).astype(BF16)
        dv = mm(C_DV, W_DSA)
        dkv_ref[:, :W_DSA] = dk
        dkv_ref[:, W_DSA:] = dv
        dv_ref[...] = dv.astype(BF16)
        iq_ref[...] = rope(mm(C_IQ, W_IDX)).astype(BF16)
        skv_ref[...] = skv
        ik32_ref[...] = ik

    sq_ref[...] = (mm(C_SQ, W_SB) * scale).astype(BF16)
    sk_ref[...] = skv[:, :W_SB].astype(BF16)
    sv_ref[...] = skv[:, W_SB:].astype(BF16)
    mq_ref[...] = (mm(C_MQ, W_MEM) * scale).astype(BF16)
    ikb_ref[...] = ik.astype(BF16)

    zm = mm(C_MISC, LANES)
    ff = zm + bf_ref[...]
    logf = -(jnp.maximum(-ff, 0.0) + _softplus_tail(ff))
    lane = lax.broadcasted_iota(jnp.int32, zm.shape, 1)
    misc_ref[...] = jnp.where(lane < H_FOX, logf, zm)


def _proj(x, g, w, tabs, bfp, tm, transposed, stacked=None, layer=0, depth=1):
    n, d = x.shape
    cosf, sa, sb, cos_t, sin_t = tabs
    nt = cosf.shape[0] // tm
    row = lambda i: (i, 0)
    col = lambda i: (0, i)
    tab = lambda i: (i % nt, 0)
    tab_t = lambda i: (0, i % nt)
    const = lambda i: (0, 0)
    stack_blk = lambda i: (layer, i // nt, 0, i % nt)
    widths = [(W_FOX, BF16, 't'), (2 * W_FOX, F32, 's'), (W_FOX, BF16, ''), (W_FOX, BF16, 't'),
              (W_DSA, BF16, 't'), (2 * W_DSA, F32, 's'), (W_DSA, BF16, ''), (W_DSA, BF16, 't'),
              (W_IDX, BF16, 't'), (W_SB, BF16, ''), (2 * W_SB, F32, 's'), (W_SB, BF16, ''),
              (W_SB, BF16, ''), (W_MEM, BF16, ''), (IDX_DIM if transposed else LANES, F32, 's'),
              (LANES, BF16, ''), (LANES, F32, '')]
    if transposed:
        widths.append((LANES, F32, 't'))
    out_specs, out_shape, stacked_idx = [], [], []
    for idx, (wd, dt, kind) in enumerate(widths):
        if transposed and kind == 't':
            out_specs.append(pl.BlockSpec((wd, tm), col))
            out_shape.append(jax.ShapeDtypeStruct((wd, n), dt))
        elif transposed and kind == 's':
            stacked_idx.append(idx)
            out_specs.append(pl.BlockSpec((1, 1, wd, tm), stack_blk))
            out_shape.append(jax.ShapeDtypeStruct((depth, n // (nt * tm), wd, nt * tm), dt))
        else:
            out_specs.append(pl.BlockSpec((tm, wd), row))
            out_shape.append(jax.ShapeDtypeStruct((n, wd), dt))
    in_specs = [pl.BlockSpec((tm, d), row), pl.BlockSpec((1, d), const),
                pl.BlockSpec((N_PROJ, d), const),
                pl.BlockSpec((tm, LANES), tab), pl.BlockSpec((tm, LANES), tab),
                pl.BlockSpec((tm, LANES), tab), pl.BlockSpec((1, LANES), const),
                pl.BlockSpec((8, tm), tab_t), pl.BlockSpec((8, tm), tab_t)]
    operands = [x, g, w, cosf, sa, sb, bfp, cos_t, sin_t]
    aliases = {}
    if stacked is not None:
        for k, buf in enumerate(stacked):
            aliases[len(operands)] = stacked_idx[k]
            in_specs.append(pl.BlockSpec(memory_space=pl.ANY))
            operands.append(buf)
    n_alias = len(aliases)
    return pl.pallas_call(
        functools.partial(_proj_kernel, transposed=transposed, n_alias=n_alias),
        name="input_projection",
        grid=(n // tm,),
        in_specs=in_specs,
        out_specs=out_specs,
        out_shape=out_shape,
        input_output_aliases=aliases,
        compiler_params=_cparams(("parallel",)),
    )(*operands)


def _matmul_kernel(a_ref, b_ref, o_ref):
    o_ref[...] = jnp.dot(a_ref[...].astype(BF16), b_ref[...], preferred_element_type=F32)


def _matmul(a, b):
    return pl.pallas_call(
        _matmul_kernel,
        out_shape=jax.ShapeDtypeStruct((a.shape[0], b.shape[1]), F32),
        compiler_params=pltpu.CompilerParams(vmem_limit_bytes=VMEM_LIMIT),
    )(a, b)


def _pair_masks(rows):
    lane = lax.broadcasted_iota(jnp.int32, (rows, LANES), 1)
    lo = (lane < HEAD_DIM).astype(F32)
    return lo, 1.0 - lo


def _fold_lanes(x):
    part = x[:, 0:LANES]
    for u in range(1, x.shape[1] // LANES):
        part = part + x[:, u * LANES:(u + 1) * LANES]
    return part


def _pair_masks_t(cols):
    sub = lax.broadcasted_iota(jnp.int32, (LANES, cols), 0)
    lo = (sub < HEAD_DIM).astype(F32)
    return lo, 1.0 - lo


def _tree_rows(x, op, nacc=4):
    parts = [x[i * 8:(i + 1) * 8] for i in range(x.shape[0] // 8)]
    accs = parts[:nacc]
    for i, part in enumerate(parts[nacc:]):
        accs[i % nacc] = op(accs[i % nacc], part)
    while len(accs) > 1:
        accs = [op(a, b) for a, b in zip(accs[0::2], accs[1::2])] + (accs[-1:] if len(accs) % 2 else [])
    return accs[0]


def _fold_rows(x):
    return _tree_rows(x, jnp.add)


def _max_rows(x):
    return _tree_rows(x, jnp.maximum)


def _flash_t_update(st, pmax, m, acc, vt1, guard):
    m_new = jnp.maximum(m, jnp.max(pmax, axis=0, keepdims=True))
    m_use = jnp.where(m_new == NEG_INF, 0.0, m_new) if guard else m_new
    alpha = jnp.exp2(m - m_use)
    p = jnp.exp2(st - m_use)
    acc = alpha * acc + jnp.dot(vt1, p.astype(BF16), preferred_element_type=F32)
    return m_new, acc


def _flash_t_init(cols):
    return jnp.full((1, cols), NEG_INF, F32), jnp.zeros((LANES, cols), F32)


def _flash_t_pair_out(acca, accb):
    return jnp.concatenate([acca[:HEAD_DIM] / acca[HEAD_DIM:HEAD_DIM + 1],
                            accb[HEAD_DIM:] / accb[0:1]], axis=0)


def _values_with_ones(vt, lo_b, hi_b):
    return vt * lo_b + hi_b, vt * hi_b + lo_b


FOX_AUG = 6


def _fox_kernel(qt_ref, k_ref, vt_ref, o_ref, s_ref, pmax_ref, *, tq, tk):
    qi = pl.program_id(2)
    q0 = pl.multiple_of(qi * tq, tq)
    jd = q0 // tk
    qt = qt_ref[...].astype(F32)
    sub = lax.broadcasted_iota(jnp.int32, (2 * LANES, tq), 0)
    aug = sub - LANES
    lo = jnp.logical_or(sub < HEAD_DIM, jnp.logical_and(aug >= 0, aug < FOX_AUG))
    hi = jnp.logical_or(jnp.logical_and(sub >= HEAD_DIM, sub < LANES),
                        jnp.logical_and(aug >= FOX_AUG, aug < 2 * FOX_AUG))
    qta = jnp.where(lo, qt, 0.0).astype(BF16)
    qtb = jnp.where(hi, qt, 0.0).astype(BF16)
    key_in = lax.broadcasted_iota(jnp.int32, (tk, tq), 0)
    q_pos = q0 + lax.broadcasted_iota(jnp.int32, (tk, tq), 1)

    def scores(j, slot):
        k0 = pl.multiple_of(j * tk, tk)
        kc = k_ref[pl.ds(k0, tk), :]
        for h, qth in enumerate((qta, qtb)):
            sh = jnp.dot(kc, qth, preferred_element_type=F32)
            s_ref[slot, h] = sh
            pmax_ref[slot, h] = _max_rows(sh)

    lo_b = (lax.broadcasted_iota(jnp.int32, (LANES, tk), 0) < HEAD_DIM).astype(F32).astype(BF16)
    hi_b = (1.0 - lo_b.astype(F32)).astype(BF16)

    def consume(slot, j, carry, diag):
        vts = _values_with_ones(vt_ref[:, pl.ds(pl.multiple_of(j * tk, tk), tk)], lo_b, hi_b)
        out = []
        for h in range(2):
            sh, pm = s_ref[slot, h], pmax_ref[slot, h]
            if diag:
                sh = jnp.where(jd * tk + key_in <= q_pos, sh, NEG_INF)
                pm = _max_rows(sh)
            out.append(_flash_t_update(sh, pm, *carry[h], vts[h], False))
        return tuple(out)

    def pair_trip(i, carry):
        scores(2 * i + 1, 1)
        carry = consume(0, 2 * i, carry, False)
        scores(2 * i + 2, 0)
        return consume(1, 2 * i + 1, carry, False)

    scores(0, 0)
    carry = lax.fori_loop(0, jd // 2, pair_trip, (_flash_t_init(tq), _flash_t_init(tq)))

    def odd_tail(carry):
        scores(jd, 1)
        carry = consume(0, jd - 1, carry, False)
        return consume(1, jd, carry, True)

    def even_tail(carry):
        return consume(0, jd, carry, True)

    (_, acca), (_, accb) = lax.cond(jd % 2 == 1, odd_tail, even_tail, carry)
    o_ref[...] = _flash_t_pair_out(acca, accb).astype(BF16)


def _fox_operands(qt, k, c):
    n = k.shape[0]
    npair = H_FOX // 2

    def chop(v):
        bits = lax.bitcast_convert_type(v, jnp.uint32) & jnp.uint32(0xFFFF0000)
        return lax.bitcast_convert_type(bits, F32)

    def split(v):
        hi = chop(v)
        mid = chop(v - hi)
        lo = chop(v - hi - mid)
        return jnp.stack([hi, mid, lo], axis=-1).astype(BF16)

    pos = split(c)
    neg = split(-c)
    ones = jnp.ones((n, H_FOX, 3), BF16)
    pad = jnp.zeros((n, npair, LANES - 2 * FOX_AUG), BF16)
    k_aug = jnp.concatenate([neg, ones], axis=-1).reshape(n, npair, 2 * FOX_AUG)
    q_aug = jnp.concatenate([ones, pos], axis=-1).reshape(n, npair, 2 * FOX_AUG)
    k_full = jnp.concatenate([k.reshape(n, npair, LANES), k_aug, pad], axis=-1)
    q_aug = jnp.transpose(jnp.concatenate([q_aug, pad], axis=-1), (1, 2, 0))
    qt_full = jnp.concatenate([qt.reshape(npair, LANES, n), q_aug], axis=1)
    return qt_full.reshape(npair * 2 * LANES, n), k_full.reshape(n, npair * 2 * LANES)


def _fox_attention(qt, k, vt, b, t_len, tq, tk):
    npair = H_FOX // 2
    nq = t_len // tq
    return pl.pallas_call(
        functools.partial(_fox_kernel, tq=tq, tk=tk),
        name="fox_attention",
        grid=(b, npair, nq),
        in_specs=[pl.BlockSpec((2 * LANES, tq), lambda bi, p, i: (p, bi * nq + i)),
                  pl.BlockSpec((t_len, 2 * LANES), lambda bi, p, i: (bi, p)),
                  pl.BlockSpec((LANES, t_len), lambda bi, p, i: (p, bi))],
        out_specs=pl.BlockSpec((LANES, tq), lambda bi, p, i: (p, bi * nq + i)),
        out_shape=jax.ShapeDtypeStruct(vt.shape, BF16),
        scratch_shapes=[pltpu.VMEM((2, 2, tk, tq), F32), pltpu.VMEM((2, 2, 8, tq), F32)],
        compiler_params=_cparams(("parallel", "parallel", "arbitrary")),
    )(qt, k, vt)


def _sb_kernel(q_ref, k_ref, v_ref, o_ref, *, t):
    qi = pl.program_id(1)
    lo, hi = _pair_masks(t)
    row = lax.broadcasted_iota(jnp.int32, (t, t), 0)
    col = lax.broadcasted_iota(jnp.int32, (t, t), 1)
    strict = col < row
    after = (row > col).astype(BF16)

    qs = []
    for p in range(H_SB // 2):
        q2 = q_ref[:, p * LANES:(p + 1) * LANES].astype(F32)
        qs += [(q2 * lo).astype(BF16), (q2 * hi).astype(BF16)]

    def one(qh, kc, vc, run, acc, diag):
        z = _nt_dot(qh, kc)
        tl = _softplus_tail(z)
        log_beta = jnp.minimum(z, 0.0) - tl
        log_1mb = -jnp.maximum(z, 0.0) - tl
        if diag:
            log_1mb = jnp.where(strict, log_1mb, 0.0)
        h1, h2, h3 = _split3(log_1mb)
        tail = (jnp.dot(h1, after, preferred_element_type=F32)
                + jnp.dot(h2, after, preferred_element_type=F32)
                + jnp.dot(h3, after, preferred_element_type=F32))
        a = jnp.exp(log_beta + (tail + run))
        if diag:
            a = jnp.where(strict, a, 0.0)
        acc = acc + jnp.dot(a.astype(BF16), vc, preferred_element_type=F32)
        run = run + (tail[:, 0:1] + log_1mb[:, 0:1])
        return run, acc

    def chunk(j, carry, diag):
        k0 = pl.multiple_of(j * t, t)
        out = []
        for hh in range(H_SB):
            cols = slice((hh // 2) * LANES, (hh // 2 + 1) * LANES)
            out.append(one(qs[hh], k_ref[pl.ds(k0, t), cols], v_ref[pl.ds(k0, t), cols],
                           *carry[hh], diag))
        return tuple(out)

    zero = (jnp.zeros((t, 1), F32), jnp.zeros((t, LANES), F32))
    carry = chunk(qi, (zero,) * H_SB, True)

    def live(state):
        j, c = state
        top = c[0][0]
        for run, _ in c[1:]:
            top = jnp.maximum(top, run)
        return jnp.logical_and(j >= 0, jnp.max(top) > SB_DEAD)

    def older(state):
        j, c = state
        return j - 1, chunk(j, c, False)

    _, carry = lax.while_loop(live, older, (qi - 1, carry))
    outs = [carry[2 * p][1] * lo + carry[2 * p + 1][1] * hi for p in range(H_SB // 2)]
    o_ref[...] = jnp.concatenate(outs, axis=1).astype(BF16)


def _sb_attention(q, k, v, b, t_len, t):
    nq = t_len // t
    return pl.pallas_call(
        functools.partial(_sb_kernel, t=t),
        name="stick_breaking_attention",
        grid=(b, nq),
        in_specs=[pl.BlockSpec((t, W_SB), lambda bi, i: (bi * nq + i, 0)),
                  pl.BlockSpec((t_len, W_SB), lambda bi, i: (bi, 0)),
                  pl.BlockSpec((t_len, W_SB), lambda bi, i: (bi, 0))],
        out_specs=pl.BlockSpec((t, W_SB), lambda bi, i: (bi * nq + i, 0)),
        out_shape=jax.ShapeDtypeStruct(q.shape, BF16),
        compiler_params=_cparams(("parallel", "arbitrary")),
    )(q, k, v)


def _sort_key(x, folded=False):
    if not folded:
        x = jnp.where(x == 0.0, 0.0, x)
    bits = pltpu.bitcast(x, jnp.int32)
    return bits ^ ((bits >> 31) & 0x7FFFFFFF)


def _top_half(x):
    bits = pltpu.bitcast(x, jnp.int32) & jnp.int32(-65536)
    return pltpu.bitcast(bits, F32).astype(BF16)


def _select_topk_bias(key_ref, nch, tk, nq, topk, keys_on_lanes, hi_ref=None):
    key_axis = 1 if keys_on_lanes else 0
    vec = (nq, 1) if keys_on_lanes else (1, nq)

    def load(k0):
        return key_ref[:, pl.ds(k0, tk)] if keys_on_lanes else key_ref[pl.ds(k0, tk), :]

    def store(k0, val):
        if keys_on_lanes:
            key_ref[:, pl.ds(k0, tk)] = val
        else:
            key_ref[pl.ds(k0, tk), :] = val

    fold = _fold_lanes if keys_on_lanes else _fold_rows

    def count(pred):
        def body(c, acc):
            k0 = pl.multiple_of(c * tk, tk)
            if keys_on_lanes or tk <= LANES:
                return acc + fold(jnp.where(pred(load(k0), k0), 1.0, 0.0))
            for u in range(tk // LANES):
                ku = k0 + u * LANES
                slab = key_ref[pl.ds(pl.multiple_of(ku, LANES), LANES), :]
                acc = acc + _tree_rows(jnp.where(pred(slab, ku), 1.0, 0.0), jnp.add, nacc=2)
            return acc
        part = (nq, LANES) if keys_on_lanes else (8, nq)
        acc = lax.fori_loop(0, nch, body, jnp.zeros(part, F32))
        return jnp.sum(acc, axis=key_axis, keepdims=True)

    kf = float(topk)

    def value_bit(i, ans):
        cand = ans | jnp.left_shift(jnp.int32(1), 31 - i)
        cand_s = cand ^ INT_MIN
        cnt = count(lambda blk, k0: blk >= cand_s)
        return jnp.where(cnt >= kf, cand, ans)

    first = 0
    ans = jnp.zeros(vec, jnp.int32)
    if hi_ref is not None:
        one_b, zero_b = jnp.ones((), BF16), jnp.zeros((), BF16)
        total = jnp.asarray(nch * tk, F32)
        below_all = INT_MIN + 0x007FFFFF

        def count_hi(cand16):
            def body(c, acc):
                k0 = pl.multiple_of(c * tk, tk)
                accs = [None, None]
                for u in range(tk // LANES):
                    slab = hi_ref[pl.ds(pl.multiple_of(k0 + u * LANES, LANES), LANES), :]
                    for i in range(LANES // 16):
                        hit = jnp.where(slab[i * 16:(i + 1) * 16] >= cand16, one_b, zero_b)
                        accs[i % 2] = hit if accs[i % 2] is None else accs[i % 2] + hit
                return acc + (accs[0] + accs[1]).astype(F32)
            acc = lax.fori_loop(0, nch, body, jnp.zeros((16, nq), F32))
            return jnp.sum(acc, axis=0, keepdims=True)

        def hi_bit(i, a):
            cand = a | jnp.left_shift(jnp.int32(1), 31 - i)
            cand_s = cand ^ INT_MIN
            bits = jnp.where(cand_s >= 0, cand_s, cand_s ^ 0x7FFFFFFF)
            tiny = jnp.where(bits < 0, 0, jnp.where((bits & 0x007F0000) != 0, 0x00800000, 0))
            bits = jnp.where((bits & 0x7F800000) == 0, tiny, bits)
            cand16 = _top_half(jnp.broadcast_to(pltpu.bitcast(bits, F32), (16, nq)))
            cnt = jnp.where(cand_s <= below_all, total, count_hi(cand16))
            return jnp.where(cnt >= kf, cand, a)

        first = 16
        ans = lax.fori_loop(0, first, hi_bit, ans)

    ans = lax.fori_loop(first, 32, value_bit, ans)
    thr = ans ^ INT_MIN
    need = kf - count(lambda blk, k0: blk > thr)

    r_i = lax.broadcasted_iota(jnp.int32, (tk, tk), 0)
    c_i = lax.broadcasted_iota(jnp.int32, (tk, tk), 1)
    tri = jnp.where(r_i <= c_i if keys_on_lanes else c_i <= r_i, 1.0, 0.0).astype(BF16)

    def write(c, base):
        k0 = pl.multiple_of(c * tk, tk)
        blk = load(k0)
        eq = blk == thr
        eqb = jnp.where(eq, 1.0, 0.0).astype(BF16)
        if keys_on_lanes:
            rank = jnp.dot(eqb, tri, preferred_element_type=F32)
            total = rank[:, tk - 1:tk]
        else:
            rank = jnp.dot(tri, eqb, preferred_element_type=F32)
            total = rank[tk - 1:tk, :]
        order = jnp.where(blk > thr, 0.0, jnp.where(eq, base + rank, jnp.inf))
        store(k0, pltpu.bitcast(jnp.where(order <= need, 0.0, NEG_INF), jnp.int32))
        return base + total

    lax.fori_loop(0, nch, write, jnp.zeros(vec, F32))


def _dsa_kernel(qt_ref, iqt_ref, misct_ref, ik_ref, k_ref, vt_ref, o_ref, key_ref, hi_ref, s_ref,
                pmax_ref, *, tq, tk, topk):
    qi = pl.program_id(1)
    q0 = qi * tq
    jd = q0 // tk
    nch = jd + 1
    lo, hi = _pair_masks_t(tq)
    key_in = lax.broadcasted_iota(jnp.int32, (tk, tq), 0)
    q_pos = q0 + lax.broadcasted_iota(jnp.int32, (tk, tq), 1)

    misct = misct_ref[...]
    iqt = iqt_ref[...].astype(F32)
    iq_heads = []
    for hh in range(IDX_HEADS):
        pair = iqt[(hh // 2) * LANES:(hh // 2 + 1) * LANES]
        iq_heads.append((pair * (lo if hh % 2 == 0 else hi)).astype(BF16))
    iws = [misct[MISC_IW + hh:MISC_IW + hh + 1] for hh in range(IDX_HEADS)]

    def score_chunk(c, diag):
        k0 = pl.multiple_of(c * tk, tk)
        ikc = ik_ref[pl.ds(k0, tk), :]
        isc = jnp.zeros((tk, tq), F32)
        for hh in range(IDX_HEADS):
            s = jnp.dot(ikc, iq_heads[hh], preferred_element_type=F32)
            isc = isc + iws[hh] * jnp.maximum(s, 0.0)
        if diag:
            isc = jnp.where(k0 + key_in <= q_pos, isc, NEG_INF)
        isc = jnp.where(jnp.abs(isc) < MIN_NORMAL, 0.0, isc)
        key_ref[pl.ds(k0, tk), :] = _sort_key(isc, folded=True)
        hi_ref[pl.ds(k0, tk), :] = _top_half(isc)

    def score_body(c, _):
        score_chunk(c, False)
        return 0

    lax.fori_loop(0, jd, score_body, 0)
    score_chunk(jd, True)

    _select_topk_bias(key_ref, nch, tk, tq, topk, keys_on_lanes=False, hi_ref=hi_ref)

    qs = []
    for p in range(H_DSA // 2):
        qt = qt_ref[p * LANES:(p + 1) * LANES, :].astype(F32)
        qs += [(qt * lo).astype(BF16), (qt * hi).astype(BF16)]

    def pair_cols(hh):
        return slice((hh // 2) * LANES, (hh // 2 + 1) * LANES)

    def scores(c, slot):
        k0 = pl.multiple_of(c * tk, tk)
        bias = pltpu.bitcast(key_ref[pl.ds(k0, tk), :], F32)
        for hh in range(H_DSA):
            st = jnp.dot(k_ref[pl.ds(k0, tk), pair_cols(hh)], qs[hh],
                         preferred_element_type=F32) + bias
            s_ref[slot, hh] = st
            pmax_ref[slot, hh] = _max_rows(st)

    lo_b = (lax.broadcasted_iota(jnp.int32, (LANES, tk), 0) < HEAD_DIM).astype(F32).astype(BF16)
    hi_b = (1.0 - lo_b.astype(F32)).astype(BF16)

    def consume(slot, c, carry, diag):
        k0 = pl.multiple_of(c * tk, tk)
        out = []
        for p in range(H_DSA // 2):
            vts = _values_with_ones(vt_ref[p * LANES:(p + 1) * LANES, pl.ds(k0, tk)], lo_b, hi_b)
            for e in range(2):
                hh = 2 * p + e
                st, pm = s_ref[slot, hh], pmax_ref[slot, hh]
                if diag:
                    st = jnp.where(jd * tk + key_in <= q_pos, st, NEG_INF)
                    pm = _max_rows(st)
                out.append(_flash_t_update(st, pm, *carry[hh], vts[e], True))
        return tuple(out)

    def pair_trip(i, carry):
        scores(2 * i + 1, 1)
        carry = consume(0, 2 * i, carry, False)
        scores(2 * i + 2, 0)
        return consume(1, 2 * i + 1, carry, False)

    scores(0, 0)
    carry = lax.fori_loop(0, jd // 2, pair_trip, tuple(_flash_t_init(tq) for _ in range(H_DSA)))

    def odd_tail(carry):
        scores(jd, 1)
        carry = consume(0, jd - 1, carry, False)
        return consume(1, jd, carry, True)

    def even_tail(carry):
        return consume(0, jd, carry, True)

    carry = lax.cond(jd % 2 == 1, odd_tail, even_tail, carry)
    outs = [_flash_t_pair_out(carry[2 * p][1], carry[2 * p + 1][1]) for p in range(H_DSA // 2)]
    o_ref[...] = jnp.concatenate(outs, axis=0).astype(BF16)


def _dsa_attention(qt, iqt, misct, ikb, k, vt, b, t_len, tq, tk, topk):
    nq = t_len // tq
    colblk = lambda bi, i: (0, bi * nq + i)
    per_b = lambda bi, i: (bi, 0)
    return pl.pallas_call(
        functools.partial(_dsa_kernel, tq=tq, tk=tk, topk=topk),
        name="dsa_attention",
        grid=(b, nq),
        in_specs=[pl.BlockSpec((W_DSA, tq), colblk), pl.BlockSpec((W_IDX, tq), colblk),
                  pl.BlockSpec((LANES, tq), colblk), pl.BlockSpec((t_len, LANES), per_b),
                  pl.BlockSpec((t_len, W_DSA), per_b),
                  pl.BlockSpec((W_DSA, t_len), lambda bi, i: (0, bi))],
        out_specs=pl.BlockSpec((W_DSA, tq), colblk),
        out_shape=jax.ShapeDtypeStruct(qt.shape, BF16),
        scratch_shapes=[pltpu.VMEM((t_len, tq), jnp.int32), pltpu.VMEM((t_len, tq), BF16),
                        pltpu.VMEM((2, H_DSA, tk, tq), F32), pltpu.VMEM((2, H_DSA, 8, tq), F32)],
        compiler_params=_cparams(("parallel", "arbitrary")),
    )(qt, iqt, misct, ikb, k, vt)


def _mem_kernel(q_ref, mkv_ref, o_ref, *, tq):
    lo, hi = _pair_masks(tq)
    outs = []
    for p in range(H_MEM // 2):
        q2 = q_ref[:, p * LANES:(p + 1) * LANES].astype(F32)
        mk = mkv_ref[0, :, p * LANES:(p + 1) * LANES].astype(BF16)
        mv = mkv_ref[0, :, W_MEM + p * LANES:W_MEM + (p + 1) * LANES].astype(BF16)

        def head(qh):
            s = _nt_dot(qh, mk)
            pr = jnp.exp(s - jnp.max(s, axis=-1, keepdims=True))
            l = jnp.sum(pr, axis=-1, keepdims=True)
            return jnp.dot(pr.astype(BF16), mv, preferred_element_type=F32) / l

        outs.append(head((q2 * lo).astype(BF16)) * lo + head((q2 * hi).astype(BF16)) * hi)
    o_ref[...] = jnp.concatenate(outs, axis=1).astype(BF16)


def _mem_attention(q, mkv, b, t_len, tq):
    nq = t_len // tq
    n_mem = mkv.shape[1]
    return pl.pallas_call(
        functools.partial(_mem_kernel, tq=tq),
        grid=(b, nq),
        in_specs=[pl.BlockSpec((tq, W_MEM), lambda bi, i: (bi * nq + i, 0)),
                  pl.BlockSpec((1, n_mem, 2 * W_MEM), lambda bi, i: (bi, 0, 0))],
        out_specs=pl.BlockSpec((tq, W_MEM), lambda bi, i: (bi * nq + i, 0)),
        out_shape=jax.ShapeDtypeStruct(q.shape, BF16),
        compiler_params=_cparams(("parallel", "arbitrary")),
    )(q, mkv)


def _rms(x, g):
    return x * lax.rsqrt(jnp.mean(x * x, axis=-1, keepdims=True) + EPS) * g


def _merge_kernel(x_ref, of_ref, od_ref, os_ref, om_ref, gpre_ref, gpost_ref, wg_ref,
                  wf_ref, wd_ref, ws_ref, wm_ref, wo_ref, y_ref, *, transposed):
    x = x_ref[...]
    d = x.shape[1]
    hb = _rms(x, gpre_ref[...]).astype(BF16)
    merged = None
    for i, (o_ref, w_ref) in enumerate(((of_ref, wf_ref), (od_ref, wd_ref),
                                        (os_ref, ws_ref), (om_ref, wm_ref))):
        gate = jax.nn.sigmoid(_nt_dot(hb, wg_ref[i * d:(i + 1) * d, :]))
        o = o_ref[...]
        if transposed and i < 2:
            o = o.astype(F32).T
        br = jnp.dot(o.astype(BF16), w_ref[...], preferred_element_type=F32)
        merged = gate * br if merged is None else merged + gate * br
    y = jnp.dot(merged.astype(BF16), wo_ref[...], preferred_element_type=F32)
    y_ref[...] = x + _rms(y, gpost_ref[...])


def _merge(x, o_fox, o_dsa, o_sb, o_mem, gpre, gpost, wg, wf, wd, ws, wm, wo, tm, transposed):
    n, d = x.shape
    row = lambda i: (i, 0)
    const = lambda i: (0, 0)
    full = lambda a: pl.BlockSpec(a.shape, const)

    def o_spec(o, can_t):
        if transposed and can_t:
            return pl.BlockSpec((o.shape[0], tm), lambda i: (0, i))
        return pl.BlockSpec((tm, o.shape[1]), row)

    return pl.pallas_call(
        functools.partial(_merge_kernel, transposed=transposed),
        name="branch_merge",
        grid=(n // tm,),
        in_specs=[pl.BlockSpec((tm, d), row), o_spec(o_fox, True), o_spec(o_dsa, True),
                  o_spec(o_sb, False), o_spec(o_mem, False)]
        + [full(a) for a in (gpre, gpost, wg, wf, wd, ws, wm, wo)],
        out_specs=pl.BlockSpec((tm, d), row),
        out_shape=jax.ShapeDtypeStruct((n, d), F32),
        compiler_params=_cparams(("parallel",)),
    )(x, o_fox, o_dsa, o_sb, o_mem, gpre, gpost, wg, wf, wd, ws, wm, wo)


def _ffn_kernel(x_ref, gpre_ref, gpost_ref, wi_ref, wo_ref, y_ref, *, d_ff, tc):
    x = x_ref[...]
    hb = _rms(x, gpre_ref[...]).astype(BF16)
    y = jnp.zeros(x.shape, F32)
    for c0 in range(0, d_ff, tc):
        gate = jnp.dot(hb, wi_ref[:, c0:c0 + tc], preferred_element_type=F32)
        up = jnp.dot(hb, wi_ref[:, d_ff + c0:d_ff + c0 + tc], preferred_element_type=F32)
        act = (gate * jax.nn.sigmoid(gate)) * up
        y = y + jnp.dot(act.astype(BF16), wo_ref[c0:c0 + tc, :], preferred_element_type=F32)
    y_ref[...] = x + _rms(y, gpost_ref[...])


def _ffn(x, gpre, gpost, wi, wo, tm):
    n, d = x.shape
    d_ff = wo.shape[0]
    tc = 256 if d_ff % 256 == 0 else d_ff
    row = lambda i: (i, 0)
    const = lambda i: (0, 0)
    return pl.pallas_call(
        functools.partial(_ffn_kernel, d_ff=d_ff, tc=tc),
        name="swiglu_ffn",
        grid=(n // tm,),
        in_specs=[pl.BlockSpec((tm, d), row), pl.BlockSpec(gpre.shape, const),
                  pl.BlockSpec(gpost.shape, const), pl.BlockSpec(wi.shape, const),
                  pl.BlockSpec(wo.shape, const)],
        out_specs=pl.BlockSpec((tm, d), row),
        out_shape=jax.ShapeDtypeStruct((n, d), F32),
        compiler_params=_cparams(("parallel",)),
    )(x, gpre, gpost, wi, wo)


def _dec_score_kernel(pt_ref, iqh_ref, iwb_ref, iknew_ref, *rest, npages):
    pages = rest[:npages]
    o_ref = rest[npages]
    iqh = iqh_ref[0]
    iwb = iwb_ref[0]
    for j in range(npages):
        s = jnp.dot(iqh, pages[j][0, 0].astype(BF16), preferred_element_type=F32)
        o_ref[0, :, j * LANES:(j + 1) * LANES] = jnp.sum(iwb * jnp.maximum(s, 0.0), axis=0,
                                                         keepdims=True)
    ik_new = iknew_ref[0][:, :IDX_DIM].astype(BF16).astype(F32)
    s_new = jnp.sum(iqh.astype(F32) * ik_new, axis=-1, keepdims=True)
    isc_new = jnp.sum(iwb[:, 0:1] * jnp.maximum(s_new, 0.0), axis=0, keepdims=True)
    lane = lax.broadcasted_iota(jnp.int32, (1, LANES), 1)
    o_ref[0, :, npages * LANES:] = jnp.where(lane == 0, isc_new, NEG_INF)


def _dec_scores(pt_flat, iqh, iwb, ik_new, idxk_pool, layer, nb, npages):
    width = (npages + 1) * LANES
    seq = lambda bi, pt: (bi, 0, 0)

    def page_map(j):
        return lambda bi, pt: (layer, pt[bi * npages + j], 0, 0)

    grid_spec = pltpu.PrefetchScalarGridSpec(
        num_scalar_prefetch=1, grid=(nb,),
        in_specs=[pl.BlockSpec((1, 8, IDX_DIM), seq), pl.BlockSpec((1, 8, LANES), seq),
                  pl.BlockSpec((1, 1, LANES), seq)]
        + [pl.BlockSpec((1, 1, IDX_DIM, PAGE_SIZE), page_map(j)) for j in range(npages)],
        out_specs=pl.BlockSpec((1, 1, width), seq))
    return pl.pallas_call(
        functools.partial(_dec_score_kernel, npages=npages),
        name="decode_index_scores",
        grid_spec=grid_spec,
        out_shape=jax.ShapeDtypeStruct((nb, 1, width), F32),
        compiler_params=_cparams(("arbitrary",)),
    )(pt_flat, iqh, iwb, ik_new, *([idxk_pool] * npages))


def _dec_select_kernel(isc_ref, o_ref, key_ref, *, topk):
    rows, width = isc_ref.shape
    key_ref[...] = _sort_key(isc_ref[...])
    _select_topk_bias(key_ref, width // LANES, LANES, rows, topk, keys_on_lanes=True)
    o_ref[...] = pltpu.bitcast(key_ref[...], F32)


def _dec_select(isc, topk):
    return pl.pallas_call(
        functools.partial(_dec_select_kernel, topk=topk),
        out_shape=jax.ShapeDtypeStruct(isc.shape, F32),
        scratch_shapes=[pltpu.VMEM(isc.shape, jnp.int32)],
        compiler_params=pltpu.CompilerParams(vmem_limit_bytes=VMEM_LIMIT),
    )(isc)


def _head_rows(width):
    sub = lax.broadcasted_iota(jnp.int32, (8, width), 0)
    lane = lax.broadcasted_iota(jnp.int32, (8, width), 1)
    return ((lane >> 6) == sub).astype(F32)


def _dec_attn_kernel(pt_ref, fq_ref, dq_ref, sq_ref, mq_ref, fnew_ref, dnew_ref, lnew_ref,
                     bias_ref, mem_ref, *rest, npages):
    fox_pages = rest[0:npages]
    logf_pages = rest[npages:2 * npages]
    dsa_pages = rest[2 * npages:3 * npages]
    sb_pages = rest[3 * npages:4 * npages]
    of_ref, od_ref, os_ref, om_ref = rest[4 * npages:]

    row = lax.broadcasted_iota(jnp.int32, (PAGE_SIZE, PAGE_SIZE), 0)
    col = lax.broadcasted_iota(jnp.int32, (PAGE_SIZE, PAGE_SIZE), 1)
    after = (row > col).astype(BF16)

    past = npages * PAGE_SIZE

    def keys_t(pages):
        return jnp.concatenate([pg[0, 0, 0].astype(BF16) for pg in pages], axis=1)

    def values_t(pages):
        return jnp.concatenate([pg[0, 0, 1].astype(BF16) for pg in pages], axis=1)

    def page_rows(x):
        return jnp.concatenate([x[:, j * PAGE_SIZE:(j + 1) * PAGE_SIZE] for j in range(npages)],
                               axis=0)

    def suffix_sums(x_rows, run):
        h1, h2, h3 = _split3(x_rows)
        inner = (jnp.dot(h1, after, preferred_element_type=F32)
                 + jnp.dot(h2, after, preferred_element_type=F32)
                 + jnp.dot(h3, after, preferred_element_type=F32))
        tot = jnp.sum(x_rows, axis=-1, keepdims=True)
        outs = [None] * npages
        for j in range(npages - 1, -1, -1):
            outs[j] = inner[j * 8:(j + 1) * 8] + run
            run = run + tot[j * 8:(j + 1) * 8]
        return jnp.concatenate(outs, axis=1)

    def head_diag(o, mask):
        return jnp.sum(o * mask, axis=0, keepdims=True)

    def softmax_pv(s_past, s_new, vt_all, v_new):
        m = jnp.maximum(jnp.max(s_past, axis=-1, keepdims=True), s_new)
        p_new = jnp.exp(s_new - m)
        pr = jnp.exp(s_past - m)
        l = p_new + jnp.sum(pr, axis=-1, keepdims=True)
        acc = p_new.astype(BF16).astype(F32) * v_new + _nt_dot(pr.astype(BF16), vt_all)
        return acc / l

    mask_f = _head_rows(W_FOX)
    qf = (fq_ref[0].astype(F32) * mask_f).astype(BF16)
    k_new = fnew_ref[0][:, :W_FOX].astype(BF16).astype(F32)
    v_new = fnew_ref[0][:, W_FOX:].astype(BF16).astype(F32)
    s_new = jnp.sum(qf.astype(F32) * k_new, axis=-1, keepdims=True)
    lf_rows = jnp.concatenate([pg[0, 0] for pg in logf_pages], axis=0)
    decay = suffix_sums(lf_rows, lnew_ref[0][:, 0:1])
    s_past = jnp.dot(qf, keys_t(fox_pages), preferred_element_type=F32) + decay
    of_ref[0] = head_diag(softmax_pv(s_past, s_new, values_t(fox_pages), v_new), mask_f)

    mask_d = _head_rows(W_DSA)
    qd = (dq_ref[0].astype(F32) * mask_d).astype(BF16)
    k_new = dnew_ref[0][:, :W_DSA].astype(BF16).astype(F32)
    v_new = dnew_ref[0][:, W_DSA:].astype(BF16).astype(F32)
    s_new = (jnp.sum(qd.astype(F32) * k_new, axis=-1, keepdims=True)
             + bias_ref[0][:, past:past + 1])
    s_past = jnp.dot(qd, keys_t(dsa_pages), preferred_element_type=F32) + bias_ref[0][:, :past]
    od_ref[0] = head_diag(softmax_pv(s_past, s_new, values_t(dsa_pages), v_new), mask_d)

    mask_s = _head_rows(W_SB)
    qs = (sq_ref[0].astype(F32) * mask_s).astype(BF16)
    z = jnp.dot(qs, keys_t(sb_pages), preferred_element_type=F32)
    tl = _softplus_tail(z)
    log_beta = jnp.minimum(z, 0.0) - tl
    log_1mb = -jnp.maximum(z, 0.0) - tl
    tail = suffix_sums(page_rows(log_1mb), jnp.zeros((8, 1), F32))
    a = jnp.exp(log_beta + tail)
    os_ref[0] = head_diag(_nt_dot(a.astype(BF16), values_t(sb_pages)), mask_s)

    mask_m = _head_rows(W_MEM)
    qm = (mq_ref[0].astype(F32) * mask_m).astype(BF16)
    s = jnp.dot(qm, mem_ref[0, 0, 0].astype(BF16), preferred_element_type=F32)
    pr = jnp.exp(s - jnp.max(s, axis=-1, keepdims=True))
    l = jnp.sum(pr, axis=-1, keepdims=True)
    o = _nt_dot(pr.astype(BF16), mem_ref[0, 0, 1].astype(BF16))
    om_ref[0] = head_diag(o / l, mask_m)


def _dec_attention(pt_flat, fq, dq, sq, mq, fnew, dnew, lnew, bias, mem, fox_pool, logf_pool,
                   dsa_pool, sb_pool, layer, nb, npages):
    seq = lambda bi, pt: (bi, 0, 0)

    def page_map(j, nd):
        return lambda bi, pt: (layer, pt[bi * npages + j]) + (0,) * nd

    def pages(pool):
        blk = (1, 1) + pool.shape[2:]
        return [pl.BlockSpec(blk, page_map(j, len(blk) - 2)) for j in range(npages)]

    def seq_spec(a):
        return pl.BlockSpec((1,) + a.shape[1:], seq)

    mem_spec = pl.BlockSpec((1, 1) + mem.shape[2:], lambda bi, pt: (layer, bi, 0, 0, 0))
    grid_spec = pltpu.PrefetchScalarGridSpec(
        num_scalar_prefetch=1, grid=(nb,),
        in_specs=[seq_spec(a) for a in (fq, dq, sq, mq, fnew, dnew, lnew, bias)] + [mem_spec]
        + pages(fox_pool) + pages(logf_pool) + pages(dsa_pool) + pages(sb_pool),
        out_specs=[pl.BlockSpec((1, 1, wd), seq) for wd in (W_FOX, W_DSA, W_SB, W_MEM)])
    return pl.pallas_call(
        functools.partial(_dec_attn_kernel, npages=npages),
        name="decode_attention",
        grid_spec=grid_spec,
        out_shape=[jax.ShapeDtypeStruct((nb, 1, wd), F32) for wd in (W_FOX, W_DSA, W_SB, W_MEM)],
        compiler_params=_cparams(("arbitrary",)),
    )(pt_flat, fq, dq, sq, mq, fnew, dnew, lnew, bias, mem,
      *([fox_pool] * npages), *([logf_pool] * npages), *([dsa_pool] * npages),
      *([sb_pool] * npages))


def _rope_tables(pos):
    rd = HEAD_DIM // 4
    half = rd // 2
    inv_freq = ROPE_THETA ** (-jnp.arange(half, dtype=F32) * 2.0 / rd)
    ang = pos.astype(F32)[:, None] * inv_freq[None, :]
    cos, sin = jnp.cos(ang), jnp.sin(ang)
    n = pos.shape[0]
    one = jnp.ones((n, HEAD_DIM - rd), F32)
    zero = jnp.zeros((n, HEAD_DIM - rd), F32)
    zh = jnp.zeros((n, half), F32)
    cos64 = jnp.concatenate([cos, cos, one], axis=1)
    sa64 = jnp.concatenate([-sin, zh, zero], axis=1)
    sb64 = jnp.concatenate([zh, sin, zero], axis=1)
    dup = lambda a: jnp.concatenate([a, a], axis=1)
    return dup(cos64), dup(sa64), dup(sb64), cos.T, sin.T


def _prep_w_in(w_in_l, b_forget_l, d_model):
    offs = np.cumsum([0, W_FOX, W_FOX, W_FOX, H_FOX, W_DSA, W_DSA, W_DSA, W_IDX, IDX_DIM,
                      IDX_HEADS, W_SB, W_SB, W_SB, W_MEM])
    (o_fq, o_fk, o_fv, o_ff, o_dq, o_dk, o_dv, o_iq, o_ik, o_iw, o_sq, o_sk, o_sv, o_mq,
     o_g) = [int(v) for v in offs]
    wt = jnp.transpose(w_in_l)
    sl = lambda o, n: wt[o:o + n]
    ik = sl(o_ik, IDX_DIM)
    misc = jnp.concatenate([sl(o_ff, H_FOX), sl(o_iw, IDX_HEADS),
                            jnp.zeros((LANES - H_FOX - IDX_HEADS, d_model), F32)], axis=0)
    wp = jnp.concatenate([sl(o_fq, 3 * W_FOX), sl(o_dq, 3 * W_DSA), sl(o_iq, W_IDX),
                          sl(o_sq, 3 * W_SB), sl(o_mq, W_MEM), ik, ik, misc], axis=0)
    wg = wt[o_g:]
    bfp = jnp.concatenate([b_forget_l, jnp.zeros((LANES - H_FOX,), F32)])[None, :]
    return wp.astype(BF16), wg.astype(BF16), bfp


def kernel(x_prompt, x_sample, cache_fox_kv, cache_fox_logf, cache_dsa_kv, cache_dsa_idxk,
           cache_sb_kv, cache_mem_kv, page_table, mem_prompt, w_in, b_forget, w_mem_kv,
           w_br_fox, w_br_dsa, w_br_sb, w_br_mem, w_out, w_ffn_in, w_ffn_out,
           g_mix_pre, g_mix_post, g_ffn_pre, g_ffn_post):
    bp, t_len, d = x_prompt.shape
    nb = x_sample.shape[0]
    depth = w_in.shape[0]
    npages = page_table.shape[1]
    past_len = npages * PAGE_SIZE
    n_pool = cache_fox_kv.shape[1]
    n_mem = mem_prompt.shape[1]
    n_p = bp * t_len

    tm = min(256, t_len)
    t_att = min(256, t_len)
    tq_dsa = min(256, t_len)
    tk_dsa = min(512, t_len)
    topk_p = min(DSA_TOPK_MAX, t_len // 4)
    topk_s = min(DSA_TOPK_MAX, (past_len + 1) // 4)
    assert tk_dsa >= topk_p and t_len % tk_dsa == 0 and t_len % tm == 0

    tabs_p = _rope_tables(jnp.arange(t_len, dtype=jnp.int32))
    tabs_s = _rope_tables(jnp.full((nb,), past_len, jnp.int32))
    pt_flat = page_table.reshape(-1).astype(jnp.int32)

    def kv_view(cache, width):
        view = jnp.transpose(cache, (0, 1, 3, 4, 5, 2))
        return view.reshape(cache.shape[:2] + (2, width, cache.shape[2]))

    fox_t = kv_view(cache_fox_kv, W_FOX)
    dsa_t = kv_view(cache_dsa_kv, W_DSA)
    sb_t = kv_view(cache_sb_kv, W_SB)
    mem_t = kv_view(cache_mem_kv, W_MEM)
    idxk_t = jnp.transpose(cache_dsa_idxk, (0, 1, 3, 2))
    logf_t = jnp.pad(jnp.transpose(cache_fox_logf, (0, 1, 3, 2)),
                     ((0, 0), (0, 0), (0, 8 - H_FOX), (0, 0)))

    xp = x_prompt.reshape(n_p, d)
    xs = x_sample.reshape(nb, d)
    mem_flat = mem_prompt.reshape(bp * n_mem, d)

    rows_p, rows_s, mem_p = [], [], []
    stacked_p = None
    for l in range(depth):
        wp, wg, bfp = _prep_w_in(w_in[l], b_forget[l], d)
        gpre, gpost = g_mix_pre[l][None, :], g_mix_post[l][None, :]
        fpre, fpost = g_ffn_pre[l][None, :], g_ffn_post[l][None, :]
        wf, wd_, ws, wm = (w.astype(BF16) for w in (w_br_fox[l], w_br_dsa[l], w_br_sb[l], w_br_mem[l]))
        wo = w_out[l].astype(BF16)
        wi, wfo = w_ffn_in[l].astype(BF16), w_ffn_out[l].astype(BF16)

        mem_kv = _matmul(mem_flat, w_mem_kv[l].astype(BF16))
        (fqt, fkv_st, fk, fvt, dqt, dkv_st, dk, dvt, iqt, sq, skv_st, sk, sv, mq, ik_st, ikb,
         misc, misct) = _proj(xp, gpre, wp, tabs_p, bfp, tm, True, stacked_p, l, depth)
        stacked_p = (fkv_st, dkv_st, skv_st, ik_st)
        logf = misc[:, :H_FOX].reshape(bp, t_len, H_FOX)
        c = jnp.cumsum(logf, axis=1) * LOG2_E
        fqt_full, fk_full = _fox_operands(fqt, fk, c.reshape(n_p, H_FOX))
        o_fox = _fox_attention(fqt_full, fk_full, fvt, bp, t_len, t_att, tk_dsa)
        o_dsa = _dsa_attention(dqt, iqt, misct, ikb, dk, dvt, bp, t_len, tq_dsa, tk_dsa, topk_p)
        o_sb = _sb_attention(sq, sk, sv, bp, t_len, t_att)
        o_mem = _mem_attention(mq, mem_kv.reshape(bp, n_mem, 2 * W_MEM), bp, t_len, tm)
        xp = _merge(xp, o_fox, o_dsa, o_sb, o_mem, gpre, gpost, wg, wf, wd_, ws, wm, wo, tm, True)
        xp = _ffn(xp, fpre, fpost, wi, wfo, tm)
        rows_p.append(logf)
        mem_p.append(mem_kv.reshape(bp, n_mem, 2, H_MEM, HEAD_DIM))

        (fq, fkv, fk, fv, dq, dkv, dk, dv, iq, sq, skv, sk, sv, mq, ik32, ikb,
         misc) = _proj(xs, gpre, wp, tabs_s, bfp, nb, False)
        r3 = lambda a: a.reshape(nb, 1, a.shape[1])
        iqh = jnp.pad(iq.reshape(nb, IDX_HEADS, IDX_DIM), ((0, 0), (0, 8 - IDX_HEADS), (0, 0)))
        iwb = jnp.pad(misc[:, MISC_IW:MISC_IW + IDX_HEADS], ((0, 0), (0, 8 - IDX_HEADS)))
        iwb = jnp.broadcast_to(iwb[:, :, None], (nb, 8, LANES))
        isc = _dec_scores(pt_flat, iqh, iwb, r3(ik32), idxk_t, l, nb, npages)
        bias = _dec_select(isc.reshape(nb, -1), topk_s).reshape(nb, 1, -1)
        lnew = jnp.pad(misc[:, :H_FOX], ((0, 0), (0, 8 - H_FOX)))
        lnew = jnp.broadcast_to(lnew[:, :, None], (nb, 8, LANES))
        o_fox, o_dsa, o_sb, o_mem = _dec_attention(
            pt_flat, r3(fq), r3(dq), r3(sq), r3(mq), r3(fkv), r3(dkv), lnew, bias,
            mem_t, fox_t, logf_t, dsa_t, sb_t, l, nb, npages)
        sq2 = lambda a: a.reshape(nb, a.shape[2])
        xs = _merge(xs, sq2(o_fox), sq2(o_dsa), sq2(o_sb), sq2(o_mem), gpre, gpost, wg, wf, wd_,
                    ws, wm, wo, nb, False)
        xs = _ffn(xs, fpre, fpost, wi, wfo, nb)
        rows_s.append((fkv.reshape(nb, 1, 2, H_FOX, HEAD_DIM), misc[:, :H_FOX].reshape(nb, 1, H_FOX),
                       dkv.reshape(nb, 1, 2, H_DSA, HEAD_DIM),
                       ik32[:, :IDX_DIM].reshape(nb, 1, IDX_DIM),
                       skv.reshape(nb, 1, 2, H_SB, HEAD_DIM)))

    stk = lambda rows, i: jnp.stack([r[i] for r in rows], axis=0)
    fkv_st, dkv_st, skv_st, ik_st = stacked_p

    def kv_rows(buf, heads):
        return jnp.transpose(buf.reshape(depth, bp, 2, heads, HEAD_DIM, t_len), (0, 1, 5, 2, 3, 4))

    return (xp.reshape(bp, t_len, d), xs.reshape(nb, 1, d),
            kv_rows(fkv_st, H_FOX), jnp.stack(rows_p, axis=0), kv_rows(dkv_st, H_DSA),
            jnp.transpose(ik_st, (0, 1, 3, 2)), kv_rows(skv_st, H_SB),
            jnp.stack(mem_p, axis=0),
            stk(rows_s, 0), stk(rows_s, 1), stk(rows_s, 2), stk(rows_s, 3), stk(rows_s, 4))
```
